```python
import jax, jax.numpy as jnp
from jax import lax
import numpy as np

D_MODEL = 2048
BATCH = 4
SEQ = 8192
DEPTH = 1

HEAD_DIM = 64
SWA_Q_HEADS = 16
SWA_KV_HEADS = 2
SWA_GROUP = SWA_Q_HEADS // SWA_KV_HEADS
WINDOW = 128
FOX_HEADS = 16
FOX_BLOCK = 128
D_FF = 4 * D_MODEL
ROPE_THETA = 10000.0
RMS_EPS = 1e-6

SWA_Q_W = SWA_Q_HEADS * HEAD_DIM
SWA_KV_W = SWA_KV_HEADS * HEAD_DIM
FOX_W = FOX_HEADS * HEAD_DIM
IN_WIDTHS = (SWA_Q_W, SWA_KV_W, SWA_KV_W, FOX_W, FOX_W, FOX_W, FOX_HEADS, D_MODEL, D_MODEL)
D_IN = sum(IN_WIDTHS)
IN_SPLITS = tuple(int(v) for v in np.cumsum(IN_WIDTHS)[:-1])

kernel_name = "hybrid_swa_sink_fox_gated_block"


def rmsnorm(x, gain):
    xf = x.astype(jnp.float32)
    out = xf * lax.rsqrt(jnp.mean(xf * xf, axis=-1, keepdims=True) + RMS_EPS) * gain.astype(jnp.float32)
    return out.astype(x.dtype)


def apply_rope(t, positions):
    inv_freq = ROPE_THETA ** (-jnp.arange(0, HEAD_DIM, 2, dtype=jnp.float32) / HEAD_DIM)
    ang = positions.astype(jnp.float32)[..., None] * inv_freq
    cos = jnp.cos(ang)[:, :, None, :]
    sin = jnp.sin(ang)[:, :, None, :]
    tf = t.astype(jnp.float32)
    t1, t2 = tf[..., : HEAD_DIM // 2], tf[..., HEAD_DIM // 2:]
    out = jnp.concatenate([t1 * cos - t2 * sin, t2 * cos + t1 * sin], axis=-1)
    return out.astype(t.dtype)


def sliding_window_gqa_sinks(q, k, v, sinks):
    B, S = q.shape[0], q.shape[1]
    nb = S // WINDOW
    scale = HEAD_DIM ** -0.5
    qb = q.reshape(B, nb, WINDOW, SWA_KV_HEADS, SWA_GROUP, HEAD_DIM)
    kb = k.reshape(B, nb, WINDOW, SWA_KV_HEADS, HEAD_DIM)
    vb = v.reshape(B, nb, WINDOW, SWA_KV_HEADS, HEAD_DIM)
    pad = ((0, 0), (1, 0), (0, 0), (0, 0), (0, 0))
    kk = jnp.concatenate([jnp.pad(kb, pad)[:, :-1], kb], axis=2)
    vv = jnp.concatenate([jnp.pad(vb, pad)[:, :-1], vb], axis=2)
    logits = jnp.einsum('bnqhgd,bnkhd->bnhgqk', qb, kk).astype(jnp.float32) * scale
    blk = jnp.arange(nb)[:, None, None]
    qi = jnp.arange(WINDOW)[None, :, None] + WINDOW
    kj = jnp.arange(2 * WINDOW)[None, None, :]
    diff = qi - kj
    allowed = (diff >= 0) & (diff < WINDOW) & (blk * WINDOW + kj - WINDOW >= 0)
    logits = jnp.where(allowed[None, :, None, None], logits, -jnp.inf)
    sink_col = jnp.broadcast_to(
        sinks.astype(jnp.float32).reshape(SWA_KV_HEADS, SWA_GROUP)[None, None, :, :, None, None],
        logits.shape[:-1] + (1,))
    probs = jax.nn.softmax(jnp.concatenate([logits, sink_col], axis=-1), axis=-1)[..., :-1]
    out = jnp.einsum('bnhgqk,bnkhd->bnqhgd', probs.astype(v.dtype), vv)
    return out.reshape(B, S, SWA_Q_HEADS * HEAD_DIM)


def forgetting_attention(q, k, v, log_f):
    B, S = q.shape[0], q.shape[1]
    nb = S // FOX_BLOCK
    scale = HEAD_DIM ** -0.5
    qh = jnp.transpose(q, (0, 2, 1, 3))
    kh = jnp.transpose(k, (0, 2, 1, 3))
    vh = jnp.transpose(v, (0, 2, 1, 3))
    c = jnp.cumsum(jnp.transpose(log_f, (0, 2, 1)), axis=-1)
    key_pos = jnp.arange(S)

    def block(i):
        start = i * FOX_BLOCK
        qi = lax.dynamic_slice_in_dim(qh, start, FOX_BLOCK, axis=2)
        ci = lax.dynamic_slice_in_dim(c, start, FOX_BLOCK, axis=2)
        logits = jnp.einsum('bhqd,bhkd->bhqk', qi, kh).astype(jnp.float32) * scale
        logits = logits + (ci[..., :, None] - c[..., None, :])
        qpos = start + jnp.arange(FOX_BLOCK)
        causal = key_pos[None, :] <= qpos[:, None]
        logits = jnp.where(causal[None, None], logits, -jnp.inf)
        probs = jax.nn.softmax(logits, axis=-1)
        return jnp.einsum('bhqk,bhkd->bhqd', probs.astype(vh.dtype), vh)

    out = lax.map(block, jnp.arange(nb))
    out = jnp.transpose(out, (1, 0, 3, 2, 4))
    return out.reshape(B, S, FOX_HEADS * HEAD_DIM)


def setup_inputs(seed: int = 0) -> dict:
    key = jax.random.key(seed)
    ks = jax.random.split(key, 14)
    f32 = jnp.float32
    x = jax.random.normal(ks[0], (BATCH, SEQ, D_MODEL), f32)
    positions = jnp.broadcast_to(jnp.arange(SEQ, dtype=jnp.int32)[None, :], (BATCH, SEQ))
    attn_norm = 1.0 + 0.05 * jax.random.normal(ks[1], (DEPTH, D_MODEL), f32)
    w_in = jax.random.normal(ks[2], (DEPTH, D_MODEL, D_IN), f32) * D_MODEL ** -0.5
    fox_f_bias = jax.random.uniform(ks[3], (DEPTH, FOX_HEADS), f32, 1.0, 6.0)
    swa_sinks = 0.5 * jax.random.normal(ks[4], (DEPTH, SWA_Q_HEADS), f32)
    w_branch_swa = jax.random.normal(ks[5], (DEPTH, SWA_Q_W, D_MODEL), f32) * SWA_Q_W ** -0.5
    w_branch_fox = jax.random.normal(ks[6], (DEPTH, FOX_W, D_MODEL), f32) * FOX_W ** -0.5
    w_out = jax.random.normal(ks[7], (DEPTH, D_MODEL, D_MODEL), f32) * D_MODEL ** -0.5
    mlp_norm = 1.0 + 0.05 * jax.random.normal(ks[8], (DEPTH, D_MODEL), f32)
    w_up = jax.random.normal(ks[9], (DEPTH, D_MODEL, D_FF), f32) * D_MODEL ** -0.5
    w_down = jax.random.normal(ks[10], (DEPTH, D_FF, D_MODEL), f32) * D_FF ** -0.5
    final_norm = 1.0 + 0.05 * jax.random.normal(ks[11], (D_MODEL,), f32)
    return {"x": x, "positions": positions, "attn_norm": attn_norm, "w_in": w_in,
            "fox_f_bias": fox_f_bias, "swa_sinks": swa_sinks, "w_branch_swa": w_branch_swa,
            "w_branch_fox": w_branch_fox, "w_out": w_out, "mlp_norm": mlp_norm,
            "w_up": w_up, "w_down": w_down, "final_norm": final_norm}


def reference(x, positions, attn_norm, w_in, fox_f_bias, swa_sinks, w_branch_swa,
              w_branch_fox, w_out, mlp_norm, w_up, w_down, final_norm):
    B, S = x.shape[0], x.shape[1]
    for l in range(DEPTH):
        h = rmsnorm(x, attn_norm[l])
        proj = jnp.einsum('bsd,de->bse', h, w_in[l])
        (a_q, a_k, a_v, f_q, f_k, f_v, f_logit, g_a, g_b) = jnp.split(proj, IN_SPLITS, axis=-1)
        a_q = apply_rope(a_q.reshape(B, S, SWA_Q_HEADS, HEAD_DIM), positions)
        a_k = apply_rope(a_k.reshape(B, S, SWA_KV_HEADS, HEAD_DIM), positions)
        a_v = a_v.reshape(B, S, SWA_KV_HEADS, HEAD_DIM)
        o_a = sliding_window_gqa_sinks(a_q, a_k, a_v, swa_sinks[l])
        log_f = jax.nn.log_sigmoid(f_logit.astype(jnp.float32) + fox_f_bias[l].astype(jnp.float32))
        o_b = forgetting_attention(f_q.reshape(B, S, FOX_HEADS, HEAD_DIM),
                                   f_k.reshape(B, S, FOX_HEADS, HEAD_DIM),
                                   f_v.reshape(B, S, FOX_HEADS, HEAD_DIM), log_f)
        merged = (jax.nn.sigmoid(g_a) * jnp.einsum('bse,ed->bsd', o_a, w_branch_swa[l])
                  + jax.nn.sigmoid(g_b) * jnp.einsum('bse,ed->bsd', o_b, w_branch_fox[l]))
        x = x + jnp.einsum('bsd,de->bse', merged, w_out[l])
        h = rmsnorm(x, mlp_norm[l])
        u = jax.nn.relu(jnp.einsum('bsd,df->bsf', h, w_up[l]))
        x = x + jnp.einsum('bsf,fd->bsd', u * u, w_down[l])
    return rmsnorm(x, final_norm)
```

```python
import functools

import numpy as np
import jax
import jax.numpy as jnp
from jax import lax
from jax.experimental import pallas as pl
from jax.experimental.pallas import tpu as pltpu

F32 = jnp.float32
BF16 = jnp.bfloat16

D_MODEL = 2048
HEAD_DIM = 64
HALF = HEAD_DIM // 2
SWA_Q_HEADS = 16
SWA_KV_HEADS = 2
SWA_GROUP = SWA_Q_HEADS // SWA_KV_HEADS
WINDOW = 128
FOX_HEADS = 16
D_FF = 4 * D_MODEL
ROPE_THETA = 10000.0
RMS_EPS = 1e-6
SWA_Q_W = SWA_Q_HEADS * HEAD_DIM
FOX_W = FOX_HEADS * HEAD_DIM

LANES = 128
NEG = -1e30
MIB = 1024 * 1024

COL_AQ = 0
COL_FQ = 1024
COL_FK = 2048
COL_FV = 3072
COL_GA = 4096
COL_GB = 6144
COL_TAIL = 8192
TAIL_W = 512
D_PROJ = COL_TAIL + TAIL_W

TM_IN = 1024
TN_IN = 512
NJ_IN = D_PROJ // TN_IN
RN_ROWS = 128

TQ_SWA = 512
TQ_FOX = 512
TK_FOX = 512
TM_MERGE = 512
TM_MLP = 512
TF_MLP = 512


def _rmsnorm_rows(x, gain):
    ms = jnp.mean(x * x, axis=-1, keepdims=True)
    return x * lax.rsqrt(ms + RMS_EPS) * gain


def _inproj_kernel(pos_ref, invf_ref, x_ref, gain_ref, w_ref, fbias_ref,
                   proj_ref, logf_ref, h_sc, cos_sc, sa_sc, sb_sc):
    j = pl.program_id(1)

    @pl.when(j == 0)
    def _():
        def body(r, _):
            rows = pl.ds(pl.multiple_of(r * RN_ROWS, RN_ROWS), RN_ROWS)
            h_sc[rows, :] = _rmsnorm_rows(x_ref[rows, :], gain_ref[...]).astype(BF16)
            ang = pos_ref[rows, :].astype(F32) * invf_ref[...]
            c = jnp.cos(ang)
            s = jnp.sin(ang)
            lane = lax.broadcasted_iota(jnp.int32, ang.shape, 1)
            first = (lane % HEAD_DIM) < HALF
            cos_sc[rows, :] = c
            sa_sc[rows, :] = jnp.where(first, -s, 0.0)
            sb_sc[rows, :] = jnp.where(first, 0.0, s)
            return 0
        lax.fori_loop(0, TM_IN // RN_ROWS, body, 0)

    acc = jnp.dot(h_sc[...], w_ref[...], preferred_element_type=F32)

    def rope_store(c0, c1):
        for c in range(c0, c1, LANES):
            blk = acc[:, c:c + LANES]
            out = (blk * cos_sc[...]
                   + pltpu.roll(blk, LANES - HALF, 1) * sa_sc[...]
                   + pltpu.roll(blk, HALF, 1) * sb_sc[...])
            proj_ref[:, c:c + LANES] = out.astype(BF16)

    @pl.when(j < COL_FQ // TN_IN)
    def _():
        rope_store(0, TN_IN)

    @pl.when((j >= COL_FQ // TN_IN) & (j < COL_GA // TN_IN))
    def _():
        proj_ref[...] = acc.astype(BF16)

    @pl.when((j >= COL_GA // TN_IN) & (j < COL_TAIL // TN_IN))
    def _():
        proj_ref[...] = (1.0 / (1.0 + jnp.exp(-acc))).astype(BF16)

    @pl.when(j == COL_TAIL // TN_IN)
    def _():
        rope_store(0, 2 * LANES)
        proj_ref[:, 2 * LANES:3 * LANES] = acc[:, 2 * LANES:3 * LANES].astype(BF16)
        z = acc[:, 3 * LANES:4 * LANES] + fbias_ref[...]
        proj_ref[:, 3 * LANES:4 * LANES] = z.astype(BF16)
        lf = jnp.minimum(z, 0.0) - jnp.log1p(jnp.exp(-jnp.abs(z)))
        logf_ref[...] = lf.T[:FOX_HEADS, :]


def _inproj(x2, pos2, invf, gain, w_all, fbias, batch, seq):
    t = x2.shape[0]
    nsb = seq // TM_IN
    return pl.pallas_call(
        _inproj_kernel,
        grid=(t // TM_IN, NJ_IN),
        in_specs=[
            pl.BlockSpec((TM_IN, 1), lambda i, j: (i, 0)),
            pl.BlockSpec((1, LANES), lambda i, j: (0, 0)),
            pl.BlockSpec((TM_IN, D_MODEL), lambda i, j: (i, 0)),
            pl.BlockSpec((1, D_MODEL), lambda i, j: (0, 0)),
            pl.BlockSpec((D_MODEL, TN_IN), lambda i, j: (0, j)),
            pl.BlockSpec((1, LANES), lambda i, j: (0, 0)),
        ],
        out_specs=[
            pl.BlockSpec((TM_IN, TN_IN), lambda i, j: (i, j)),
            pl.BlockSpec((None, FOX_HEADS, TM_IN), lambda i, j: (i // nsb, 0, i % nsb)),
        ],
        out_shape=[
            jax.ShapeDtypeStruct((t, D_PROJ), BF16),
            jax.ShapeDtypeStruct((batch, FOX_HEADS, seq), F32),
        ],
        scratch_shapes=[
            pltpu.VMEM((TM_IN, D_MODEL), BF16),
            pltpu.VMEM((TM_IN, LANES), F32),
            pltpu.VMEM((TM_IN, LANES), F32),
            pltpu.VMEM((TM_IN, LANES), F32),
        ],
        compiler_params=pltpu.CompilerParams(
            dimension_semantics=("arbitrary", "arbitrary"),
            vmem_limit_bytes=48 * MIB),
        name="inproj",
    )(pos2, invf, x2, gain, w_all, fbias)


def _cumsum_kernel(x_ref, o_ref):
    y = x_ref[...]
    n = y.shape[1]
    idx = lax.broadcasted_iota(jnp.int32, y.shape, 1)
    d = 1
    while d < n:
        y = y + jnp.where(idx >= d, pltpu.roll(y, d, 1), 0.0)
        d *= 2
    o_ref[...] = y


def _cumsum_rows(x):
    rows, n = x.shape
    return pl.pallas_call(
        _cumsum_kernel,
        grid=(rows // 8,),
        in_specs=[pl.BlockSpec((8, n), lambda i: (i, 0))],
        out_specs=pl.BlockSpec((8, n), lambda i: (i, 0)),
        out_shape=jax.ShapeDtypeStruct((rows, n), F32),
        name="cumsum",
    )(x)


def _swa_kernel(sinks_ref, q_ref, cur_ref, prev_ref, o_ref, kv_sc):
    i = pl.program_id(1)
    kvw = 3 * LANES
    kv_sc[0:WINDOW, :] = prev_ref[:, 0:kvw]
    kv_sc[WINDOW:WINDOW + TQ_SWA, :] = cur_ref[:, 0:kvw]

    row = lax.broadcasted_iota(jnp.int32, (WINDOW, 2 * WINDOW), 0)
    col = lax.broadcasted_iota(jnp.int32, (WINDOW, 2 * WINDOW), 1)
    allowed = (col > row) & (col <= row + WINDOW)
    bias_any = jnp.where(allowed, 0.0, NEG)
    bias_first = jnp.where(allowed & (col >= WINDOW), 0.0, NEG)
    bias_r0 = jnp.where(i == 0, bias_first, bias_any)
    low = lax.broadcasted_iota(jnp.int32, (WINDOW, LANES), 1) < HEAD_DIM

    for r in range(TQ_SWA // WINDOW):
        bias = bias_r0 if r == 0 else bias_any
        rows = slice(r * WINDOW, (r + 1) * WINDOW)
        win = slice(r * WINDOW, (r + 2) * WINDOW)
        v_pair = kv_sc[win, 2 * LANES:3 * LANES]
        for p in range(SWA_Q_HEADS // 2):
            q_pair = q_ref[rows, p * LANES:(p + 1) * LANES]
            g = (2 * p) // SWA_GROUP
            k_dup = kv_sc[win, g * LANES:(g + 1) * LANES]
            halves = []
            for hh in range(2):
                h = 2 * p + hh
                q_h = jnp.where(low if hh == 0 else ~low, q_pair, jnp.zeros_like(q_pair))
                s = lax.dot_general(q_h, k_dup, (((1,), (1,)), ((), ())),
                                    preferred_element_type=F32) + bias
                sink = sinks_ref[h]
                m = jnp.maximum(jnp.max(s, axis=1, keepdims=True), sink)
                pr = jnp.exp(s - m)
                den = jnp.sum(pr, axis=1, keepdims=True) + jnp.exp(sink - m)
                o = jnp.dot(pr.astype(BF16), v_pair, preferred_element_type=F32)
                o = o * (1.0 / den)
                if g != hh:
                    o = pltpu.roll(o, HEAD_DIM, 1)
                halves.append(o)
            o_ref[rows, p * LANES:(p + 1) * LANES] = jnp.where(low, halves[0], halves[1]).astype(BF16)


def _swa(proj, sinks, batch, seq):
    t = proj.shape[0]
    nsq = seq // TQ_SWA
    per = TQ_SWA // WINDOW
    tail_blk = COL_TAIL // TAIL_W
    return pl.pallas_call(
        _swa_kernel,
        grid=(batch, nsq),
        in_specs=[
            pl.BlockSpec(memory_space=pltpu.SMEM),
            pl.BlockSpec((TQ_SWA, SWA_Q_W), lambda b, i: (b * nsq + i, 0)),
            pl.BlockSpec((TQ_SWA, TAIL_W), lambda b, i: (b * nsq + i, tail_blk)),
            pl.BlockSpec((WINDOW, TAIL_W),
                         lambda b, i: (jnp.maximum((b * nsq + i) * per - 1, 0), tail_blk)),
        ],
        out_specs=pl.BlockSpec((TQ_SWA, SWA_Q_W), lambda b, i: (b * nsq + i, 0)),
        scratch_shapes=[pltpu.VMEM((WINDOW + TQ_SWA, 3 * LANES), BF16)],
        out_shape=jax.ShapeDtypeStruct((t, SWA_Q_W), BF16),
        compiler_params=pltpu.CompilerParams(
            dimension_semantics=("arbitrary", "arbitrary")),
        name="swa",
    )(sinks, proj, proj, proj)


def _fox_kernel(q_ref, k_ref, v_ref, c_ref, o_ref, m_sc, l_sc, acc_sc):
    qi = pl.program_id(2)
    q = q_ref[...]
    low = lax.broadcasted_iota(jnp.int32, (TQ_FOX, LANES), 1) < HEAD_DIM
    zero = jnp.zeros_like(q)
    q_heads = (jnp.where(low, q, zero), jnp.where(low, zero, q))
    c_q = c_ref[:, pl.ds(pl.multiple_of(qi * TQ_FOX, TQ_FOX), LANES)]

    m_sc[...] = jnp.full(m_sc.shape, NEG, F32)
    l_sc[...] = jnp.zeros(l_sc.shape, F32)
    acc_sc[...] = jnp.zeros(acc_sc.shape, F32)

    def step(j, masked):
        ks = pl.ds(pl.multiple_of(j * TK_FOX, TK_FOX), TK_FOX)
        k = k_ref[ks, :]
        v = v_ref[ks, :]
        c_k = c_ref[:, ks]
        for h in range(2):
            s = lax.dot_general(q_heads[h], k, (((1,), (1,)), ((), ())),
                                preferred_element_type=F32)
            s = s + (c_q[h:h + 1, 0:1] - c_k[h:h + 1, :])
            if masked:
                row = lax.broadcasted_iota(jnp.int32, s.shape, 0)
                col = lax.broadcasted_iota(jnp.int32, s.shape, 1)
                s = jnp.where(col <= row, s, NEG)
            m_prev = m_sc[h]
            m_new = jnp.maximum(m_prev, jnp.max(s, axis=1, keepdims=True))
            alpha = jnp.exp(m_prev - m_new)
            pr = jnp.exp(s - m_new[:, 0:1])
            l_sc[h] = alpha * l_sc[h] + jnp.sum(pr, axis=1, keepdims=True)
            acc_sc[h] = alpha * acc_sc[h] + jnp.dot(pr.astype(BF16), v, preferred_element_type=F32)
            m_sc[h] = m_new

    def body(j, carry):
        step(j, False)
        return carry

    lax.fori_loop(0, qi, body, 0)
    step(qi, True)

    o0 = acc_sc[0] * (1.0 / l_sc[0])
    o1 = acc_sc[1] * (1.0 / l_sc[1])
    o_ref[...] = jnp.where(low, o0, o1).astype(BF16)


def _fox(proj, c4, batch, seq):
    t = proj.shape[0]
    nq = seq // TQ_FOX
    pairs = FOX_HEADS // 2
    return pl.pallas_call(
        _fox_kernel,
        grid=(batch, pairs, nq),
        in_specs=[
            pl.BlockSpec((TQ_FOX, LANES), lambda b, p, i: (b * nq + i, COL_FQ // LANES + p)),
            pl.BlockSpec((seq, LANES), lambda b, p, i: (b, COL_FK // LANES + p)),
            pl.BlockSpec((seq, LANES), lambda b, p, i: (b, COL_FV // LANES + p)),
            pl.BlockSpec((None, None, 2, seq), lambda b, p, i: (b, p, 0, 0)),
        ],
        out_specs=pl.BlockSpec((TQ_FOX, LANES), lambda b, p, i: (b * nq + i, p)),
        out_shape=jax.ShapeDtypeStruct((t, FOX_W), BF16),
        scratch_shapes=[
            pltpu.VMEM((2, TQ_FOX, LANES), F32),
            pltpu.VMEM((2, TQ_FOX, LANES), F32),
            pltpu.VMEM((2, TQ_FOX, LANES), F32),
        ],
        compiler_params=pltpu.CompilerParams(
            dimension_semantics=("arbitrary", "arbitrary", "arbitrary")),
        name="fox",
    )(proj, proj, proj, c4)


def _merge_kernel(oa_ref, ob_ref, ga_ref, gb_ref, x_ref, wa_ref, wb_ref, wo_ref, gain_ref,
                  x1_ref, h2_ref):
    ya = jnp.dot(oa_ref[...], wa_ref[...], preferred_element_type=F32)
    yb = jnp.dot(ob_ref[...], wb_ref[...], preferred_element_type=F32)
    merged = ga_ref[...].astype(F32) * ya + gb_ref[...].astype(F32) * yb
    x1 = x_ref[...] + jnp.dot(merged.astype(BF16), wo_ref[...], preferred_element_type=F32)
    x1_ref[...] = x1
    h2_ref[...] = _rmsnorm_rows(x1, gain_ref[...]).astype(BF16)


def _merge(o_a, o_b, proj, x2, wa, wb, wo, gain):
    t = x2.shape[0]
    tm = TM_MERGE
    const = dict(pipeline_mode=pl.Buffered(1))
    return pl.pallas_call(
        _merge_kernel,
        grid=(t // tm,),
        in_specs=[
            pl.BlockSpec((tm, SWA_Q_W), lambda i: (i, 0)),
            pl.BlockSpec((tm, FOX_W), lambda i: (i, 0)),
            pl.BlockSpec((tm, D_MODEL), lambda i: (i, COL_GA // D_MODEL)),
            pl.BlockSpec((tm, D_MODEL), lambda i: (i, COL_GB // D_MODEL)),
            pl.BlockSpec((tm, D_MODEL), lambda i: (i, 0)),
            pl.BlockSpec((SWA_Q_W, D_MODEL), lambda i: (0, 0), **const),
            pl.BlockSpec((FOX_W, D_MODEL), lambda i: (0, 0), **const),
            pl.BlockSpec((D_MODEL, D_MODEL), lambda i: (0, 0), **const),
            pl.BlockSpec((1, D_MODEL), lambda i: (0, 0), **const),
        ],
        out_specs=[
            pl.BlockSpec((tm, D_MODEL), lambda i: (i, 0)),
            pl.BlockSpec((tm, D_MODEL), lambda i: (i, 0)),
        ],
        out_shape=[
            jax.ShapeDtypeStruct((t, D_MODEL), F32),
            jax.ShapeDtypeStruct((t, D_MODEL), BF16),
        ],
        compiler_params=pltpu.CompilerParams(
            dimension_semantics=("arbitrary",),
            vmem_limit_bytes=56 * MIB),
        name="merge",
    )(o_a, o_b, proj, proj, x2, wa, wb, wo, gain)


def _mlp_kernel(h2_ref, x1_ref, wup_ref, wdn_ref, gain_ref, o_ref, *, final_norm):
    f = pl.program_id(1)

    @pl.when(f == 0)
    def _():
        o_ref[...] = x1_ref[...]

    u = jnp.maximum(jnp.dot(h2_ref[...], wup_ref[...], preferred_element_type=F32), 0.0)
    o_ref[...] += jnp.dot((u * u).astype(BF16), wdn_ref[...], preferred_element_type=F32)

    if final_norm:
        @pl.when(f == pl.num_programs(1) - 1)
        def _():
            o_ref[...] = _rmsnorm_rows(o_ref[...], gain_ref[...])


def _mlp(h2, x1, wup, wdn, gain, final_norm):
    t = x1.shape[0]
    tm, tf = TM_MLP, TF_MLP
    return pl.pallas_call(
        functools.partial(_mlp_kernel, final_norm=final_norm),
        grid=(t // tm, D_FF // tf),
        in_specs=[
            pl.BlockSpec((tm, D_MODEL), lambda i, f: (i, 0)),
            pl.BlockSpec((tm, D_MODEL), lambda i, f: (i, 0)),
            pl.BlockSpec((D_MODEL, tf), lambda i, f: (0, f)),
            pl.BlockSpec((tf, D_MODEL), lambda i, f: (f, 0)),
            pl.BlockSpec((1, D_MODEL), lambda i, f: (0, 0)),
        ],
        out_specs=pl.BlockSpec((tm, D_MODEL), lambda i, f: (i, 0)),
        out_shape=jax.ShapeDtypeStruct((t, D_MODEL), F32),
        compiler_params=pltpu.CompilerParams(
            dimension_semantics=("arbitrary", "arbitrary"),
            vmem_limit_bytes=48 * MIB),
        name="mlp",
    )(h2, x1, wup, wdn, gain)


def _pack_w_in(w, scale):
    o = 0
    a_q = w[:, o:o + SWA_Q_W] * scale; o += SWA_Q_W
    a_k = w[:, o:o + SWA_KV_HEADS * HEAD_DIM]; o += SWA_KV_HEADS * HEAD_DIM
    a_v = w[:, o:o + SWA_KV_HEADS * HEAD_DIM]; o += SWA_KV_HEADS * HEAD_DIM
    f_q = w[:, o:o + FOX_W] * scale; o += FOX_W
    f_k = w[:, o:o + FOX_W]; o += FOX_W
    f_v = w[:, o:o + FOX_W]; o += FOX_W
    f_l = w[:, o:o + FOX_HEADS]; o += FOX_HEADS
    g_a = w[:, o:o + D_MODEL]; o += D_MODEL
    g_b = w[:, o:o + D_MODEL]; o += D_MODEL
    k0, k1 = a_k[:, :HEAD_DIM], a_k[:, HEAD_DIM:]
    pad = jnp.zeros((w.shape[0], LANES - FOX_HEADS), w.dtype)
    return jnp.concatenate([a_q, f_q, f_k, f_v, g_a, g_b, k0, k0, k1, k1, a_v, f_l, pad],
                           axis=1).astype(BF16)


def kernel(x, positions, attn_norm, w_in, fox_f_bias, swa_sinks, w_branch_swa, w_branch_fox,
           w_out, mlp_norm, w_up, w_down, final_norm):
    batch, seq, d = x.shape
    depth = w_in.shape[0]
    assert d == D_MODEL and seq % TM_IN == 0 and seq % TQ_FOX == 0
    t = batch * seq
    scale = HEAD_DIM ** -0.5
    inv_freq = ROPE_THETA ** (-jnp.arange(0, HEAD_DIM, 2, dtype=F32) / HEAD_DIM)
    invf = jnp.tile(inv_freq, LANES // HALF)[None, :]
    pos2 = positions.reshape(t, 1)
    x2 = x.reshape(t, d)
    for l in range(depth):
        w_all = _pack_w_in(w_in[l], scale)
        fbias = jnp.pad(fox_f_bias[l].astype(F32), (0, LANES - FOX_HEADS))[None, :]
        proj, logf_t = _inproj(x2, pos2, invf, attn_norm[l][None, :].astype(F32), w_all, fbias,
                               batch, seq)
        c = _cumsum_rows(logf_t.reshape(batch * FOX_HEADS, seq))
        c4 = c.reshape(batch, FOX_HEADS // 2, 2, seq)
        o_a = _swa(proj, swa_sinks[l].astype(F32), batch, seq)
        o_b = _fox(proj, c4, batch, seq)
        x1, h2 = _merge(o_a, o_b, proj, x2,
                        w_branch_swa[l].astype(BF16), w_branch_fox[l].astype(BF16),
                        w_out[l].astype(BF16), mlp_norm[l][None, :].astype(F32))
        x2 = _mlp(h2, x1, w_up[l].astype(BF16), w_down[l].astype(BF16),
                  final_norm[None, :].astype(F32), final_norm=(l == depth - 1))
    return x2.reshape(batch, seq, d)
```

```python
import functools

import jax
import jax.numpy as jnp
from jax import lax
from jax.experimental import pallas as pl
from jax.experimental.pallas import tpu as pltpu

F32 = jnp.float32
BF16 = jnp.bfloat16

D_MODEL = 2048
HEAD_DIM = 64
HALF = HEAD_DIM // 2
SWA_Q_HEADS = 16
SWA_KV_HEADS = 2
SWA_GROUP = SWA_Q_HEADS // SWA_KV_HEADS
WINDOW = 128
FOX_HEADS = 16
D_FF = 4 * D_MODEL
ROPE_THETA = 10000.0
RMS_EPS = 1e-6
SWA_Q_W = SWA_Q_HEADS * HEAD_DIM
FOX_W = FOX_HEADS * HEAD_DIM

LANES = 128
NEG = -1e30
MIB = 1024 * 1024
LOG2E = 1.4426950408889634

COL_AQ = 0
COL_FQ = 1024
COL_GA = 2048
COL_GB = 4096
COL_FK = 6144
COL_TAIL = 7168
TAIL_W = 512
D_PROJ = COL_TAIL + TAIL_W

TM_IN = 1024
TN_IN = 512
NJ_PROJ = D_PROJ // TN_IN
NJ_IN = NJ_PROJ + FOX_W // TN_IN
RN_ROWS = 128

TQ_SWA = 512
TQ_FOX = 512
TK_FOX = 512
TM_MERGE = 512
TM_MLP = 512
TF_MLP = 512


def _rmsnorm_rows(x, gain):
    ms = jnp.mean(x * x, axis=-1, keepdims=True)
    return x * lax.rsqrt(ms + RMS_EPS) * gain


def _split3(x):
    hi = x.astype(BF16)
    r1 = x - hi.astype(F32)
    mid = r1.astype(BF16)
    lo = (r1 - mid.astype(F32)).astype(BF16)
    return hi, mid, lo


def _inproj_kernel(pos_ref, invf_ref, x_ref, gain_ref, w_ref, wvt_ref, fbias_ref,
                   proj_ref, logf_ref, vt_ref, h_sc, cos_sc, sa_sc, sb_sc):
    j = pl.program_id(1)

    @pl.when(j == 0)
    def _():
        def body(r, _):
            rows = pl.ds(pl.multiple_of(r * RN_ROWS, RN_ROWS), RN_ROWS)
            h_sc[rows, :] = _rmsnorm_rows(x_ref[rows, :], gain_ref[...]).astype(BF16)
            ang = pos_ref[rows, :].astype(F32) * invf_ref[...]
            c = jnp.cos(ang)
            s = jnp.sin(ang)
            lane = lax.broadcasted_iota(jnp.int32, ang.shape, 1)
            first = (lane % HEAD_DIM) < HALF
            cos_sc[rows, :] = c
            sa_sc[rows, :] = jnp.where(first, -s, 0.0)
            sb_sc[rows, :] = jnp.where(first, 0.0, s)
            return 0
        lax.fori_loop(0, TM_IN // RN_ROWS, body, 0)

    def project():
        return jnp.dot(h_sc[...], w_ref[...], preferred_element_type=F32)

    def rope_store(acc, c0, c1):
        for c in range(c0, c1, LANES):
            blk = acc[:, c:c + LANES]
            out = (blk * cos_sc[...]
                   + pltpu.roll(blk, LANES - HALF, 1) * sa_sc[...]
                   + pltpu.roll(blk, HALF, 1) * sb_sc[...])
            proj_ref[:, c:c + LANES] = out.astype(BF16)

    @pl.when(j < COL_FQ // TN_IN)
    def _():
        rope_store(project(), 0, TN_IN)

    plain_q = (j >= COL_FQ // TN_IN) & (j < COL_GA // TN_IN)
    plain_k = (j >= COL_FK // TN_IN) & (j < COL_TAIL // TN_IN)

    @pl.when(plain_q | plain_k)
    def _():
        proj_ref[...] = project().astype(BF16)

    @pl.when((j >= COL_GA // TN_IN) & (j < COL_FK // TN_IN))
    def _():
        proj_ref[...] = (1.0 / (1.0 + jnp.exp(-project()))).astype(BF16)

    @pl.when(j == COL_TAIL // TN_IN)
    def _():
        acc = project()
        rope_store(acc, 0, 2 * LANES)
        proj_ref[:, 2 * LANES:3 * LANES] = acc[:, 2 * LANES:3 * LANES].astype(BF16)
        proj_ref[:, 3 * LANES:4 * LANES] = jnp.zeros((TM_IN, LANES), BF16)
        z = acc[:, 3 * LANES:4 * LANES] + fbias_ref[...]
        logf_ref[...] = jnp.minimum(z, 0.0) - jnp.log1p(jnp.exp(-jnp.abs(z)))

    @pl.when(j >= NJ_PROJ)
    def _():
        vt = lax.dot_general(wvt_ref[...], h_sc[...], (((1,), (1,)), ((), ())),
                             preferred_element_type=F32)
        vt_ref[...] = vt.astype(BF16)


def _inproj(x2, pos2, invf, gain, w_all, w_vt, fbias, batch, seq):
    t = x2.shape[0]
    nsb = seq // TM_IN
    last = NJ_PROJ - 1

    def vt_blk(j):
        return jnp.clip(j - NJ_PROJ, 0, FOX_W // TN_IN - 1)

    return pl.pallas_call(
        _inproj_kernel,
        grid=(t // TM_IN, NJ_IN),
        in_specs=[
            pl.BlockSpec((TM_IN, 1), lambda i, j: (i, 0)),
            pl.BlockSpec((1, LANES), lambda i, j: (0, 0)),
            pl.BlockSpec((TM_IN, D_MODEL), lambda i, j: (i, 0)),
            pl.BlockSpec((1, D_MODEL), lambda i, j: (0, 0)),
            pl.BlockSpec((D_MODEL, TN_IN), lambda i, j: (0, jnp.minimum(j, last))),
            pl.BlockSpec((TN_IN, D_MODEL), lambda i, j: (vt_blk(j), 0)),
            pl.BlockSpec((1, LANES), lambda i, j: (0, 0)),
        ],
        out_specs=[
            pl.BlockSpec((TM_IN, TN_IN), lambda i, j: (i, jnp.minimum(j, last))),
            pl.BlockSpec((TM_IN, LANES), lambda i, j: (i, 0)),
            pl.BlockSpec((None, TN_IN, TM_IN), lambda i, j: (i // nsb, vt_blk(j), i % nsb)),
        ],
        out_shape=[
            jax.ShapeDtypeStruct((t, D_PROJ), BF16),
            jax.ShapeDtypeStruct((t, LANES), F32),
            jax.ShapeDtypeStruct((batch, FOX_W, seq), BF16),
        ],
        scratch_shapes=[
            pltpu.VMEM((TM_IN, D_MODEL), BF16),
            pltpu.VMEM((TM_IN, LANES), F32),
            pltpu.VMEM((TM_IN, LANES), F32),
            pltpu.VMEM((TM_IN, LANES), F32),
        ],
        compiler_params=pltpu.CompilerParams(
            dimension_semantics=("arbitrary", "arbitrary"),
            vmem_limit_bytes=52 * MIB),
        name="inproj",
    )(pos2, invf, x2, gain, w_all, w_vt, fbias)


def _decay_kernel(logf_ref, aug_ref, base_ref, carry_sc):
    sb = pl.program_id(1)

    @pl.when(sb == 0)
    def _():
        carry_sc[...] = jnp.zeros(carry_sc.shape, F32)

    x2 = logf_ref[...] * LOG2E
    row = lax.broadcasted_iota(jnp.int32, (TK_FOX, TK_FOX), 0)
    col = lax.broadcasted_iota(jnp.int32, (TK_FOX, TK_FOX), 1)
    tri = jnp.where(col <= row, 1.0, 0.0).astype(BF16)
    hi, mid, lo = _split3(x2)
    cl = (jnp.dot(tri, hi, preferred_element_type=F32)
          + jnp.dot(tri, mid, preferred_element_type=F32)
          + jnp.dot(tri, lo, preferred_element_type=F32))
    yh, ym, yl = _split3(-cl)
    lane = lax.broadcasted_iota(jnp.int32, (TK_FOX, LANES), 1)
    zero = jnp.zeros((TK_FOX, LANES), F32)
    pieces = jnp.where(lane < FOX_HEADS, yh.astype(F32),
                       jnp.where(lane < 2 * FOX_HEADS, pltpu.roll(ym.astype(F32), FOX_HEADS, 1),
                                 jnp.where(lane < 3 * FOX_HEADS,
                                           pltpu.roll(yl.astype(F32), 2 * FOX_HEADS, 1), zero)))
    aug_ref[...] = pieces.astype(BF16)
    base_ref[...] = -carry_sc[...]
    carry_sc[...] = carry_sc[...] + cl[TK_FOX - 1:TK_FOX, :]


def _decay(logf, batch, seq):
    t = logf.shape[0]
    nb = seq // TK_FOX
    return pl.pallas_call(
        _decay_kernel,
        grid=(batch, nb),
        in_specs=[pl.BlockSpec((TK_FOX, LANES), lambda b, s: (b * nb + s, 0))],
        out_specs=[
            pl.BlockSpec((TK_FOX, LANES), lambda b, s: (b * nb + s, 0)),
            pl.BlockSpec((None, None, 1, LANES), lambda b, s: (b, s, 0, 0)),
        ],
        out_shape=[
            jax.ShapeDtypeStruct((t, LANES), BF16),
            jax.ShapeDtypeStruct((batch, nb, 1, LANES), F32),
        ],
        scratch_shapes=[pltpu.VMEM((1, LANES), F32)],
        compiler_params=pltpu.CompilerParams(
            dimension_semantics=("arbitrary", "arbitrary")),
        name="decay",
    )(logf)


def _swa_kernel(sinks_ref, q_ref, cur_ref, prev_ref, o_ref, kv_sc):
    i = pl.program_id(1)
    kvw = 3 * LANES
    kv_sc[0:WINDOW, :] = prev_ref[:, 0:kvw]
    kv_sc[WINDOW:WINDOW + TQ_SWA, :] = cur_ref[:, 0:kvw]

    row = lax.broadcasted_iota(jnp.int32, (WINDOW, 2 * WINDOW), 0)
    col = lax.broadcasted_iota(jnp.int32, (WINDOW, 2 * WINDOW), 1)
    allowed = (col > row) & (col <= row + WINDOW)
    bias_any = jnp.where(allowed, 0.0, NEG)
    bias_first = jnp.where(allowed & (col >= WINDOW), 0.0, NEG)
    bias_r0 = jnp.where(i == 0, bias_first, bias_any)
    low = lax.broadcasted_iota(jnp.int32, (WINDOW, LANES), 1) < HEAD_DIM

    for r in range(TQ_SWA // WINDOW):
        bias = bias_r0 if r == 0 else bias_any
        rows = slice(r * WINDOW, (r + 1) * WINDOW)
        win = slice(r * WINDOW, (r + 2) * WINDOW)
        v_pair = kv_sc[win, 2 * LANES:3 * LANES]
        for p in range(SWA_Q_HEADS // 2):
            q_pair = q_ref[rows, p * LANES:(p + 1) * LANES]
            g = (2 * p) // SWA_GROUP
            k_dup = kv_sc[win, g * LANES:(g + 1) * LANES]
            halves = []
            for hh in range(2):
                h = 2 * p + hh
                q_h = jnp.where(low if hh == 0 else ~low, q_pair, jnp.zeros_like(q_pair))
                s = lax.dot_general(q_h, k_dup, (((1,), (1,)), ((), ())),
                                    preferred_element_type=F32) + bias
                sink = sinks_ref[h]
                m = jnp.maximum(jnp.max(s, axis=1, keepdims=True), sink)
                pr = jnp.exp(s - m)
                den = jnp.sum(pr, axis=1, keepdims=True) + jnp.exp(sink - m)
                o = jnp.dot(pr.astype(BF16), v_pair, preferred_element_type=F32)
                o = o * (1.0 / den)
                if g != hh:
                    o = pltpu.roll(o, HEAD_DIM, 1)
                halves.append(o)
            o_ref[rows, p * LANES:(p + 1) * LANES] = jnp.where(low, halves[0], halves[1]).astype(BF16)


def _swa(proj, sinks, batch, seq):
    t = proj.shape[0]
    nsq = seq // TQ_SWA
    per = TQ_SWA // WINDOW
    tail_blk = COL_TAIL // TAIL_W
    return pl.pallas_call(
        _swa_kernel,
        grid=(batch, nsq),
        in_specs=[
            pl.BlockSpec(memory_space=pltpu.SMEM),
            pl.BlockSpec((TQ_SWA, SWA_Q_W), lambda b, i: (b * nsq + i, 0)),
            pl.BlockSpec((TQ_SWA, TAIL_W), lambda b, i: (b * nsq + i, tail_blk)),
            pl.BlockSpec((WINDOW, TAIL_W),
                         lambda b, i: (jnp.maximum((b * nsq + i) * per - 1, 0), tail_blk)),
        ],
        out_specs=pl.BlockSpec((TQ_SWA, SWA_Q_W), lambda b, i: (b * nsq + i, 0)),
        scratch_shapes=[pltpu.VMEM((WINDOW + TQ_SWA, 3 * LANES), BF16)],
        out_shape=jax.ShapeDtypeStruct((t, SWA_Q_W), BF16),
        compiler_params=pltpu.CompilerParams(
            dimension_semantics=("arbitrary", "arbitrary")),
        name="swa",
    )(sinks, proj, proj, proj)


def _fox_kernel(base_ref, q_ref, k_ref, aug_ref, vt_ref, o_ref, m_sc, l_sc, acc_sc, qt_sc):
    b = pl.program_id(0)
    p = pl.program_id(1)
    qi = pl.program_id(2)
    nb = k_ref.shape[0] // TK_FOX
    q = q_ref[...]
    lane = lax.broadcasted_iota(jnp.int32, (TQ_FOX, LANES), 1)
    low = lane < HEAD_DIM
    zero = jnp.zeros_like(q)
    for hh in range(2):
        h = 2 * p + hh
        sel = (lane == h) | (lane == h + FOX_HEADS) | (lane == h + 2 * FOX_HEADS)
        ones = jnp.where(sel, 1.0, 0.0)
        q_h = jnp.where(low if hh == 0 else ~low, q, zero).astype(F32)
        q_full = jnp.concatenate([q_h, ones], axis=1)
        qt_sc[hh] = q_full.T.astype(BF16)

    m_sc[...] = jnp.full(m_sc.shape, NEG, F32)
    l_sc[...] = jnp.zeros(l_sc.shape, F32)
    acc_sc[...] = jnp.zeros(acc_sc.shape, F32)

    def step(j, masked):
        ks = pl.ds(pl.multiple_of(j * TK_FOX, TK_FOX), TK_FOX)
        k_full = jnp.concatenate([k_ref[ks, :], aug_ref[ks, :]], axis=1)
        scores = [jnp.dot(k_full, qt_sc[hh], preferred_element_type=F32) for hh in range(2)]
        for hh in range(2):
            h = 2 * p + hh
            base = base_ref[(b * nb + j) * FOX_HEADS + h]
            s = scores[hh]
            if masked:
                row = lax.broadcasted_iota(jnp.int32, s.shape, 0)
                col = lax.broadcasted_iota(jnp.int32, s.shape, 1)
                s = jnp.where(row <= col, s, NEG)
            m_prev = m_sc[hh] - base
            m_new = jnp.maximum(m_prev, jnp.max(s, axis=0, keepdims=True))
            alpha = jnp.exp2(m_prev - m_new)
            pr = jnp.exp2(s - m_new)
            l_sc[hh] = alpha * l_sc[hh] + jnp.sum(pr, axis=0, keepdims=True)
            vt_h = vt_ref[hh * HEAD_DIM:(hh + 1) * HEAD_DIM, ks]
            acc_sc[hh] = alpha * acc_sc[hh] + jnp.dot(vt_h, pr.astype(BF16),
                                                      preferred_element_type=F32)
            m_sc[hh] = m_new + base

    def body(j, carry):
        step(j, False)
        return carry

    lax.fori_loop(0, qi, body, 0)
    step(qi, True)

    o_t = jnp.concatenate([acc_sc[0] * (1.0 / l_sc[0]), acc_sc[1] * (1.0 / l_sc[1])], axis=0)
    o_ref[...] = o_t.T.astype(BF16)


def _fox(proj, aug, base, v_t, batch, seq):
    t = proj.shape[0]
    nq = seq // TQ_FOX
    pairs = FOX_HEADS // 2
    return pl.pallas_call(
        _fox_kernel,
        grid=(batch, pairs, nq),
        in_specs=[
            pl.BlockSpec(memory_space=pltpu.SMEM),
            pl.BlockSpec((TQ_FOX, LANES), lambda b, p, i: (b * nq + i, COL_FQ // LANES + p)),
            pl.BlockSpec((seq, LANES), lambda b, p, i: (b, COL_FK // LANES + p)),
            pl.BlockSpec((seq, LANES), lambda b, p, i: (b, 0)),
            pl.BlockSpec((None, LANES, seq), lambda b, p, i: (b, p, 0)),
        ],
        out_specs=pl.BlockSpec((TQ_FOX, LANES), lambda b, p, i: (b * nq + i, p)),
        out_shape=jax.ShapeDtypeStruct((t, FOX_W), BF16),
        scratch_shapes=[
            pltpu.VMEM((2, 1, TQ_FOX), F32),
            pltpu.VMEM((2, 1, TQ_FOX), F32),
            pltpu.VMEM((2, HEAD_DIM, TQ_FOX), F32),
            pltpu.VMEM((2, 2 * LANES, TQ_FOX), BF16),
        ],
        compiler_params=pltpu.CompilerParams(
            dimension_semantics=("arbitrary", "arbitrary", "arbitrary")),
        name="fox",
    )(base, proj, proj, aug, v_t)


def _merge_kernel(oa_ref, ob_ref, ga_ref, gb_ref, x_ref, wa_ref, wb_ref, wo_ref, gain_ref,
                  x1_ref, h2_ref):
    ya = jnp.dot(oa_ref[...], wa_ref[...], preferred_element_type=F32)
    yb = jnp.dot(ob_ref[...], wb_ref[...], preferred_element_type=F32)
    merged = ga_ref[...].astype(F32) * ya + gb_ref[...].astype(F32) * yb
    x1 = x_ref[...] + jnp.dot(merged.astype(BF16), wo_ref[...], preferred_element_type=F32)
    x1_ref[...] = x1
    h2_ref[...] = _rmsnorm_rows(x1, gain_ref[...]).astype(BF16)


def _merge(o_a, o_b, proj, x2, wa, wb, wo, gain):
    t = x2.shape[0]
    tm = TM_MERGE
    const = dict(pipeline_mode=pl.Buffered(1))
    return pl.pallas_call(
        _merge_kernel,
        grid=(t // tm,),
        in_specs=[
            pl.BlockSpec((tm, SWA_Q_W), lambda i: (i, 0)),
            pl.BlockSpec((tm, FOX_W), lambda i: (i, 0)),
            pl.BlockSpec((tm, D_MODEL), lambda i: (i, COL_GA // D_MODEL)),
            pl.BlockSpec((tm, D_MODEL), lambda i: (i, COL_GB // D_MODEL)),
            pl.BlockSpec((tm, D_MODEL), lambda i: (i, 0)),
            pl.BlockSpec((SWA_Q_W, D_MODEL), lambda i: (0, 0), **const),
            pl.BlockSpec((FOX_W, D_MODEL), lambda i: (0, 0), **const),
            pl.BlockSpec((D_MODEL, D_MODEL), lambda i: (0, 0), **const),
            pl.BlockSpec((1, D_MODEL), lambda i: (0, 0), **const),
        ],
        out_specs=[
            pl.BlockSpec((tm, D_MODEL), lambda i: (i, 0)),
            pl.BlockSpec((tm, D_MODEL), lambda i: (i, 0)),
        ],
        out_shape=[
            jax.ShapeDtypeStruct((t, D_MODEL), F32),
            jax.ShapeDtypeStruct((t, D_MODEL), BF16),
        ],
        compiler_params=pltpu.CompilerParams(
            dimension_semantics=("arbitrary",),
            vmem_limit_bytes=56 * MIB),
        name="merge",
    )(o_a, o_b, proj, proj, x2, wa, wb, wo, gain)


def _mlp_kernel(h2_ref, x1_ref, wup_ref, wdn_ref, gain_ref, o_ref, *, final_norm):
    f = pl.program_id(1)

    @pl.when(f == 0)
    def _():
        o_ref[...] = x1_ref[...]

    u = jnp.maximum(jnp.dot(h2_ref[...], wup_ref[...], preferred_element_type=F32), 0.0)
    o_ref[...] += jnp.dot((u * u).astype(BF16), wdn_ref[...], preferred_element_type=F32)

    if final_norm:
        @pl.when(f == pl.num_programs(1) - 1)
        def _():
            o_ref[...] = _rmsnorm_rows(o_ref[...], gain_ref[...])


def _mlp(h2, x1, wup, wdn, gain, final_norm):
    t = x1.shape[0]
    tm, tf = TM_MLP, TF_MLP
    return pl.pallas_call(
        functools.partial(_mlp_kernel, final_norm=final_norm),
        grid=(t // tm, D_FF // tf),
        in_specs=[
            pl.BlockSpec((tm, D_MODEL), lambda i, f: (i, 0)),
            pl.BlockSpec((tm, D_MODEL), lambda i, f: (i, 0)),
            pl.BlockSpec((D_MODEL, tf), lambda i, f: (0, f)),
            pl.BlockSpec((tf, D_MODEL), lambda i, f: (f, 0)),
            pl.BlockSpec((1, D_MODEL), lambda i, f: (0, 0)),
        ],
        out_specs=pl.BlockSpec((tm, D_MODEL), lambda i, f: (i, 0)),
        out_shape=jax.ShapeDtypeStruct((t, D_MODEL), F32),
        compiler_params=pltpu.CompilerParams(
            dimension_semantics=("arbitrary", "arbitrary"),
            vmem_limit_bytes=48 * MIB),
        name="mlp",
    )(h2, x1, wup, wdn, gain)


def _pack_w_in(w, scale):
    o = 0
    a_q = w[:, o:o + SWA_Q_W] * scale; o += SWA_Q_W
    a_k = w[:, o:o + SWA_KV_HEADS * HEAD_DIM]; o += SWA_KV_HEADS * HEAD_DIM
    a_v = w[:, o:o + SWA_KV_HEADS * HEAD_DIM]; o += SWA_KV_HEADS * HEAD_DIM
    f_q = w[:, o:o + FOX_W] * (scale * LOG2E); o += FOX_W
    f_k = w[:, o:o + FOX_W]; o += FOX_W
    f_v = w[:, o:o + FOX_W]; o += FOX_W
    f_l = w[:, o:o + FOX_HEADS]; o += FOX_HEADS
    g_a = w[:, o:o + D_MODEL]; o += D_MODEL
    g_b = w[:, o:o + D_MODEL]; o += D_MODEL
    k0, k1 = a_k[:, :HEAD_DIM], a_k[:, HEAD_DIM:]
    pad = jnp.zeros((w.shape[0], LANES - FOX_HEADS), w.dtype)
    w_all = jnp.concatenate([a_q, f_q, g_a, g_b, f_k, k0, k0, k1, k1, a_v, f_l, pad],
                            axis=1).astype(BF16)
    return w_all, f_v.T.astype(BF16)


def kernel(x, positions, attn_norm, w_in, fox_f_bias, swa_sinks, w_branch_swa, w_branch_fox,
           w_out, mlp_norm, w_up, w_down, final_norm):
    batch, seq, d = x.shape
    depth = w_in.shape[0]
    assert d == D_MODEL and seq % TM_IN == 0 and seq % TQ_FOX == 0
    t = batch * seq
    scale = HEAD_DIM ** -0.5
    inv_freq = ROPE_THETA ** (-jnp.arange(0, HEAD_DIM, 2, dtype=F32) / HEAD_DIM)
    invf = jnp.tile(inv_freq, LANES // HALF)[None, :]
    pos2 = positions.reshape(t, 1)
    x2 = x.reshape(t, d)
    for l in range(depth):
        w_all, w_vt = _pack_w_in(w_in[l], scale)
        fbias = jnp.pad(fox_f_bias[l].astype(F32), (0, LANES - FOX_HEADS))[None, :]
        proj, logf, v_t = _inproj(x2, pos2, invf, attn_norm[l][None, :].astype(F32), w_all, w_vt,
                                  fbias, batch, seq)
        aug, base = _decay(logf, batch, seq)
        base = base[:, :, 0, :FOX_HEADS].reshape(-1)
        o_a = _swa(proj, swa_sinks[l].astype(F32), batch, seq)
        o_b = _fox(proj, aug, base, v_t, batch, seq)
        x1, h2 = _merge(o_a, o_b, proj, x2,
                        w_branch_swa[l].astype(BF16), w_branch_fox[l].astype(BF16),
                        w_out[l].astype(BF16), mlp_norm[l][None, :].astype(F32))
        x2 = _mlp(h2, x1, w_up[l].astype(BF16), w_down[l].astype(BF16),
                  final_norm[None, :].astype(F32), final_norm=(l == depth - 1))
    return x2.reshape(batch, seq, d)
```

```python
import functools

import jax
import jax.numpy as jnp
from jax import lax
from jax.experimental import pallas as pl
from jax.experimental.pallas import tpu as pltpu

F32 = jnp.float32
BF16 = jnp.bfloat16

D_MODEL = 2048
HEAD_DIM = 64
HALF = HEAD_DIM // 2
SWA_Q_HEADS = 16
SWA_KV_HEADS = 2
SWA_GROUP = SWA_Q_HEADS // SWA_KV_HEADS
WINDOW = 128
FOX_HEADS = 16
D_FF = 4 * D_MODEL
ROPE_THETA = 10000.0
RMS_EPS = 1e-6
SWA_Q_W = SWA_Q_HEADS * HEAD_DIM
FOX_W = FOX_HEADS * HEAD_DIM

LANES = 128
NEG = -1e30
MIB = 1024 * 1024
LOG2E = 1.4426950408889634

COL_AQ = 0
COL_FQ = 1024
COL_GA = 2048
COL_GB = 4096
COL_FK = 6144
COL_TAIL = 7168
TAIL_W = 512
D_PROJ = COL_TAIL + TAIL_W

TM_IN = 1024
TN_IN = 512
NJ_PROJ = D_PROJ // TN_IN
NJ_IN = NJ_PROJ + FOX_W // TN_IN
RN_ROWS = 128

TQ_SWA = 512
TQ_FOX = 1024
TK_FOX = 256
FOX_AHEAD = 2
FOX_ROWS = 16
FOX_DEN_ROWS = 16
TM_MERGE = 512
TM_MLP = 512
TF_MLP = 1024


def _rmsnorm_rows(x, gain):
    ms = jnp.mean(x * x, axis=-1, keepdims=True)
    return x * lax.rsqrt(ms + RMS_EPS) * gain


def _split3(x):
    hi = x.astype(BF16)
    r1 = x - hi.astype(F32)
    mid = r1.astype(BF16)
    lo = (r1 - mid.astype(F32)).astype(BF16)
    return hi, mid, lo


def _inproj_kernel(pos_ref, invf_ref, x_ref, gain_ref, w_ref, wvt_ref, fbias_ref,
                   proj_ref, logf_ref, vt_ref, h_sc, cos_sc, sa_sc, sb_sc):
    j = pl.program_id(1)

    @pl.when(j == 0)
    def _():
        def body(r, _):
            rows = pl.ds(pl.multiple_of(r * RN_ROWS, RN_ROWS), RN_ROWS)
            h_sc[rows, :] = _rmsnorm_rows(x_ref[rows, :], gain_ref[...]).astype(BF16)
            ang = pos_ref[rows, :].astype(F32) * invf_ref[...]
            c = jnp.cos(ang)
            s = jnp.sin(ang)
            lane = lax.broadcasted_iota(jnp.int32, ang.shape, 1)
            first = (lane % HEAD_DIM) < HALF
            cos_sc[rows, :] = c
            sa_sc[rows, :] = jnp.where(first, -s, 0.0)
            sb_sc[rows, :] = jnp.where(first, 0.0, s)
            return 0
        lax.fori_loop(0, TM_IN // RN_ROWS, body, 0)

    def project():
        return jnp.dot(h_sc[...], w_ref[...], preferred_element_type=F32)

    def rope_store(acc, c0, c1):
        for c in range(c0, c1, LANES):
            blk = acc[:, c:c + LANES]
            out = (blk * cos_sc[...]
                   + pltpu.roll(blk, LANES - HALF, 1) * sa_sc[...]
                   + pltpu.roll(blk, HALF, 1) * sb_sc[...])
            proj_ref[:, c:c + LANES] = out.astype(BF16)

    @pl.when(j < COL_FQ // TN_IN)
    def _():
        rope_store(project(), 0, TN_IN)

    @pl.when((j >= COL_FQ // TN_IN) & (j < COL_TAIL // TN_IN))
    def _():
        proj_ref[...] = project().astype(BF16)

    @pl.when(j == COL_TAIL // TN_IN)
    def _():
        acc = project()
        rope_store(acc, 0, 2 * LANES)
        proj_ref[:, 2 * LANES:3 * LANES] = acc[:, 2 * LANES:3 * LANES].astype(BF16)
        proj_ref[:, 3 * LANES:4 * LANES] = jnp.zeros((TM_IN, LANES), BF16)
        z = acc[:, 3 * LANES:4 * LANES] + fbias_ref[...]
        logf_ref[...] = jnp.minimum(z, 0.0) - jnp.log1p(jnp.exp(-jnp.abs(z)))

    @pl.when(j >= NJ_PROJ)
    def _():
        vt = lax.dot_general(wvt_ref[...], h_sc[...], (((1,), (1,)), ((), ())),
                             preferred_element_type=F32)
        vt_ref[...] = vt.astype(BF16)


def _inproj(x2, pos2, invf, gain, w_all, w_vt, fbias, batch, seq):
    t = x2.shape[0]
    nsb = seq // TM_IN
    last = NJ_PROJ - 1

    def vt_blk(j):
        return jnp.clip(j - NJ_PROJ, 0, FOX_W // TN_IN - 1)

    return pl.pallas_call(
        _inproj_kernel,
        grid=(t // TM_IN, NJ_IN),
        in_specs=[
            pl.BlockSpec((TM_IN, 1), lambda i, j: (i, 0)),
            pl.BlockSpec((1, LANES), lambda i, j: (0, 0)),
            pl.BlockSpec((TM_IN, D_MODEL), lambda i, j: (i, 0)),
            pl.BlockSpec((1, D_MODEL), lambda i, j: (0, 0)),
            pl.BlockSpec((D_MODEL, TN_IN), lambda i, j: (0, jnp.minimum(j, last))),
            pl.BlockSpec((TN_IN, D_MODEL), lambda i, j: (vt_blk(j), 0)),
            pl.BlockSpec((1, LANES), lambda i, j: (0, 0)),
        ],
        out_specs=[
            pl.BlockSpec((TM_IN, TN_IN), lambda i, j: (i, jnp.minimum(j, last))),
            pl.BlockSpec((TM_IN, LANES), lambda i, j: (i, 0)),
            pl.BlockSpec((None, TN_IN, TM_IN), lambda i, j: (i // nsb, vt_blk(j), i % nsb)),
        ],
        out_shape=[
            jax.ShapeDtypeStruct((t, D_PROJ), BF16),
            jax.ShapeDtypeStruct((t, LANES), F32),
            jax.ShapeDtypeStruct((batch, FOX_W, seq), BF16),
        ],
        scratch_shapes=[
            pltpu.VMEM((TM_IN, D_MODEL), BF16),
            pltpu.VMEM((TM_IN, LANES), F32),
            pltpu.VMEM((TM_IN, LANES), F32),
            pltpu.VMEM((TM_IN, LANES), F32),
        ],
        compiler_params=pltpu.CompilerParams(
            dimension_semantics=("arbitrary", "arbitrary"),
            vmem_limit_bytes=52 * MIB),
        name="inproj",
    )(pos2, invf, x2, gain, w_all, w_vt, fbias)


def _decay_kernel(logf_ref, aug_ref, base_ref, carry_sc):
    sb = pl.program_id(1)

    @pl.when(sb == 0)
    def _():
        carry_sc[...] = jnp.zeros(carry_sc.shape, F32)

    x2 = logf_ref[...] * LOG2E
    row = lax.broadcasted_iota(jnp.int32, (TK_FOX, TK_FOX), 0)
    col = lax.broadcasted_iota(jnp.int32, (TK_FOX, TK_FOX), 1)
    tri = jnp.where(col <= row, 1.0, 0.0).astype(BF16)
    hi, mid, lo = _split3(x2)
    cl = (jnp.dot(tri, hi, preferred_element_type=F32)
          + jnp.dot(tri, mid, preferred_element_type=F32)
          + jnp.dot(tri, lo, preferred_element_type=F32))
    yh, ym, yl = _split3(-cl)
    lane = lax.broadcasted_iota(jnp.int32, (TK_FOX, LANES), 1)
    zero = jnp.zeros((TK_FOX, LANES), F32)
    pieces = jnp.where(lane < FOX_HEADS, yh.astype(F32),
                       jnp.where(lane < 2 * FOX_HEADS, pltpu.roll(ym.astype(F32), FOX_HEADS, 1),
                                 jnp.where(lane < 3 * FOX_HEADS,
                                           pltpu.roll(yl.astype(F32), 2 * FOX_HEADS, 1), zero)))
    aug_ref[...] = pieces.astype(BF16)
    base_ref[...] = -carry_sc[...]
    carry_sc[...] = carry_sc[...] + cl[TK_FOX - 1:TK_FOX, :]


def _decay(logf, batch, seq):
    t = logf.shape[0]
    nb = seq // TK_FOX
    return pl.pallas_call(
        _decay_kernel,
        grid=(batch, nb),
        in_specs=[pl.BlockSpec((TK_FOX, LANES), lambda b, s: (b * nb + s, 0))],
        out_specs=[
            pl.BlockSpec((TK_FOX, LANES), lambda b, s: (b * nb + s, 0)),
            pl.BlockSpec((None, None, 1, LANES), lambda b, s: (b, s, 0, 0)),
        ],
        out_shape=[
            jax.ShapeDtypeStruct((t, LANES), BF16),
            jax.ShapeDtypeStruct((batch, nb, 1, LANES), F32),
        ],
        scratch_shapes=[pltpu.VMEM((1, LANES), F32)],
        compiler_params=pltpu.CompilerParams(
            dimension_semantics=("arbitrary", "arbitrary")),
        name="decay",
    )(logf)


def _swa_kernel(sinks_ref, q_ref, cur_ref, prev_ref, o_ref, kv_sc):
    i = pl.program_id(1)
    kvw = 3 * LANES
    kv_sc[0:WINDOW, :] = prev_ref[:, 0:kvw]
    kv_sc[WINDOW:WINDOW + TQ_SWA, :] = cur_ref[:, 0:kvw]

    row = lax.broadcasted_iota(jnp.int32, (WINDOW, 2 * WINDOW), 0)
    col = lax.broadcasted_iota(jnp.int32, (WINDOW, 2 * WINDOW), 1)
    allowed = (col > row) & (col <= row + WINDOW)
    bias_any = jnp.where(allowed, 0.0, NEG)
    bias_first = jnp.where(allowed & (col >= WINDOW), 0.0, NEG)
    bias_r0 = jnp.where(i == 0, bias_first, bias_any)
    low = lax.broadcasted_iota(jnp.int32, (WINDOW, LANES), 1) < HEAD_DIM

    for r in range(TQ_SWA // WINDOW):
        bias = bias_r0 if r == 0 else bias_any
        rows = slice(r * WINDOW, (r + 1) * WINDOW)
        win = slice(r * WINDOW, (r + 2) * WINDOW)
        v_pair = kv_sc[win, 2 * LANES:3 * LANES]
        for p in range(SWA_Q_HEADS // 2):
            q_pair = q_ref[rows, p * LANES:(p + 1) * LANES]
            g = (2 * p) // SWA_GROUP
            k_dup = kv_sc[win, g * LANES:(g + 1) * LANES]
            halves = []
            for hh in range(2):
                h = 2 * p + hh
                q_h = jnp.where(low if hh == 0 else ~low, q_pair, jnp.zeros_like(q_pair))
                s = lax.dot_general(q_h, k_dup, (((1,), (1,)), ((), ())),
                                    preferred_element_type=F32) + bias
                sink = sinks_ref[h]
                m = jnp.maximum(jnp.max(s, axis=1, keepdims=True), sink)
                pr = jnp.exp(s - m)
                den = jnp.sum(pr, axis=1, keepdims=True) + jnp.exp(sink - m)
                o = jnp.dot(pr.astype(BF16), v_pair, preferred_element_type=F32)
                o = o * (1.0 / den)
                if g != hh:
                    o = pltpu.roll(o, HEAD_DIM, 1)
                halves.append(o)
            o_ref[rows, p * LANES:(p + 1) * LANES] = jnp.where(low, halves[0], halves[1]).astype(BF16)


def _swa(proj, sinks, batch, seq):
    t = proj.shape[0]
    nsq = seq // TQ_SWA
    per = TQ_SWA // WINDOW
    tail_blk = COL_TAIL // TAIL_W
    return pl.pallas_call(
        _swa_kernel,
        grid=(batch, nsq),
        in_specs=[
            pl.BlockSpec(memory_space=pltpu.SMEM),
            pl.BlockSpec((TQ_SWA, SWA_Q_W), lambda b, i: (b * nsq + i, 0)),
            pl.BlockSpec((TQ_SWA, TAIL_W), lambda b, i: (b * nsq + i, tail_blk)),
            pl.BlockSpec((WINDOW, TAIL_W),
                         lambda b, i: (jnp.maximum((b * nsq + i) * per - 1, 0), tail_blk)),
        ],
        out_specs=pl.BlockSpec((TQ_SWA, SWA_Q_W), lambda b, i: (b * nsq + i, 0)),
        scratch_shapes=[pltpu.VMEM((WINDOW + TQ_SWA, 3 * LANES), BF16)],
        out_shape=jax.ShapeDtypeStruct((t, SWA_Q_W), BF16),
        compiler_params=pltpu.CompilerParams(
            dimension_semantics=("arbitrary", "arbitrary")),
        name="swa",
    )(sinks, proj, proj, proj)


def _fox_kernel(base_ref, q_ref, k_ref, aug_ref, vt_ref, o_ref,
                m_sc, acc_sc, qt_sc, s_sc, p_sc):
    b = pl.program_id(0)
    p = pl.program_id(1)
    qi = pl.program_id(2)
    nb = k_ref.shape[0] // TK_FOX
    q = q_ref[...]
    lane = lax.broadcasted_iota(jnp.int32, (TQ_FOX, LANES), 1)
    low = lane < HEAD_DIM
    zero = jnp.zeros_like(q)
    for hh in range(2):
        h = 2 * p + hh
        sel = (lane == h) | (lane == h + FOX_HEADS) | (lane == h + 2 * FOX_HEADS)
        ones = jnp.where(sel, 1.0, 0.0)
        q_h = jnp.where(low if hh == 0 else ~low, q, zero).astype(F32)
        q_full = jnp.concatenate([q_h, ones], axis=1)
        qt_sc[hh] = q_full.T.astype(BF16)

    m_sc[...] = jnp.full(m_sc.shape, NEG, F32)
    acc_sc[...] = jnp.zeros(acc_sc.shape, F32)
    ones_rows = jnp.ones((FOX_DEN_ROWS, TK_FOX), BF16)

    def scores(chain, slot):
        kstart, _, hh, q_lo, masked = chain
        w = TQ_FOX - q_lo
        ks = pl.ds(pl.multiple_of(kstart, TK_FOX), TK_FOX)
        k_full = jnp.concatenate([k_ref[ks, :], aug_ref[ks, :]], axis=1)
        s = jnp.dot(k_full, qt_sc[hh, :, q_lo:], preferred_element_type=F32)
        if masked:
            row = lax.broadcasted_iota(jnp.int32, s.shape, 0)
            col = lax.broadcasted_iota(jnp.int32, s.shape, 1)
            s = jnp.where(row <= col, s, NEG)
        s_sc[slot, :, 0:w] = s
        return jnp.max(s, axis=0, keepdims=True)

    def finish(blk_max, chain, slot, p_slot):
        kstart, kblk, hh, q_lo, _ = chain
        w = TQ_FOX - q_lo
        ks = pl.ds(pl.multiple_of(kstart, TK_FOX), TK_FOX)
        base = base_ref[(b * nb + kblk) * FOX_HEADS + 2 * p + hh]
        m_prev = m_sc[hh, :, q_lo:] - base
        m_new = jnp.maximum(m_prev, blk_max)
        alpha = jnp.exp2(m_prev - m_new)
        for r in range(0, TK_FOX, FOX_ROWS):
            pr = jnp.exp2(s_sc[slot, r:r + FOX_ROWS, 0:w] - m_new)
            p_sc[p_slot, r:r + FOX_ROWS, 0:w] = pr.astype(BF16)
        vt_h = jnp.concatenate([vt_ref[hh * HEAD_DIM:(hh + 1) * HEAD_DIM, ks], ones_rows], axis=0)
        acc_sc[hh, :, q_lo:] = alpha * acc_sc[hh, :, q_lo:] + jnp.dot(
            vt_h, p_sc[p_slot, :, 0:w], preferred_element_type=F32)
        m_sc[hh, :, q_lo:] = m_new + base

    def run(chains):
        nslot = FOX_AHEAD + 1
        pending = [scores(c, n) for n, c in enumerate(chains[:FOX_AHEAD])]
        for n, c in enumerate(chains):
            if n + FOX_AHEAD < len(chains):
                pending.append(scores(chains[n + FOX_AHEAD], (n + FOX_AHEAD) % nslot))
            finish(pending[n], c, n % nslot, n % 2)

    per = TQ_FOX // TK_FOX

    def body(j, carry):
        run([(j * TQ_FOX + c * TK_FOX, j * per + c, hh, 0, False)
             for c in range(per) for hh in range(2)])
        return carry

    lax.fori_loop(0, qi, body, 0)
    run([(qi * TQ_FOX + c * TK_FOX, qi * per + c, hh, c * TK_FOX, True)
         for c in range(per) for hh in range(2)])

    o_t = jnp.concatenate(
        [acc_sc[hh, 0:HEAD_DIM, :] * (1.0 / acc_sc[hh, HEAD_DIM:HEAD_DIM + 1, :]) for hh in range(2)],
        axis=0)
    o_ref[...] = o_t.T.astype(BF16)


def _fox(proj, aug, base, v_t, batch, seq):
    t = proj.shape[0]
    nq = seq // TQ_FOX
    pairs = FOX_HEADS // 2
    return pl.pallas_call(
        _fox_kernel,
        grid=(batch, pairs, nq),
        in_specs=[
            pl.BlockSpec(memory_space=pltpu.SMEM),
            pl.BlockSpec((TQ_FOX, LANES), lambda b, p, i: (b * nq + i, COL_FQ // LANES + p)),
            pl.BlockSpec((seq, LANES), lambda b, p, i: (b, COL_FK // LANES + p)),
            pl.BlockSpec((seq, LANES), lambda b, p, i: (b, 0)),
            pl.BlockSpec((None, LANES, seq), lambda b, p, i: (b, p, 0)),
        ],
        out_specs=pl.BlockSpec((TQ_FOX, LANES), lambda b, p, i: (b * nq + i, p)),
        out_shape=jax.ShapeDtypeStruct((t, FOX_W), BF16),
        scratch_shapes=[
            pltpu.VMEM((2, 1, TQ_FOX), F32),
            pltpu.VMEM((2, HEAD_DIM + FOX_DEN_ROWS, TQ_FOX), F32),
            pltpu.VMEM((2, 2 * LANES, TQ_FOX), BF16),
            pltpu.VMEM((FOX_AHEAD + 1, TK_FOX, TQ_FOX), F32),
            pltpu.VMEM((2, TK_FOX, TQ_FOX), BF16),
        ],
        compiler_params=pltpu.CompilerParams(
            dimension_semantics=("arbitrary", "arbitrary", "arbitrary")),
        name="fox",
    )(base, proj, proj, aug, v_t)


def _merge_kernel(oa_ref, ob_ref, ga_ref, gb_ref, x_ref, wa_ref, wb_ref, wo_ref, gain_ref,
                  x1_ref, h2_ref):
    ya = jnp.dot(oa_ref[...], wa_ref[...], preferred_element_type=F32)
    yb = jnp.dot(ob_ref[...], wb_ref[...], preferred_element_type=F32)
    def gate(ref):
        return 1.0 / (1.0 + jnp.exp(-ref[...].astype(F32)))

    merged = gate(ga_ref) * ya + gate(gb_ref) * yb
    x1 = x_ref[...] + jnp.dot(merged.astype(BF16), wo_ref[...], preferred_element_type=F32)
    x1_ref[...] = x1
    h2_ref[...] = _rmsnorm_rows(x1, gain_ref[...]).astype(BF16)


def _merge(o_a, o_b, proj, x2, wa, wb, wo, gain):
    t = x2.shape[0]
    tm = TM_MERGE
    const = dict(pipeline_mode=pl.Buffered(1))
    return pl.pallas_call(
        _merge_kernel,
        grid=(t // tm,),
        in_specs=[
            pl.BlockSpec((tm, SWA_Q_W), lambda i: (i, 0)),
            pl.BlockSpec((tm, FOX_W), lambda i: (i, 0)),
            pl.BlockSpec((tm, D_MODEL), lambda i: (i, COL_GA // D_MODEL)),
            pl.BlockSpec((tm, D_MODEL), lambda i: (i, COL_GB // D_MODEL)),
            pl.BlockSpec((tm, D_MODEL), lambda i: (i, 0)),
            pl.BlockSpec((SWA_Q_W, D_MODEL), lambda i: (0, 0), **const),
            pl.BlockSpec((FOX_W, D_MODEL), lambda i: (0, 0), **const),
            pl.BlockSpec((D_MODEL, D_MODEL), lambda i: (0, 0), **const),
            pl.BlockSpec((1, D_MODEL), lambda i: (0, 0), **const),
        ],
        out_specs=[
            pl.BlockSpec((tm, D_MODEL), lambda i: (i, 0)),
            pl.BlockSpec((tm, D_MODEL), lambda i: (i, 0)),
        ],
        out_shape=[
            jax.ShapeDtypeStruct((t, D_MODEL), F32),
            jax.ShapeDtypeStruct((t, D_MODEL), BF16),
        ],
        compiler_params=pltpu.CompilerParams(
            dimension_semantics=("arbitrary",),
            vmem_limit_bytes=56 * MIB),
        name="merge",
    )(o_a, o_b, proj, proj, x2, wa, wb, wo, gain)


def _mlp_kernel(h2_ref, x1_ref, wup_ref, wdn_ref, gain_ref, o_ref, *, final_norm):
    f = pl.program_id(1)

    @pl.when(f == 0)
    def _():
        o_ref[...] = x1_ref[...]

    u = jnp.maximum(jnp.dot(h2_ref[...], wup_ref[...], preferred_element_type=F32), 0.0)
    o_ref[...] += jnp.dot((u * u).astype(BF16), wdn_ref[...], preferred_element_type=F32)

    if final_norm:
        @pl.when(f == pl.num_programs(1) - 1)
        def _():
            o_ref[...] = _rmsnorm_rows(o_ref[...], gain_ref[...])


def _mlp(h2, x1, wup, wdn, gain, final_norm):
    t = x1.shape[0]
    tm, tf = TM_MLP, TF_MLP
    return pl.pallas_call(
        functools.partial(_mlp_kernel, final_norm=final_norm),
        grid=(t // tm, D_FF // tf),
        in_specs=[
            pl.BlockSpec((tm, D_MODEL), lambda i, f: (i, 0)),
            pl.BlockSpec((tm, D_MODEL), lambda i, f: (i, 0)),
            pl.BlockSpec((D_MODEL, tf), lambda i, f: (0, f)),
            pl.BlockSpec((tf, D_MODEL), lambda i, f: (f, 0)),
            pl.BlockSpec((1, D_MODEL), lambda i, f: (0, 0)),
        ],
        out_specs=pl.BlockSpec((tm, D_MODEL), lambda i, f: (i, 0)),
        out_shape=jax.ShapeDtypeStruct((t, D_MODEL), F32),
        compiler_params=pltpu.CompilerParams(
            dimension_semantics=("arbitrary", "arbitrary"),
            vmem_limit_bytes=48 * MIB),
        name="mlp",
    )(h2, x1, wup, wdn, gain)


def _pack_w_in(w, scale):
    o = 0
    a_q = w[:, o:o + SWA_Q_W] * scale; o += SWA_Q_W
    a_k = w[:, o:o + SWA_KV_HEADS * HEAD_DIM]; o += SWA_KV_HEADS * HEAD_DIM
    a_v = w[:, o:o + SWA_KV_HEADS * HEAD_DIM]; o += SWA_KV_HEADS * HEAD_DIM
    f_q = w[:, o:o + FOX_W] * (scale * LOG2E); o += FOX_W
    f_k = w[:, o:o + FOX_W]; o += FOX_W
    f_v = w[:, o:o + FOX_W]; o += FOX_W
    f_l = w[:, o:o + FOX_HEADS]; o += FOX_HEADS
    g_a = w[:, o:o + D_MODEL]; o += D_MODEL
    g_b = w[:, o:o + D_MODEL]; o += D_MODEL
    k0, k1 = a_k[:, :HEAD_DIM], a_k[:, HEAD_DIM:]
    pad = jnp.zeros((w.shape[0], LANES - FOX_HEADS), w.dtype)
    w_all = jnp.concatenate([a_q, f_q, g_a, g_b, f_k, k0, k0, k1, k1, a_v, f_l, pad],
                            axis=1).astype(BF16)
    return w_all, f_v.T.astype(BF16)


def kernel(x, positions, attn_norm, w_in, fox_f_bias, swa_sinks, w_branch_swa, w_branch_fox,
           w_out, mlp_norm, w_up, w_down, final_norm):
    batch, seq, d = x.shape
    depth = w_in.shape[0]
    assert d == D_MODEL and seq % TM_IN == 0 and seq % TQ_FOX == 0
    t = batch * seq
    scale = HEAD_DIM ** -0.5
    inv_freq = ROPE_THETA ** (-jnp.arange(0, HEAD_DIM, 2, dtype=F32) / HEAD_DIM)
    invf = jnp.tile(inv_freq, LANES // HALF)[None, :]
    pos2 = positions.reshape(t, 1)
    x2 = x.reshape(t, d)
    for l in range(depth):
        w_all, w_vt = _pack_w_in(w_in[l], scale)
        fbias = jnp.pad(fox_f_bias[l].astype(F32), (0, LANES - FOX_HEADS))[None, :]
        proj, logf, v_t = _inproj(x2, pos2, invf, attn_norm[l][None, :].astype(F32), w_all, w_vt,
                                  fbias, batch, seq)
        aug, base = _decay(logf, batch, seq)
        base = base[:, :, 0, :FOX_HEADS].reshape(-1)
        o_a = _swa(proj, swa_sinks[l].astype(F32), batch, seq)
        o_b = _fox(proj, aug, base, v_t, batch, seq)
        x1, h2 = _merge(o_a, o_b, proj, x2,
                        w_branch_swa[l].astype(BF16), w_branch_fox[l].astype(BF16),
                        w_out[l].astype(BF16), mlp_norm[l][None, :].astype(F32))
        x2 = _mlp(h2, x1, w_up[l].astype(BF16), w_down[l].astype(BF16),
                  final_norm[None, :].astype(F32), final_norm=(l == depth - 1))
    return x2.reshape(batch, seq, d)
```

```python
import functools

import jax
import jax.numpy as jnp
from jax import lax
from jax.experimental import pallas as pl
from jax.experimental.pallas import tpu as pltpu

F32 = jnp.float32
BF16 = jnp.bfloat16

D_MODEL = 2048
HEAD_DIM = 64
HALF = HEAD_DIM // 2
SWA_Q_HEADS = 16
SWA_KV_HEADS = 2
SWA_GROUP = SWA_Q_HEADS // SWA_KV_HEADS
WINDOW = 128
FOX_HEADS = 16
D_FF = 4 * D_MODEL
ROPE_THETA = 10000.0
RMS_EPS = 1e-6
SWA_Q_W = SWA_Q_HEADS * HEAD_DIM
FOX_W = FOX_HEADS * HEAD_DIM

LANES = 128
NEG = -1e30
MIB = 1024 * 1024
LOG2E = 1.4426950408889634

COL_AQ = 0
COL_FQ = 1024
COL_GA = 2048
COL_GB = 4096
COL_FK = 6144
COL_TAIL = 7168
TAIL_W = 512
D_PROJ = COL_TAIL + TAIL_W

TM_IN = 1024
TN_IN = 512
NJ_PROJ = D_PROJ // TN_IN
NJ_IN = NJ_PROJ + FOX_W // TN_IN
RN_ROWS = 128

TQ_SWA = 512
TQ_FOX = 1024
TK_FOX = 256
FOX_AHEAD = 2
FOX_CHAINS = 2 * (TQ_FOX // TK_FOX)
FOX_SLOTS = 4
assert FOX_CHAINS % FOX_SLOTS == 0 and FOX_AHEAD < FOX_SLOTS
FOX_ROWS = 16
FOX_DEN_ROWS = 16
TM_MERGE = 512
TM_MLP = 512
TF_MLP = 1024


def _rmsnorm_rows(x, gain):
    ms = jnp.mean(x * x, axis=-1, keepdims=True)
    return x * lax.rsqrt(ms + RMS_EPS) * gain


def _split3(x):
    hi = x.astype(BF16)
    r1 = x - hi.astype(F32)
    mid = r1.astype(BF16)
    lo = (r1 - mid.astype(F32)).astype(BF16)
    return hi, mid, lo


def _inproj_kernel(pos_ref, invf_ref, x_ref, gain_ref, w_ref, wvt_ref, fbias_ref,
                   proj_ref, logf_ref, vt_ref, h_sc, cos_sc, sa_sc, sb_sc):
    j = pl.program_id(1)

    @pl.when(j == 0)
    def _():
        def body(r, _):
            rows = pl.ds(pl.multiple_of(r * RN_ROWS, RN_ROWS), RN_ROWS)
            h_sc[rows, :] = _rmsnorm_rows(x_ref[rows, :], gain_ref[...]).astype(BF16)
            ang = pos_ref[rows, :].astype(F32) * invf_ref[...]
            c = jnp.cos(ang)
            s = jnp.sin(ang)
            lane = lax.broadcasted_iota(jnp.int32, ang.shape, 1)
            first = (lane % HEAD_DIM) < HALF
            cos_sc[rows, :] = c
            sa_sc[rows, :] = jnp.where(first, -s, 0.0)
            sb_sc[rows, :] = jnp.where(first, 0.0, s)
            return 0
        lax.fori_loop(0, TM_IN // RN_ROWS, body, 0)

    def project():
        return jnp.dot(h_sc[...], w_ref[...], preferred_element_type=F32)

    def rope_store(acc, c0, c1):
        for c in range(c0, c1, LANES):
            blk = acc[:, c:c + LANES]
            out = (blk * cos_sc[...]
                   + pltpu.roll(blk, LANES - HALF, 1) * sa_sc[...]
                   + pltpu.roll(blk, HALF, 1) * sb_sc[...])
            proj_ref[:, c:c + LANES] = out.astype(BF16)

    @pl.when(j < COL_FQ // TN_IN)
    def _():
        rope_store(project(), 0, TN_IN)

    @pl.when((j >= COL_FQ // TN_IN) & (j < COL_TAIL // TN_IN))
    def _():
        proj_ref[...] = project().astype(BF16)

    @pl.when(j == COL_TAIL // TN_IN)
    def _():
        acc = project()
        rope_store(acc, 0, 2 * LANES)
        proj_ref[:, 2 * LANES:3 * LANES] = acc[:, 2 * LANES:3 * LANES].astype(BF16)
        proj_ref[:, 3 * LANES:4 * LANES] = jnp.zeros((TM_IN, LANES), BF16)
        z = acc[:, 3 * LANES:4 * LANES] + fbias_ref[...]
        logf_ref[...] = jnp.minimum(z, 0.0) - jnp.log1p(jnp.exp(-jnp.abs(z)))

    @pl.when(j >= NJ_PROJ)
    def _():
        vt = lax.dot_general(wvt_ref[...], h_sc[...], (((1,), (1,)), ((), ())),
                             preferred_element_type=F32)
        vt_ref[...] = vt.astype(BF16)


def _inproj(x2, pos2, invf, gain, w_all, w_vt, fbias, batch, seq):
    t = x2.shape[0]
    nsb = seq // TM_IN
    last = NJ_PROJ - 1

    def vt_blk(j):
        return jnp.clip(j - NJ_PROJ, 0, FOX_W // TN_IN - 1)

    return pl.pallas_call(
        _inproj_kernel,
        grid=(t // TM_IN, NJ_IN),
        in_specs=[
            pl.BlockSpec((TM_IN, 1), lambda i, j: (i, 0)),
            pl.BlockSpec((1, LANES), lambda i, j: (0, 0)),
            pl.BlockSpec((TM_IN, D_MODEL), lambda i, j: (i, 0)),
            pl.BlockSpec((1, D_MODEL), lambda i, j: (0, 0)),
            pl.BlockSpec((D_MODEL, TN_IN), lambda i, j: (0, jnp.minimum(j, last))),
            pl.BlockSpec((TN_IN, D_MODEL), lambda i, j: (vt_blk(j), 0)),
            pl.BlockSpec((1, LANES), lambda i, j: (0, 0)),
        ],
        out_specs=[
            pl.BlockSpec((TM_IN, TN_IN), lambda i, j: (i, jnp.minimum(j, last))),
            pl.BlockSpec((TM_IN, LANES), lambda i, j: (i, 0)),
            pl.BlockSpec((None, TN_IN, TM_IN), lambda i, j: (i // nsb, vt_blk(j), i % nsb)),
        ],
        out_shape=[
            jax.ShapeDtypeStruct((t, D_PROJ), BF16),
            jax.ShapeDtypeStruct((t, LANES), F32),
            jax.ShapeDtypeStruct((batch, FOX_W, seq), BF16),
        ],
        scratch_shapes=[
            pltpu.VMEM((TM_IN, D_MODEL), BF16),
            pltpu.VMEM((TM_IN, LANES), F32),
            pltpu.VMEM((TM_IN, LANES), F32),
            pltpu.VMEM((TM_IN, LANES), F32),
        ],
        compiler_params=pltpu.CompilerParams(
            dimension_semantics=("arbitrary", "arbitrary"),
            vmem_limit_bytes=52 * MIB),
        name="inproj",
    )(pos2, invf, x2, gain, w_all, w_vt, fbias)


def _decay_kernel(logf_ref, aug_ref, base_ref, carry_sc):
    sb = pl.program_id(1)

    @pl.when(sb == 0)
    def _():
        carry_sc[...] = jnp.zeros(carry_sc.shape, F32)

    x2 = logf_ref[...] * LOG2E
    row = lax.broadcasted_iota(jnp.int32, (TK_FOX, TK_FOX), 0)
    col = lax.broadcasted_iota(jnp.int32, (TK_FOX, TK_FOX), 1)
    tri = jnp.where(col <= row, 1.0, 0.0).astype(BF16)
    hi, mid, lo = _split3(x2)
    cl = (jnp.dot(tri, hi, preferred_element_type=F32)
          + jnp.dot(tri, mid, preferred_element_type=F32)
          + jnp.dot(tri, lo, preferred_element_type=F32))
    yh, ym, yl = _split3(-cl)
    lane = lax.broadcasted_iota(jnp.int32, (TK_FOX, LANES), 1)
    zero = jnp.zeros((TK_FOX, LANES), F32)
    pieces = jnp.where(lane < FOX_HEADS, yh.astype(F32),
                       jnp.where(lane < 2 * FOX_HEADS, pltpu.roll(ym.astype(F32), FOX_HEADS, 1),
                                 jnp.where(lane < 3 * FOX_HEADS,
                                           pltpu.roll(yl.astype(F32), 2 * FOX_HEADS, 1), zero)))
    aug_ref[...] = pieces.astype(BF16)
    base_ref[...] = -carry_sc[...]
    carry_sc[...] = carry_sc[...] + cl[TK_FOX - 1:TK_FOX, :]


def _decay(logf, batch, seq):
    t = logf.shape[0]
    nb = seq // TK_FOX
    return pl.pallas_call(
        _decay_kernel,
        grid=(batch, nb),
        in_specs=[pl.BlockSpec((TK_FOX, LANES), lambda b, s: (b * nb + s, 0))],
        out_specs=[
            pl.BlockSpec((TK_FOX, LANES), lambda b, s: (b * nb + s, 0)),
            pl.BlockSpec((None, None, 1, LANES), lambda b, s: (b, s, 0, 0)),
        ],
        out_shape=[
            jax.ShapeDtypeStruct((t, LANES), BF16),
            jax.ShapeDtypeStruct((batch, nb, 1, LANES), F32),
        ],
        scratch_shapes=[pltpu.VMEM((1, LANES), F32)],
        compiler_params=pltpu.CompilerParams(
            dimension_semantics=("arbitrary", "arbitrary")),
        name="decay",
    )(logf)


def _swa_kernel(sinks_ref, q_ref, cur_ref, prev_ref, o_ref, kv_sc):
    i = pl.program_id(1)
    kvw = 3 * LANES
    kv_sc[0:WINDOW, :] = prev_ref[:, 0:kvw]
    kv_sc[WINDOW:WINDOW + TQ_SWA, :] = cur_ref[:, 0:kvw]

    row = lax.broadcasted_iota(jnp.int32, (WINDOW, 2 * WINDOW), 0)
    col = lax.broadcasted_iota(jnp.int32, (WINDOW, 2 * WINDOW), 1)
    allowed = (col > row) & (col <= row + WINDOW)
    bias_any = jnp.where(allowed, 0.0, NEG)
    bias_first = jnp.where(allowed & (col >= WINDOW), 0.0, NEG)
    bias_r0 = jnp.where(i == 0, bias_first, bias_any)
    low = lax.broadcasted_iota(jnp.int32, (WINDOW, LANES), 1) < HEAD_DIM

    for r in range(TQ_SWA // WINDOW):
        bias = bias_r0 if r == 0 else bias_any
        rows = slice(r * WINDOW, (r + 1) * WINDOW)
        win = slice(r * WINDOW, (r + 2) * WINDOW)
        v_pair = kv_sc[win, 2 * LANES:3 * LANES]
        for p in range(SWA_Q_HEADS // 2):
            q_pair = q_ref[rows, p * LANES:(p + 1) * LANES]
            g = (2 * p) // SWA_GROUP
            k_dup = kv_sc[win, g * LANES:(g + 1) * LANES]
            halves = []
            for hh in range(2):
                h = 2 * p + hh
                q_h = jnp.where(low if hh == 0 else ~low, q_pair, jnp.zeros_like(q_pair))
                s = lax.dot_general(q_h, k_dup, (((1,), (1,)), ((), ())),
                                    preferred_element_type=F32) + bias
                sink = sinks_ref[h]
                m = jnp.maximum(jnp.max(s, axis=1, keepdims=True), sink)
                pr = jnp.exp(s - m)
                den = jnp.sum(pr, axis=1, keepdims=True) + jnp.exp(sink - m)
                o = jnp.dot(pr.astype(BF16), v_pair, preferred_element_type=F32)
                o = o * (1.0 / den)
                if g != hh:
                    o = pltpu.roll(o, HEAD_DIM, 1)
                halves.append(o)
            o_ref[rows, p * LANES:(p + 1) * LANES] = jnp.where(low, halves[0], halves[1]).astype(BF16)


def _swa(proj, sinks, batch, seq):
    t = proj.shape[0]
    nsq = seq // TQ_SWA
    per = TQ_SWA // WINDOW
    tail_blk = COL_TAIL // TAIL_W
    return pl.pallas_call(
        _swa_kernel,
        grid=(batch, nsq),
        in_specs=[
            pl.BlockSpec(memory_space=pltpu.SMEM),
            pl.BlockSpec((TQ_SWA, SWA_Q_W), lambda b, i: (b * nsq + i, 0)),
            pl.BlockSpec((TQ_SWA, TAIL_W), lambda b, i: (b * nsq + i, tail_blk)),
            pl.BlockSpec((WINDOW, TAIL_W),
                         lambda b, i: (jnp.maximum((b * nsq + i) * per - 1, 0), tail_blk)),
        ],
        out_specs=pl.BlockSpec((TQ_SWA, SWA_Q_W), lambda b, i: (b * nsq + i, 0)),
        scratch_shapes=[pltpu.VMEM((WINDOW + TQ_SWA, 3 * LANES), BF16)],
        out_shape=jax.ShapeDtypeStruct((t, SWA_Q_W), BF16),
        compiler_params=pltpu.CompilerParams(
            dimension_semantics=("arbitrary", "arbitrary")),
        name="swa",
    )(sinks, proj, proj, proj)


def _fox_kernel(base_ref, q_ref, k_ref, aug_ref, vt_ref, o_ref,
                m_sc, acc_sc, qt_sc, bm_sc, *bufs):
    b = pl.program_id(0)
    p = pl.program_id(1)
    qi = pl.program_id(2)
    s_bufs, p_bufs = bufs[:FOX_SLOTS], bufs[FOX_SLOTS:]
    dz = pl.multiple_of(jnp.minimum(qi, 0), FOX_ROWS)
    nb = k_ref.shape[0] // TK_FOX
    per = TQ_FOX // TK_FOX
    q = q_ref[...]
    lane = lax.broadcasted_iota(jnp.int32, (TQ_FOX, LANES), 1)
    low = lane < HEAD_DIM
    zero = jnp.zeros_like(q)
    for hh in range(2):
        h = 2 * p + hh
        sel = (lane == h) | (lane == h + FOX_HEADS) | (lane == h + 2 * FOX_HEADS)
        ones = jnp.where(sel, 1.0, 0.0)
        q_h = jnp.where(low if hh == 0 else ~low, q, zero).astype(F32)
        q_full = jnp.concatenate([q_h, ones], axis=1)
        qt_sc[hh] = q_full.T.astype(BF16)

    m_sc[...] = jnp.full(m_sc.shape, NEG, F32)
    acc_sc[...] = jnp.zeros(acc_sc.shape, F32)
    ones_rows = jnp.ones((FOX_DEN_ROWS, TK_FOX), BF16)

    def scores(blk, n, q_lo, diag):
        hh, slot, w = n % 2, n % FOX_SLOTS, TQ_FOX - q_lo
        ks = pl.ds(pl.multiple_of(blk * TQ_FOX + (n // 2) * TK_FOX, TK_FOX), TK_FOX)
        k_full = jnp.concatenate([k_ref[ks, :], aug_ref[ks, :]], axis=1)
        s = jnp.dot(k_full, qt_sc[hh, :, q_lo:], preferred_element_type=F32)
        if diag is not None:
            row = lax.broadcasted_iota(jnp.int32, s.shape, 0)
            col = lax.broadcasted_iota(jnp.int32, s.shape, 1)
            limit = 0 if diag is True else jnp.where(diag, 0, TK_FOX)
            s = jnp.where(row - col <= limit, s, NEG)
        s_bufs[slot][pl.ds(dz, TK_FOX), 0:w] = s
        bm_sc[slot, :, 0:w] = jnp.max(s, axis=0, keepdims=True)

    def finish(blk, n, q_lo):
        hh, slot, p_slot, w = n % 2, n % FOX_SLOTS, n % 2, TQ_FOX - q_lo
        ks = pl.ds(pl.multiple_of(blk * TQ_FOX + (n // 2) * TK_FOX, TK_FOX), TK_FOX)
        base = base_ref[(b * nb + blk * per + n // 2) * FOX_HEADS + 2 * p + hh]
        m_prev = m_sc[hh, :, q_lo:] - base
        m_new = jnp.maximum(m_prev, bm_sc[slot, :, 0:w])
        alpha = jnp.exp2(m_prev - m_new)
        for r in range(0, TK_FOX, FOX_ROWS):
            rows = pl.ds(dz + r, FOX_ROWS)
            pr = jnp.exp2(s_bufs[slot][rows, 0:w] - m_new)
            p_bufs[p_slot][rows, 0:w] = pr.astype(BF16)
        vt_h = jnp.concatenate([vt_ref[hh * HEAD_DIM:(hh + 1) * HEAD_DIM, ks], ones_rows], axis=0)
        acc_sc[hh, :, q_lo:] = alpha * acc_sc[hh, :, q_lo:] + jnp.dot(
            vt_h, p_bufs[p_slot][pl.ds(dz, TK_FOX), 0:w], preferred_element_type=F32)
        m_sc[hh, :, q_lo:] = m_new + base

    for n in range(FOX_AHEAD):
        scores(0, n, 0, qi == 0)

    def body(j, carry):
        for n in range(FOX_CHAINS):
            m = n + FOX_AHEAD
            if m < FOX_CHAINS:
                scores(j, m, 0, None)
            else:
                scores(j + 1, m - FOX_CHAINS, 0, j + 1 == qi)
            finish(j, n, 0)
        return carry

    lax.fori_loop(0, qi, body, 0)
    for n in range(FOX_CHAINS):
        m = n + FOX_AHEAD
        if m < FOX_CHAINS:
            scores(qi, m, (m // 2) * TK_FOX, True)
        finish(qi, n, (n // 2) * TK_FOX)

    o_t = jnp.concatenate(
        [acc_sc[hh, 0:HEAD_DIM, :] * (1.0 / acc_sc[hh, HEAD_DIM:HEAD_DIM + 1, :]) for hh in range(2)],
        axis=0)
    o_ref[...] = o_t.T.astype(BF16)


def _fox(proj, aug, base, v_t, batch, seq):
    t = proj.shape[0]
    nq = seq // TQ_FOX
    pairs = FOX_HEADS // 2
    return pl.pallas_call(
        _fox_kernel,
        grid=(batch, pairs, nq),
        in_specs=[
            pl.BlockSpec(memory_space=pltpu.SMEM),
            pl.BlockSpec((TQ_FOX, LANES), lambda b, p, i: (b * nq + i, COL_FQ // LANES + p)),
            pl.BlockSpec((seq, LANES), lambda b, p, i: (b, COL_FK // LANES + p)),
            pl.BlockSpec((seq, LANES), lambda b, p, i: (b, 0)),
            pl.BlockSpec((None, LANES, seq), lambda b, p, i: (b, p, 0)),
        ],
        out_specs=pl.BlockSpec((TQ_FOX, LANES), lambda b, p, i: (b * nq + i, p)),
        out_shape=jax.ShapeDtypeStruct((t, FOX_W), BF16),
        scratch_shapes=[
            pltpu.VMEM((2, 1, TQ_FOX), F32),
            pltpu.VMEM((2, HEAD_DIM + FOX_DEN_ROWS, TQ_FOX), F32),
            pltpu.VMEM((2, 2 * LANES, TQ_FOX), BF16),
            pltpu.VMEM((FOX_SLOTS, 1, TQ_FOX), F32),
        ] + [pltpu.VMEM((TK_FOX, TQ_FOX), F32)] * FOX_SLOTS + [
            pltpu.VMEM((TK_FOX, TQ_FOX), BF16),
            pltpu.VMEM((TK_FOX, TQ_FOX), BF16),
        ],
        compiler_params=pltpu.CompilerParams(
            dimension_semantics=("arbitrary", "arbitrary", "arbitrary")),
        name="fox",
    )(base, proj, proj, aug, v_t)


def _merge_kernel(oa_ref, ob_ref, ga_ref, gb_ref, x_ref, wa_ref, wb_ref, wo_ref, gain_ref,
                  x1_ref, h2_ref):
    ya = jnp.dot(oa_ref[...], wa_ref[...], preferred_element_type=F32)
    yb = jnp.dot(ob_ref[...], wb_ref[...], preferred_element_type=F32)
    def gate(ref):
        return 1.0 / (1.0 + jnp.exp(-ref[...].astype(F32)))

    merged = gate(ga_ref) * ya + gate(gb_ref) * yb
    x1 = x_ref[...] + jnp.dot(merged.astype(BF16), wo_ref[...], preferred_element_type=F32)
    x1_ref[...] = x1
    h2_ref[...] = _rmsnorm_rows(x1, gain_ref[...]).astype(BF16)


def _merge(o_a, o_b, proj, x2, wa, wb, wo, gain):
    t = x2.shape[0]
    tm = TM_MERGE
    const = dict(pipeline_mode=pl.Buffered(1))
    return pl.pallas_call(
        _merge_kernel,
        grid=(t // tm,),
        in_specs=[
            pl.BlockSpec((tm, SWA_Q_W), lambda i: (i, 0)),
            pl.BlockSpec((tm, FOX_W), lambda i: (i, 0)),
            pl.BlockSpec((tm, D_MODEL), lambda i: (i, COL_GA // D_MODEL)),
            pl.BlockSpec((tm, D_MODEL), lambda i: (i, COL_GB // D_MODEL)),
            pl.BlockSpec((tm, D_MODEL), lambda i: (i, 0)),
            pl.BlockSpec((SWA_Q_W, D_MODEL), lambda i: (0, 0), **const),
            pl.BlockSpec((FOX_W, D_MODEL), lambda i: (0, 0), **const),
            pl.BlockSpec((D_MODEL, D_MODEL), lambda i: (0, 0), **const),
            pl.BlockSpec((1, D_MODEL), lambda i: (0, 0), **const),
        ],
        out_specs=[
            pl.BlockSpec((tm, D_MODEL), lambda i: (i, 0)),
            pl.BlockSpec((tm, D_MODEL), lambda i: (i, 0)),
        ],
        out_shape=[
            jax.ShapeDtypeStruct((t, D_MODEL), F32),
            jax.ShapeDtypeStruct((t, D_MODEL), BF16),
        ],
        compiler_params=pltpu.CompilerParams(
            dimension_semantics=("arbitrary",),
            vmem_limit_bytes=56 * MIB),
        name="merge",
    )(o_a, o_b, proj, proj, x2, wa, wb, wo, gain)


def _mlp_kernel(h2_ref, x1_ref, wup_ref, wdn_ref, gain_ref, o_ref, *, final_norm):
    f = pl.program_id(1)

    @pl.when(f == 0)
    def _():
        o_ref[...] = x1_ref[...]

    u = jnp.maximum(jnp.dot(h2_ref[...], wup_ref[...], preferred_element_type=F32), 0.0)
    o_ref[...] += jnp.dot((u * u).astype(BF16), wdn_ref[...], preferred_element_type=F32)

    if final_norm:
        @pl.when(f == pl.num_programs(1) - 1)
        def _():
            o_ref[...] = _rmsnorm_rows(o_ref[...], gain_ref[...])


def _mlp(h2, x1, wup, wdn, gain, final_norm):
    t = x1.shape[0]
    tm, tf = TM_MLP, TF_MLP
    return pl.pallas_call(
        functools.partial(_mlp_kernel, final_norm=final_norm),
        grid=(t // tm, D_FF // tf),
        in_specs=[
            pl.BlockSpec((tm, D_MODEL), lambda i, f: (i, 0)),
            pl.BlockSpec((tm, D_MODEL), lambda i, f: (i, 0)),
            pl.BlockSpec((D_MODEL, tf), lambda i, f: (0, f)),
            pl.BlockSpec((tf, D_MODEL), lambda i, f: (f, 0)),
            pl.BlockSpec((1, D_MODEL), lambda i, f: (0, 0)),
        ],
        out_specs=pl.BlockSpec((tm, D_MODEL), lambda i, f: (i, 0)),
        out_shape=jax.ShapeDtypeStruct((t, D_MODEL), F32),
        compiler_params=pltpu.CompilerParams(
            dimension_semantics=("arbitrary", "arbitrary"),
            vmem_limit_bytes=48 * MIB),
        name="mlp",
    )(h2, x1, wup, wdn, gain)


def _pack_w_in(w, scale):
    o = 0
    a_q = w[:, o:o + SWA_Q_W] * scale; o += SWA_Q_W
    a_k = w[:, o:o + SWA_KV_HEADS * HEAD_DIM]; o += SWA_KV_HEADS * HEAD_DIM
    a_v = w[:, o:o + SWA_KV_HEADS * HEAD_DIM]; o += SWA_KV_HEADS * HEAD_DIM
    f_q = w[:, o:o + FOX_W] * (scale * LOG2E); o += FOX_W
    f_k = w[:, o:o + FOX_W]; o += FOX_W
    f_v = w[:, o:o + FOX_W]; o += FOX_W
    f_l = w[:, o:o + FOX_HEADS]; o += FOX_HEADS
    g_a = w[:, o:o + D_MODEL]; o += D_MODEL
    g_b = w[:, o:o + D_MODEL]; o += D_MODEL
    k0, k1 = a_k[:, :HEAD_DIM], a_k[:, HEAD_DIM:]
    pad = jnp.zeros((w.shape[0], LANES - FOX_HEADS), w.dtype)
    w_all = jnp.concatenate([a_q, f_q, g_a, g_b, f_k, k0, k0, k1, k1, a_v, f_l, pad],
                            axis=1).astype(BF16)
    return w_all, f_v.T.astype(BF16)


def kernel(x, positions, attn_norm, w_in, fox_f_bias, swa_sinks, w_branch_swa, w_branch_fox,
           w_out, mlp_norm, w_up, w_down, final_norm):
    batch, seq, d = x.shape
    depth = w_in.shape[0]
    assert d == D_MODEL and seq % TM_IN == 0 and seq % TQ_FOX == 0
    t = batch * seq
    scale = HEAD_DIM ** -0.5
    inv_freq = ROPE_THETA ** (-jnp.arange(0, HEAD_DIM, 2, dtype=F32) / HEAD_DIM)
    invf = jnp.tile(inv_freq, LANES // HALF)[None, :]
    pos2 = positions.reshape(t, 1)
    x2 = x.reshape(t, d)
    for l in range(depth):
        w_all, w_vt = _pack_w_in(w_in[l], scale)
        fbias = jnp.pad(fox_f_bias[l].astype(F32), (0, LANES - FOX_HEADS))[None, :]
        proj, logf, v_t = _inproj(x2, pos2, invf, attn_norm[l][None, :].astype(F32), w_all, w_vt,
                                  fbias, batch, seq)
        aug, base = _decay(logf, batch, seq)
        base = base[:, :, 0, :FOX_HEADS].reshape(-1)
        o_a = _swa(proj, swa_sinks[l].astype(F32), batch, seq)
        o_b = _fox(proj, aug, base, v_t, batch, seq)
        x1, h2 = _merge(o_a, o_b, proj, x2,
                        w_branch_swa[l].astype(BF16), w_branch_fox[l].astype(BF16),
                        w_out[l].astype(BF16), mlp_norm[l][None, :].astype(F32))
        x2 = _mlp(h2, x1, w_up[l].astype(BF16), w_down[l].astype(BF16),
                  final_norm[None, :].astype(F32), final_norm=(l == depth - 1))
    return x2.reshape(batch, seq, d)
```

```python
import functools

import jax
import jax.numpy as jnp
from jax import lax
from jax.experimental import pallas as pl
from jax.experimental.pallas import tpu as pltpu

F32 = jnp.float32
BF16 = jnp.bfloat16

D_MODEL = 2048
HEAD_DIM = 64
HALF = HEAD_DIM // 2
SWA_Q_HEADS = 16
SWA_KV_HEADS = 2
SWA_GROUP = SWA_Q_HEADS // SWA_KV_HEADS
WINDOW = 128
FOX_HEADS = 16
D_FF = 4 * D_MODEL
ROPE_THETA = 10000.0
RMS_EPS = 1e-6
SWA_Q_W = SWA_Q_HEADS * HEAD_DIM
FOX_W = FOX_HEADS * HEAD_DIM

LANES = 128
NEG = -1e30
MIB = 1024 * 1024
LOG2E = 1.4426950408889634

COL_AQ = 0
COL_FQ = 1024
COL_GA = 2048
COL_GB = 4096
COL_FK = 6144
COL_TAIL = 7168
TAIL_W = 512
D_PROJ = COL_TAIL + TAIL_W

TM_IN = 1024
TN_IN = 512
NJ_PROJ = D_PROJ // TN_IN
NJ_IN = NJ_PROJ + FOX_W // TN_IN
RN_ROWS = 128

TQ_SWA = 512
TQ_FOX = 1024
TK_FOX = 256
FOX_AHEAD = 2
FOX_CHAINS = 2 * (TQ_FOX // TK_FOX)
FOX_SLOTS = 4
assert FOX_CHAINS % FOX_SLOTS == 0 and FOX_AHEAD < FOX_SLOTS
FOX_ROWS = 16
FOX_DEN_ROWS = 16
TM_MERGE = 512
TM_MLP = 512
TF_MLP = 1024


def _rmsnorm_rows(x, gain):
    ms = jnp.mean(x * x, axis=-1, keepdims=True)
    return x * lax.rsqrt(ms + RMS_EPS) * gain


def _split3(x):
    hi = x.astype(BF16)
    r1 = x - hi.astype(F32)
    mid = r1.astype(BF16)
    lo = (r1 - mid.astype(F32)).astype(BF16)
    return hi, mid, lo


def _inproj_kernel(pos_ref, invf_ref, x_ref, gain_ref, w_ref, wvt_ref, fbias_ref,
                   proj_ref, logf_ref, vt_ref, h_sc, cos_sc, sa_sc, sb_sc):
    j = pl.program_id(1)

    @pl.when(j == 0)
    def _():
        def body(r, _):
            rows = pl.ds(pl.multiple_of(r * RN_ROWS, RN_ROWS), RN_ROWS)
            h_sc[rows, :] = _rmsnorm_rows(x_ref[rows, :], gain_ref[...]).astype(BF16)
            ang = pos_ref[rows, :].astype(F32) * invf_ref[...]
            c = jnp.cos(ang)
            s = jnp.sin(ang)
            lane = lax.broadcasted_iota(jnp.int32, ang.shape, 1)
            first = (lane % HEAD_DIM) < HALF
            cos_sc[rows, :] = c
            sa_sc[rows, :] = jnp.where(first, -s, 0.0)
            sb_sc[rows, :] = jnp.where(first, 0.0, s)
            return 0
        lax.fori_loop(0, TM_IN // RN_ROWS, body, 0)

    def project():
        return jnp.dot(h_sc[...], w_ref[...], preferred_element_type=F32)

    def rope_store(acc, c0, c1):
        for c in range(c0, c1, LANES):
            blk = acc[:, c:c + LANES]
            out = (blk * cos_sc[...]
                   + pltpu.roll(blk, LANES - HALF, 1) * sa_sc[...]
                   + pltpu.roll(blk, HALF, 1) * sb_sc[...])
            proj_ref[:, c:c + LANES] = out.astype(BF16)

    @pl.when(j < COL_FQ // TN_IN)
    def _():
        rope_store(project(), 0, TN_IN)

    @pl.when((j >= COL_FQ // TN_IN) & (j < COL_TAIL // TN_IN))
    def _():
        proj_ref[...] = project().astype(BF16)

    @pl.when(j == COL_TAIL // TN_IN)
    def _():
        acc = project()
        rope_store(acc, 0, 2 * LANES)
        proj_ref[:, 2 * LANES:3 * LANES] = acc[:, 2 * LANES:3 * LANES].astype(BF16)
        proj_ref[:, 3 * LANES:4 * LANES] = jnp.zeros((TM_IN, LANES), BF16)
        z = acc[:, 3 * LANES:4 * LANES] + fbias_ref[...]
        logf_ref[...] = jnp.minimum(z, 0.0) - jnp.log1p(jnp.exp(-jnp.abs(z)))

    @pl.when(j >= NJ_PROJ)
    def _():
        vt = lax.dot_general(wvt_ref[...], h_sc[...], (((1,), (1,)), ((), ())),
                             preferred_element_type=F32)
        vt_ref[...] = vt.astype(BF16)


def _inproj(x2, pos2, invf, gain, w_all, w_vt, fbias, batch, seq):
    t = x2.shape[0]
    nsb = seq // TM_IN
    last = NJ_PROJ - 1

    def vt_blk(j):
        return jnp.clip(j - NJ_PROJ, 0, FOX_W // TN_IN - 1)

    return pl.pallas_call(
        _inproj_kernel,
        grid=(t // TM_IN, NJ_IN),
        in_specs=[
            pl.BlockSpec((TM_IN, 1), lambda i, j: (i, 0)),
            pl.BlockSpec((1, LANES), lambda i, j: (0, 0)),
            pl.BlockSpec((TM_IN, D_MODEL), lambda i, j: (i, 0)),
            pl.BlockSpec((1, D_MODEL), lambda i, j: (0, 0)),
            pl.BlockSpec((None, D_MODEL, TN_IN), lambda i, j: (jnp.minimum(j, last), 0, 0)),
            pl.BlockSpec((TN_IN, D_MODEL), lambda i, j: (vt_blk(j), 0)),
            pl.BlockSpec((1, LANES), lambda i, j: (0, 0)),
        ],
        out_specs=[
            pl.BlockSpec((TM_IN, TN_IN), lambda i, j: (i, jnp.minimum(j, last))),
            pl.BlockSpec((TM_IN, LANES), lambda i, j: (i, 0)),
            pl.BlockSpec((None, TN_IN, TM_IN), lambda i, j: (i // nsb, vt_blk(j), i % nsb)),
        ],
        out_shape=[
            jax.ShapeDtypeStruct((t, D_PROJ), BF16),
            jax.ShapeDtypeStruct((t, LANES), F32),
            jax.ShapeDtypeStruct((batch, FOX_W, seq), BF16),
        ],
        scratch_shapes=[
            pltpu.VMEM((TM_IN, D_MODEL), BF16),
            pltpu.VMEM((TM_IN, LANES), F32),
            pltpu.VMEM((TM_IN, LANES), F32),
            pltpu.VMEM((TM_IN, LANES), F32),
        ],
        compiler_params=pltpu.CompilerParams(
            dimension_semantics=("arbitrary", "arbitrary"),
            vmem_limit_bytes=52 * MIB),
        name="inproj",
    )(pos2, invf, x2, gain, w_all, w_vt, fbias)


def _decay_kernel(logf_ref, aug_ref, base_ref, carry_sc):
    sb = pl.program_id(1)

    @pl.when(sb == 0)
    def _():
        carry_sc[...] = jnp.zeros(carry_sc.shape, F32)

    x2 = logf_ref[...] * LOG2E
    row = lax.broadcasted_iota(jnp.int32, (TK_FOX, TK_FOX), 0)
    col = lax.broadcasted_iota(jnp.int32, (TK_FOX, TK_FOX), 1)
    tri = jnp.where(col <= row, 1.0, 0.0).astype(BF16)
    hi, mid, lo = _split3(x2)
    cl = (jnp.dot(tri, hi, preferred_element_type=F32)
          + jnp.dot(tri, mid, preferred_element_type=F32)
          + jnp.dot(tri, lo, preferred_element_type=F32))
    yh, ym, yl = _split3(-cl)
    lane = lax.broadcasted_iota(jnp.int32, (TK_FOX, LANES), 1)
    zero = jnp.zeros((TK_FOX, LANES), F32)
    pieces = jnp.where(lane < FOX_HEADS, yh.astype(F32),
                       jnp.where(lane < 2 * FOX_HEADS, pltpu.roll(ym.astype(F32), FOX_HEADS, 1),
                                 jnp.where(lane < 3 * FOX_HEADS,
                                           pltpu.roll(yl.astype(F32), 2 * FOX_HEADS, 1), zero)))
    aug_ref[...] = pieces.astype(BF16)
    base_ref[...] = -carry_sc[...]
    carry_sc[...] = carry_sc[...] + cl[TK_FOX - 1:TK_FOX, :]


def _decay(logf, batch, seq):
    t = logf.shape[0]
    nb = seq // TK_FOX
    return pl.pallas_call(
        _decay_kernel,
        grid=(batch, nb),
        in_specs=[pl.BlockSpec((TK_FOX, LANES), lambda b, s: (b * nb + s, 0))],
        out_specs=[
            pl.BlockSpec((TK_FOX, LANES), lambda b, s: (b * nb + s, 0)),
            pl.BlockSpec((None, None, 1, LANES), lambda b, s: (b, s, 0, 0)),
        ],
        out_shape=[
            jax.ShapeDtypeStruct((t, LANES), BF16),
            jax.ShapeDtypeStruct((batch, nb, 1, LANES), F32),
        ],
        scratch_shapes=[pltpu.VMEM((1, LANES), F32)],
        compiler_params=pltpu.CompilerParams(
            dimension_semantics=("arbitrary", "arbitrary")),
        name="decay",
    )(logf)


def _swa_kernel(sinks_ref, q_ref, cur_ref, prev_ref, o_ref, kv_sc):
    i = pl.program_id(1)
    kvw = 3 * LANES
    kv_sc[0:WINDOW, :] = prev_ref[:, 0:kvw]
    kv_sc[WINDOW:WINDOW + TQ_SWA, :] = cur_ref[:, 0:kvw]

    row = lax.broadcasted_iota(jnp.int32, (WINDOW, 2 * WINDOW), 0)
    col = lax.broadcasted_iota(jnp.int32, (WINDOW, 2 * WINDOW), 1)
    allowed = (col > row) & (col <= row + WINDOW)
    bias_any = jnp.where(allowed, 0.0, NEG)
    bias_first = jnp.where(allowed & (col >= WINDOW), 0.0, NEG)
    bias_r0 = jnp.where(i == 0, bias_first, bias_any)
    low = lax.broadcasted_iota(jnp.int32, (WINDOW, LANES), 1) < HEAD_DIM

    for r in range(TQ_SWA // WINDOW):
        bias = bias_r0 if r == 0 else bias_any
        rows = slice(r * WINDOW, (r + 1) * WINDOW)
        win = slice(r * WINDOW, (r + 2) * WINDOW)
        v_pair = kv_sc[win, 2 * LANES:3 * LANES]
        for p in range(SWA_Q_HEADS // 2):
            q_pair = q_ref[rows, p * LANES:(p + 1) * LANES]
            g = (2 * p) // SWA_GROUP
            k_dup = kv_sc[win, g * LANES:(g + 1) * LANES]
            halves = []
            for hh in range(2):
                h = 2 * p + hh
                q_h = jnp.where(low if hh == 0 else ~low, q_pair, jnp.zeros_like(q_pair))
                s = lax.dot_general(q_h, k_dup, (((1,), (1,)), ((), ())),
                                    preferred_element_type=F32) + bias
                sink = sinks_ref[h]
                m = jnp.maximum(jnp.max(s, axis=1, keepdims=True), sink)
                pr = jnp.exp(s - m)
                den = jnp.sum(pr, axis=1, keepdims=True) + jnp.exp(sink - m)
                o = jnp.dot(pr.astype(BF16), v_pair, preferred_element_type=F32)
                o = o * (1.0 / den)
                if g != hh:
                    o = pltpu.roll(o, HEAD_DIM, 1)
                halves.append(o)
            o_ref[rows, p * LANES:(p + 1) * LANES] = jnp.where(low, halves[0], halves[1]).astype(BF16)


def _swa(proj, sinks, batch, seq):
    t = proj.shape[0]
    nsq = seq // TQ_SWA
    per = TQ_SWA // WINDOW
    tail_blk = COL_TAIL // TAIL_W
    return pl.pallas_call(
        _swa_kernel,
        grid=(batch, nsq),
        in_specs=[
            pl.BlockSpec(memory_space=pltpu.SMEM),
            pl.BlockSpec((TQ_SWA, SWA_Q_W), lambda b, i: (b * nsq + i, 0)),
            pl.BlockSpec((TQ_SWA, TAIL_W), lambda b, i: (b * nsq + i, tail_blk)),
            pl.BlockSpec((WINDOW, TAIL_W),
                         lambda b, i: (jnp.maximum((b * nsq + i) * per - 1, 0), tail_blk)),
        ],
        out_specs=pl.BlockSpec((TQ_SWA, SWA_Q_W), lambda b, i: (b * nsq + i, 0)),
        scratch_shapes=[pltpu.VMEM((WINDOW + TQ_SWA, 3 * LANES), BF16)],
        out_shape=jax.ShapeDtypeStruct((t, SWA_Q_W), BF16),
        compiler_params=pltpu.CompilerParams(
            dimension_semantics=("arbitrary", "arbitrary")),
        name="swa",
    )(sinks, proj, proj, proj)


def _fox_kernel(base_ref, q_ref, k_ref, aug_ref, vt_ref, o_ref,
                m_sc, acc_sc, qt_sc, bm_sc, *bufs):
    b = pl.program_id(0)
    p = pl.program_id(1)
    qi = pl.program_id(2)
    s_bufs, p_bufs = bufs[:FOX_SLOTS], bufs[FOX_SLOTS:]
    dz = pl.multiple_of(jnp.minimum(qi, 0), FOX_ROWS)
    nb = k_ref.shape[0] // TK_FOX
    per = TQ_FOX // TK_FOX
    q = q_ref[...]
    lane = lax.broadcasted_iota(jnp.int32, (TQ_FOX, LANES), 1)
    low = lane < HEAD_DIM
    zero = jnp.zeros_like(q)
    for hh in range(2):
        h = 2 * p + hh
        sel = (lane == h) | (lane == h + FOX_HEADS) | (lane == h + 2 * FOX_HEADS)
        ones = jnp.where(sel, 1.0, 0.0)
        q_h = jnp.where(low if hh == 0 else ~low, q, zero).astype(F32)
        q_full = jnp.concatenate([q_h, ones], axis=1)
        qt_sc[hh] = q_full.T.astype(BF16)

    m_sc[...] = jnp.full(m_sc.shape, NEG, F32)
    acc_sc[...] = jnp.zeros(acc_sc.shape, F32)
    ones_rows = jnp.ones((FOX_DEN_ROWS, TK_FOX), BF16)

    def scores(blk, n, q_lo, diag):
        hh, slot, w = n % 2, n % FOX_SLOTS, TQ_FOX - q_lo
        ks = pl.ds(pl.multiple_of(blk * TQ_FOX + (n // 2) * TK_FOX, TK_FOX), TK_FOX)
        k_full = jnp.concatenate([k_ref[ks, :], aug_ref[ks, :]], axis=1)
        s = jnp.dot(k_full, qt_sc[hh, :, q_lo:], preferred_element_type=F32)
        if diag is not None:
            row = lax.broadcasted_iota(jnp.int32, s.shape, 0)
            col = lax.broadcasted_iota(jnp.int32, s.shape, 1)
            limit = 0 if diag is True else jnp.where(diag, 0, TK_FOX)
            s = jnp.where(row - col <= limit, s, NEG)
        s_bufs[slot][pl.ds(dz, TK_FOX), 0:w] = s
        bm_sc[slot, :, 0:w] = jnp.max(s, axis=0, keepdims=True)

    def finish(blk, n, q_lo):
        hh, slot, p_slot, w = n % 2, n % FOX_SLOTS, n % 2, TQ_FOX - q_lo
        ks = pl.ds(pl.multiple_of(blk * TQ_FOX + (n // 2) * TK_FOX, TK_FOX), TK_FOX)
        base = base_ref[(b * nb + blk * per + n // 2) * FOX_HEADS + 2 * p + hh]
        m_prev = m_sc[hh, :, q_lo:] - base
        m_new = jnp.maximum(m_prev, bm_sc[slot, :, 0:w])
        alpha = jnp.exp2(m_prev - m_new)
        for r in range(0, TK_FOX, FOX_ROWS):
            rows = pl.ds(dz + r, FOX_ROWS)
            pr = jnp.exp2(s_bufs[slot][rows, 0:w] - m_new)
            p_bufs[p_slot][rows, 0:w] = pr.astype(BF16)
        vt_h = jnp.concatenate([vt_ref[hh * HEAD_DIM:(hh + 1) * HEAD_DIM, ks], ones_rows], axis=0)
        acc_sc[hh, :, q_lo:] = alpha * acc_sc[hh, :, q_lo:] + jnp.dot(
            vt_h, p_bufs[p_slot][pl.ds(dz, TK_FOX), 0:w], preferred_element_type=F32)
        m_sc[hh, :, q_lo:] = m_new + base

    for n in range(FOX_AHEAD):
        scores(0, n, 0, qi == 0)

    def body(j, carry):
        for n in range(FOX_CHAINS):
            m = n + FOX_AHEAD
            if m < FOX_CHAINS:
                scores(j, m, 0, None)
            else:
                scores(j + 1, m - FOX_CHAINS, 0, j + 1 == qi)
            finish(j, n, 0)
        return carry

    lax.fori_loop(0, qi, body, 0)
    for n in range(FOX_CHAINS):
        m = n + FOX_AHEAD
        if m < FOX_CHAINS:
            scores(qi, m, (m // 2) * TK_FOX, True)
        finish(qi, n, (n // 2) * TK_FOX)

    o_t = jnp.concatenate(
        [acc_sc[hh, 0:HEAD_DIM, :] * (1.0 / acc_sc[hh, HEAD_DIM:HEAD_DIM + 1, :]) for hh in range(2)],
        axis=0)
    o_ref[...] = o_t.T.astype(BF16)


def _fox(proj, aug, base, v_t, batch, seq):
    t = proj.shape[0]
    nq = seq // TQ_FOX
    pairs = FOX_HEADS // 2
    return pl.pallas_call(
        _fox_kernel,
        grid=(batch, pairs, nq),
        in_specs=[
            pl.BlockSpec(memory_space=pltpu.SMEM),
            pl.BlockSpec((TQ_FOX, LANES), lambda b, p, i: (b * nq + i, COL_FQ // LANES + p)),
            pl.BlockSpec((seq, LANES), lambda b, p, i: (b, COL_FK // LANES + p)),
            pl.BlockSpec((seq, LANES), lambda b, p, i: (b, 0)),
            pl.BlockSpec((None, LANES, seq), lambda b, p, i: (b, p, 0)),
        ],
        out_specs=pl.BlockSpec((TQ_FOX, LANES), lambda b, p, i: (b * nq + i, p)),
        out_shape=jax.ShapeDtypeStruct((t, FOX_W), BF16),
        scratch_shapes=[
            pltpu.VMEM((2, 1, TQ_FOX), F32),
            pltpu.VMEM((2, HEAD_DIM + FOX_DEN_ROWS, TQ_FOX), F32),
            pltpu.VMEM((2, 2 * LANES, TQ_FOX), BF16),
            pltpu.VMEM((FOX_SLOTS, 1, TQ_FOX), F32),
        ] + [pltpu.VMEM((TK_FOX, TQ_FOX), F32)] * FOX_SLOTS + [
            pltpu.VMEM((TK_FOX, TQ_FOX), BF16),
            pltpu.VMEM((TK_FOX, TQ_FOX), BF16),
        ],
        compiler_params=pltpu.CompilerParams(
            dimension_semantics=("arbitrary", "arbitrary", "arbitrary")),
        name="fox",
    )(base, proj, proj, aug, v_t)


def _merge_kernel(oa_ref, ob_ref, ga_ref, gb_ref, x_ref, wa_ref, wb_ref, wo_ref, gain_ref,
                  x1_ref, h2_ref):
    ya = jnp.dot(oa_ref[...], wa_ref[...], preferred_element_type=F32)
    yb = jnp.dot(ob_ref[...], wb_ref[...], preferred_element_type=F32)
    def gate(ref):
        return 1.0 / (1.0 + jnp.exp(-ref[...].astype(F32)))

    merged = gate(ga_ref) * ya + gate(gb_ref) * yb
    x1 = x_ref[...] + jnp.dot(merged.astype(BF16), wo_ref[...], preferred_element_type=F32)
    x1_ref[...] = x1
    h2_ref[...] = _rmsnorm_rows(x1, gain_ref[...]).astype(BF16)


def _merge(o_a, o_b, proj, x2, wa, wb, wo, gain):
    t = x2.shape[0]
    tm = TM_MERGE
    const = dict(pipeline_mode=pl.Buffered(1))
    return pl.pallas_call(
        _merge_kernel,
        grid=(t // tm,),
        in_specs=[
            pl.BlockSpec((tm, SWA_Q_W), lambda i: (i, 0)),
            pl.BlockSpec((tm, FOX_W), lambda i: (i, 0)),
            pl.BlockSpec((tm, D_MODEL), lambda i: (i, COL_GA // D_MODEL)),
            pl.BlockSpec((tm, D_MODEL), lambda i: (i, COL_GB // D_MODEL)),
            pl.BlockSpec((tm, D_MODEL), lambda i: (i, 0)),
            pl.BlockSpec((SWA_Q_W, D_MODEL), lambda i: (0, 0), **const),
            pl.BlockSpec((FOX_W, D_MODEL), lambda i: (0, 0), **const),
            pl.BlockSpec((D_MODEL, D_MODEL), lambda i: (0, 0), **const),
            pl.BlockSpec((1, D_MODEL), lambda i: (0, 0), **const),
        ],
        out_specs=[
            pl.BlockSpec((tm, D_MODEL), lambda i: (i, 0)),
            pl.BlockSpec((tm, D_MODEL), lambda i: (i, 0)),
        ],
        out_shape=[
            jax.ShapeDtypeStruct((t, D_MODEL), F32),
            jax.ShapeDtypeStruct((t, D_MODEL), BF16),
        ],
        compiler_params=pltpu.CompilerParams(
            dimension_semantics=("arbitrary",),
            vmem_limit_bytes=56 * MIB),
        name="merge",
    )(o_a, o_b, proj, proj, x2, wa, wb, wo, gain)


def _mlp_kernel(h2_ref, x1_ref, wup_ref, wdn_ref, gain_ref, o_ref, *, final_norm):
    f = pl.program_id(1)

    @pl.when(f == 0)
    def _():
        o_ref[...] = x1_ref[...]

    u = jnp.maximum(jnp.dot(h2_ref[...], wup_ref[...], preferred_element_type=F32), 0.0)
    o_ref[...] += jnp.dot((u * u).astype(BF16), wdn_ref[...], preferred_element_type=F32)

    if final_norm:
        @pl.when(f == pl.num_programs(1) - 1)
        def _():
            o_ref[...] = _rmsnorm_rows(o_ref[...], gain_ref[...])


def _mlp(h2, x1, wup, wdn, gain, final_norm):
    t = x1.shape[0]
    tm, tf = TM_MLP, TF_MLP
    return pl.pallas_call(
        functools.partial(_mlp_kernel, final_norm=final_norm),
        grid=(t // tm, D_FF // tf),
        in_specs=[
            pl.BlockSpec((tm, D_MODEL), lambda i, f: (i, 0)),
            pl.BlockSpec((tm, D_MODEL), lambda i, f: (i, 0)),
            pl.BlockSpec((None, D_MODEL, tf), lambda i, f: (f, 0, 0)),
            pl.BlockSpec((tf, D_MODEL), lambda i, f: (f, 0)),
            pl.BlockSpec((1, D_MODEL), lambda i, f: (0, 0)),
        ],
        out_specs=pl.BlockSpec((tm, D_MODEL), lambda i, f: (i, 0)),
        out_shape=jax.ShapeDtypeStruct((t, D_MODEL), F32),
        compiler_params=pltpu.CompilerParams(
            dimension_semantics=("arbitrary", "arbitrary"),
            vmem_limit_bytes=48 * MIB),
        name="mlp",
    )(h2, x1, wup, wdn, gain)


def _pack_w_in(w, scale):
    o = 0
    a_q = w[:, o:o + SWA_Q_W] * scale; o += SWA_Q_W
    a_k = w[:, o:o + SWA_KV_HEADS * HEAD_DIM]; o += SWA_KV_HEADS * HEAD_DIM
    a_v = w[:, o:o + SWA_KV_HEADS * HEAD_DIM]; o += SWA_KV_HEADS * HEAD_DIM
    f_q = w[:, o:o + FOX_W] * (scale * LOG2E); o += FOX_W
    f_k = w[:, o:o + FOX_W]; o += FOX_W
    f_v = w[:, o:o + FOX_W]; o += FOX_W
    f_l = w[:, o:o + FOX_HEADS]; o += FOX_HEADS
    g_a = w[:, o:o + D_MODEL]; o += D_MODEL
    g_b = w[:, o:o + D_MODEL]; o += D_MODEL
    k0, k1 = a_k[:, :HEAD_DIM], a_k[:, HEAD_DIM:]
    pad = jnp.zeros((w.shape[0], LANES - FOX_HEADS), w.dtype)
    w_all = jnp.concatenate([a_q, f_q, g_a, g_b, f_k, k0, k0, k1, k1, a_v, f_l, pad],
                            axis=1).astype(BF16)
    return _col_blocks(w_all, TN_IN), f_v.T.astype(BF16)


def _col_blocks(w, tn):
    k, n = w.shape
    return w.reshape(k, n // tn, tn).transpose(1, 0, 2)


def kernel(x, positions, attn_norm, w_in, fox_f_bias, swa_sinks, w_branch_swa, w_branch_fox,
           w_out, mlp_norm, w_up, w_down, final_norm):
    batch, seq, d = x.shape
    depth = w_in.shape[0]
    assert d == D_MODEL and seq % TM_IN == 0 and seq % TQ_FOX == 0
    t = batch * seq
    scale = HEAD_DIM ** -0.5
    inv_freq = ROPE_THETA ** (-jnp.arange(0, HEAD_DIM, 2, dtype=F32) / HEAD_DIM)
    invf = jnp.tile(inv_freq, LANES // HALF)[None, :]
    pos2 = positions.reshape(t, 1)
    x2 = x.reshape(t, d)
    for l in range(depth):
        w_all, w_vt = _pack_w_in(w_in[l], scale)
        fbias = jnp.pad(fox_f_bias[l].astype(F32), (0, LANES - FOX_HEADS))[None, :]
        proj, logf, v_t = _inproj(x2, pos2, invf, attn_norm[l][None, :].astype(F32), w_all, w_vt,
                                  fbias, batch, seq)
        aug, base = _decay(logf, batch, seq)
        base = base[:, :, 0, :FOX_HEADS].reshape(-1)
        o_a = _swa(proj, swa_sinks[l].astype(F32), batch, seq)
        o_b = _fox(proj, aug, base, v_t, batch, seq)
        x1, h2 = _merge(o_a, o_b, proj, x2,
                        w_branch_swa[l].astype(BF16), w_branch_fox[l].astype(BF16),
                        w_out[l].astype(BF16), mlp_norm[l][None, :].astype(F32))
        x2 = _mlp(h2, x1, _col_blocks(w_up[l].astype(BF16), TF_MLP), w_down[l].astype(BF16),
                  final_norm[None, :].astype(F32), final_norm=(l == depth - 1))
    return x2.reshape(batch, seq, d)
```

```python
import functools

import jax
import jax.numpy as jnp
from jax import lax
from jax.experimental import pallas as pl
from jax.experimental.pallas import tpu as pltpu

F32 = jnp.float32
BF16 = jnp.bfloat16

D_MODEL = 2048
HEAD_DIM = 64
HALF = HEAD_DIM // 2
SWA_Q_HEADS = 16
SWA_KV_HEADS = 2
SWA_GROUP = SWA_Q_HEADS // SWA_KV_HEADS
WINDOW = 128
FOX_HEADS = 16
D_FF = 4 * D_MODEL
ROPE_THETA = 10000.0
RMS_EPS = 1e-6
SWA_Q_W = SWA_Q_HEADS * HEAD_DIM
FOX_W = FOX_HEADS * HEAD_DIM

LANES = 128
NEG = -1e30
MIB = 1024 * 1024
LOG2E = 1.4426950408889634

COL_AQ = 0
COL_FQ = 1024
COL_GA = 2048
COL_GB = 4096
COL_FK = 6144
COL_TAIL = 7168
TAIL_W = 512
D_PROJ = COL_TAIL + TAIL_W

TM_IN = 1024
TN_IN = 512
NJ_PROJ = D_PROJ // TN_IN
NJ_IN = NJ_PROJ + FOX_W // TN_IN

TQ_SWA = 512
TQ_FOX = 1024
TK_FOX = 256
FOX_AHEAD = 2
FOX_CHAINS = 2 * (TQ_FOX // TK_FOX)
FOX_SLOTS = 4
assert FOX_CHAINS % FOX_SLOTS == 0 and FOX_AHEAD < FOX_SLOTS
FOX_ROWS = 16
FOX_DEN_ROWS = 16
TM_MERGE = 512
TM_MLP = 512
TF_MLP = 1024


def _rmsnorm_rows(x, gain):
    ms = jnp.mean(x * x, axis=-1, keepdims=True)
    return x * lax.rsqrt(ms + RMS_EPS) * gain


def _split3(x):
    hi = x.astype(BF16)
    r1 = x - hi.astype(F32)
    mid = r1.astype(BF16)
    lo = (r1 - mid.astype(F32)).astype(BF16)
    return hi, mid, lo


def _inproj_kernel(pos_ref, invf_ref, x_ref, gain_ref, w_ref, wvt_ref, fbias_ref,
                   proj_ref, logf_ref, vt_ref, h_sc, cos_sc, sa_sc, sb_sc):
    j = pl.program_id(1)

    @pl.when(j == 0)
    def _():
        pos = pos_ref[...].astype(F32)
        pos_t = jnp.concatenate([pos, jnp.zeros((LANES - pos.shape[0], LANES), F32)], axis=0).T
        for r in range(TM_IN // LANES):
            rows = slice(r * LANES, (r + 1) * LANES)
            h_sc[rows, :] = _rmsnorm_rows(x_ref[rows, :], gain_ref[...]).astype(BF16)
            ang = pos_t[:, r:r + 1] * invf_ref[...]
            c = jnp.cos(ang)
            s = jnp.sin(ang)
            lane = lax.broadcasted_iota(jnp.int32, ang.shape, 1)
            first = (lane % HEAD_DIM) < HALF
            cos_sc[rows, :] = c
            sa_sc[rows, :] = jnp.where(first, -s, 0.0)
            sb_sc[rows, :] = jnp.where(first, 0.0, s)

    def project():
        return jnp.dot(h_sc[...], w_ref[...], preferred_element_type=F32)

    def rope_store(acc, c0, c1):
        for c in range(c0, c1, LANES):
            blk = acc[:, c:c + LANES]
            out = (blk * cos_sc[...]
                   + pltpu.roll(blk, LANES - HALF, 1) * sa_sc[...]
                   + pltpu.roll(blk, HALF, 1) * sb_sc[...])
            proj_ref[:, c:c + LANES] = out.astype(BF16)

    @pl.when(j < COL_FQ // TN_IN)
    def _():
        rope_store(project(), 0, TN_IN)

    @pl.when((j >= COL_FQ // TN_IN) & (j < COL_TAIL // TN_IN))
    def _():
        proj_ref[...] = project().astype(BF16)

    @pl.when(j == COL_TAIL // TN_IN)
    def _():
        acc = project()
        rope_store(acc, 0, 2 * LANES)
        proj_ref[:, 2 * LANES:3 * LANES] = acc[:, 2 * LANES:3 * LANES].astype(BF16)
        proj_ref[:, 3 * LANES:4 * LANES] = jnp.zeros((TM_IN, LANES), BF16)
        z = acc[:, 3 * LANES:4 * LANES] + fbias_ref[...]
        logf_ref[...] = jnp.minimum(z, 0.0) - jnp.log1p(jnp.exp(-jnp.abs(z)))

    @pl.when(j >= NJ_PROJ)
    def _():
        vt = lax.dot_general(wvt_ref[...], h_sc[...], (((1,), (1,)), ((), ())),
                             preferred_element_type=F32)
        vt_ref[...] = vt.astype(BF16)


def _inproj(x2, pos2, invf, gain, w_all, w_vt, fbias, batch, seq):
    t = x2.shape[0]
    nsb = seq // TM_IN
    last = NJ_PROJ - 1

    def vt_blk(j):
        return jnp.clip(j - NJ_PROJ, 0, FOX_W // TN_IN - 1)

    return pl.pallas_call(
        _inproj_kernel,
        grid=(t // TM_IN, NJ_IN),
        in_specs=[
            pl.BlockSpec((TM_IN // LANES, LANES), lambda i, j: (i, 0)),
            pl.BlockSpec((1, LANES), lambda i, j: (0, 0)),
            pl.BlockSpec((TM_IN, D_MODEL), lambda i, j: (i, 0)),
            pl.BlockSpec((1, D_MODEL), lambda i, j: (0, 0)),
            pl.BlockSpec((D_MODEL, TN_IN), lambda i, j: (0, jnp.minimum(j, last))),
            pl.BlockSpec((TN_IN, D_MODEL), lambda i, j: (vt_blk(j), 0)),
            pl.BlockSpec((1, LANES), lambda i, j: (0, 0)),
        ],
        out_specs=[
            pl.BlockSpec((TM_IN, TN_IN), lambda i, j: (i, jnp.minimum(j, last))),
            pl.BlockSpec((TM_IN, LANES), lambda i, j: (i, 0)),
            pl.BlockSpec((None, TN_IN, TM_IN), lambda i, j: (i // nsb, vt_blk(j), i % nsb)),
        ],
        out_shape=[
            jax.ShapeDtypeStruct((t, D_PROJ), BF16),
            jax.ShapeDtypeStruct((t, LANES), F32),
            jax.ShapeDtypeStruct((batch, FOX_W, seq), BF16),
        ],
        scratch_shapes=[
            pltpu.VMEM((TM_IN, D_MODEL), BF16),
            pltpu.VMEM((TM_IN, LANES), F32),
            pltpu.VMEM((TM_IN, LANES), F32),
            pltpu.VMEM((TM_IN, LANES), F32),
        ],
        compiler_params=pltpu.CompilerParams(
            dimension_semantics=("arbitrary", "arbitrary"),
            vmem_limit_bytes=52 * MIB),
        name="inproj",
    )(pos2, invf, x2, gain, w_all, w_vt, fbias)


def _decay_kernel(logf_ref, aug_ref, base_ref, carry_sc):
    sb = pl.program_id(1)

    @pl.when(sb == 0)
    def _():
        carry_sc[...] = jnp.zeros(carry_sc.shape, F32)

    x2 = logf_ref[...] * LOG2E
    row = lax.broadcasted_iota(jnp.int32, (TK_FOX, TK_FOX), 0)
    col = lax.broadcasted_iota(jnp.int32, (TK_FOX, TK_FOX), 1)
    tri = jnp.where(col <= row, 1.0, 0.0).astype(BF16)
    hi, mid, lo = _split3(x2)
    cl = (jnp.dot(tri, hi, preferred_element_type=F32)
          + jnp.dot(tri, mid, preferred_element_type=F32)
          + jnp.dot(tri, lo, preferred_element_type=F32))
    yh, ym, yl = _split3(-cl)
    lane = lax.broadcasted_iota(jnp.int32, (TK_FOX, LANES), 1)
    zero = jnp.zeros((TK_FOX, LANES), F32)
    pieces = jnp.where(lane < FOX_HEADS, yh.astype(F32),
                       jnp.where(lane < 2 * FOX_HEADS, pltpu.roll(ym.astype(F32), FOX_HEADS, 1),
                                 jnp.where(lane < 3 * FOX_HEADS,
                                           pltpu.roll(yl.astype(F32), 2 * FOX_HEADS, 1), zero)))
    aug_ref[...] = pieces.astype(BF16)
    base_ref[...] = -carry_sc[...]
    carry_sc[...] = carry_sc[...] + cl[TK_FOX - 1:TK_FOX, :]


def _decay(logf, batch, seq):
    t = logf.shape[0]
    nb = seq // TK_FOX
    return pl.pallas_call(
        _decay_kernel,
        grid=(batch, nb),
        in_specs=[pl.BlockSpec((TK_FOX, LANES), lambda b, s: (b * nb + s, 0))],
        out_specs=[
            pl.BlockSpec((TK_FOX, LANES), lambda b, s: (b * nb + s, 0)),
            pl.BlockSpec((None, None, 1, LANES), lambda b, s: (b, s, 0, 0)),
        ],
        out_shape=[
            jax.ShapeDtypeStruct((t, LANES), BF16),
            jax.ShapeDtypeStruct((batch, nb, 1, LANES), F32),
        ],
        scratch_shapes=[pltpu.VMEM((1, LANES), F32)],
        compiler_params=pltpu.CompilerParams(
            dimension_semantics=("arbitrary", "arbitrary")),
        name="decay",
    )(logf)


def _swa_kernel(sinks_ref, q_ref, cur_ref, prev_ref, o_ref, kv_sc):
    i = pl.program_id(1)
    kvw = 3 * LANES
    kv_sc[0:WINDOW, :] = prev_ref[:, 0:kvw]
    kv_sc[WINDOW:WINDOW + TQ_SWA, :] = cur_ref[:, 0:kvw]

    row = lax.broadcasted_iota(jnp.int32, (WINDOW, 2 * WINDOW), 0)
    col = lax.broadcasted_iota(jnp.int32, (WINDOW, 2 * WINDOW), 1)
    allowed = (col > row) & (col <= row + WINDOW)
    bias_any = jnp.where(allowed, 0.0, NEG)
    bias_first = jnp.where(allowed & (col >= WINDOW), 0.0, NEG)
    bias_r0 = jnp.where(i == 0, bias_first, bias_any)
    low = lax.broadcasted_iota(jnp.int32, (WINDOW, LANES), 1) < HEAD_DIM

    for r in range(TQ_SWA // WINDOW):
        bias = bias_r0 if r == 0 else bias_any
        rows = slice(r * WINDOW, (r + 1) * WINDOW)
        win = slice(r * WINDOW, (r + 2) * WINDOW)
        v_pair = kv_sc[win, 2 * LANES:3 * LANES]
        for p in range(SWA_Q_HEADS // 2):
            q_pair = q_ref[rows, p * LANES:(p + 1) * LANES]
            g = (2 * p) // SWA_GROUP
            k_dup = kv_sc[win, g * LANES:(g + 1) * LANES]
            halves = []
            for hh in range(2):
                h = 2 * p + hh
                q_h = jnp.where(low if hh == 0 else ~low, q_pair, jnp.zeros_like(q_pair))
                s = lax.dot_general(q_h, k_dup, (((1,), (1,)), ((), ())),
                                    preferred_element_type=F32) + bias
                sink = sinks_ref[h]
                m = jnp.maximum(jnp.max(s, axis=1, keepdims=True), sink)
                pr = jnp.exp(s - m)
                den = jnp.sum(pr, axis=1, keepdims=True) + jnp.exp(sink - m)
                o = jnp.dot(pr.astype(BF16), v_pair, preferred_element_type=F32)
                o = o * (1.0 / den)
                if g != hh:
                    o = pltpu.roll(o, HEAD_DIM, 1)
                halves.append(o)
            o_ref[rows, p * LANES:(p + 1) * LANES] = jnp.where(low, halves[0], halves[1]).astype(BF16)


def _swa(proj, sinks, batch, seq):
    t = proj.shape[0]
    nsq = seq // TQ_SWA
    per = TQ_SWA // WINDOW
    tail_blk = COL_TAIL // TAIL_W
    return pl.pallas_call(
        _swa_kernel,
        grid=(batch, nsq),
        in_specs=[
            pl.BlockSpec(memory_space=pltpu.SMEM),
            pl.BlockSpec((TQ_SWA, SWA_Q_W), lambda b, i: (b * nsq + i, 0)),
            pl.BlockSpec((TQ_SWA, TAIL_W), lambda b, i: (b * nsq + i, tail_blk)),
            pl.BlockSpec((WINDOW, TAIL_W),
                         lambda b, i: (jnp.maximum((b * nsq + i) * per - 1, 0), tail_blk)),
        ],
        out_specs=pl.BlockSpec((TQ_SWA, SWA_Q_W), lambda b, i: (b * nsq + i, 0)),
        scratch_shapes=[pltpu.VMEM((WINDOW + TQ_SWA, 3 * LANES), BF16)],
        out_shape=jax.ShapeDtypeStruct((t, SWA_Q_W), BF16),
        compiler_params=pltpu.CompilerParams(
            dimension_semantics=("arbitrary", "arbitrary")),
        name="swa",
    )(sinks, proj, proj, proj)


def _fox_kernel(base_ref, q_ref, k_ref, aug_ref, vt_ref, o_ref,
                m_sc, acc_sc, qt_sc, bm_sc, *bufs):
    b = pl.program_id(0)
    p = pl.program_id(1)
    qi = pl.program_id(2)
    s_bufs, p_bufs = bufs[:FOX_SLOTS], bufs[FOX_SLOTS:]
    dz = pl.multiple_of(jnp.minimum(qi, 0), FOX_ROWS)
    nb = k_ref.shape[0] // TK_FOX
    per = TQ_FOX // TK_FOX
    q = q_ref[...]
    lane = lax.broadcasted_iota(jnp.int32, (TQ_FOX, LANES), 1)
    low = lane < HEAD_DIM
    zero = jnp.zeros_like(q)
    for hh in range(2):
        h = 2 * p + hh
        sel = (lane == h) | (lane == h + FOX_HEADS) | (lane == h + 2 * FOX_HEADS)
        ones = jnp.where(sel, 1.0, 0.0)
        q_h = jnp.where(low if hh == 0 else ~low, q, zero).astype(F32)
        q_full = jnp.concatenate([q_h, ones], axis=1)
        qt_sc[hh] = q_full.T.astype(BF16)

    m_sc[...] = jnp.full(m_sc.shape, NEG, F32)
    acc_sc[...] = jnp.zeros(acc_sc.shape, F32)
    ones_rows = jnp.ones((FOX_DEN_ROWS, TK_FOX), BF16)

    def scores(blk, n, q_lo, diag):
        hh, slot, w = n % 2, n % FOX_SLOTS, TQ_FOX - q_lo
        ks = pl.ds(pl.multiple_of(blk * TQ_FOX + (n // 2) * TK_FOX, TK_FOX), TK_FOX)
        k_full = jnp.concatenate([k_ref[ks, :], aug_ref[ks, :]], axis=1)
        s = jnp.dot(k_full, qt_sc[hh, :, q_lo:], preferred_element_type=F32)
        if diag is not None:
            row = lax.broadcasted_iota(jnp.int32, s.shape, 0)
            col = lax.broadcasted_iota(jnp.int32, s.shape, 1)
            limit = 0 if diag is True else jnp.where(diag, 0, TK_FOX)
            s = jnp.where(row - col <= limit, s, NEG)
        s_bufs[slot][pl.ds(dz, TK_FOX), 0:w] = s
        bm_sc[slot, :, 0:w] = jnp.max(s, axis=0, keepdims=True)

    def finish(blk, n, q_lo):
        hh, slot, p_slot, w = n % 2, n % FOX_SLOTS, n % 2, TQ_FOX - q_lo
        ks = pl.ds(pl.multiple_of(blk * TQ_FOX + (n // 2) * TK_FOX, TK_FOX), TK_FOX)
        base = base_ref[(b * nb + blk * per + n // 2) * FOX_HEADS + 2 * p + hh]
        m_prev = m_sc[hh, :, q_lo:] - base
        m_new = jnp.maximum(m_prev, bm_sc[slot, :, 0:w])
        alpha = jnp.exp2(m_prev - m_new)
        for r in range(0, TK_FOX, FOX_ROWS):
            rows = pl.ds(dz + r, FOX_ROWS)
            pr = jnp.exp2(s_bufs[slot][rows, 0:w] - m_new)
            p_bufs[p_slot][rows, 0:w] = pr.astype(BF16)
        vt_h = jnp.concatenate([vt_ref[hh * HEAD_DIM:(hh + 1) * HEAD_DIM, ks], ones_rows], axis=0)
        acc_sc[hh, :, q_lo:] = alpha * acc_sc[hh, :, q_lo:] + jnp.dot(
            vt_h, p_bufs[p_slot][pl.ds(dz, TK_FOX), 0:w], preferred_element_type=F32)
        m_sc[hh, :, q_lo:] = m_new + base

    for n in range(FOX_AHEAD):
        scores(0, n, 0, qi == 0)

    def body(j, carry):
        for n in range(FOX_CHAINS):
            m = n + FOX_AHEAD
            if m < FOX_CHAINS:
                scores(j, m, 0, None)
            else:
                scores(j + 1, m - FOX_CHAINS, 0, j + 1 == qi)
            finish(j, n, 0)
        return carry

    lax.fori_loop(0, qi, body, 0)
    for n in range(FOX_CHAINS):
        m = n + FOX_AHEAD
        if m < FOX_CHAINS:
            scores(qi, m, (m // 2) * TK_FOX, True)
        finish(qi, n, (n // 2) * TK_FOX)

    o_t = jnp.concatenate(
        [acc_sc[hh, 0:HEAD_DIM, :] * (1.0 / acc_sc[hh, HEAD_DIM:HEAD_DIM + 1, :]) for hh in range(2)],
        axis=0)
    o_ref[...] = o_t.T.astype(BF16)


def _fox(proj, aug, base, v_t, batch, seq):
    t = proj.shape[0]
    nq = seq // TQ_FOX
    pairs = FOX_HEADS // 2
    return pl.pallas_call(
        _fox_kernel,
        grid=(batch, pairs, nq),
        in_specs=[
            pl.BlockSpec(memory_space=pltpu.SMEM),
            pl.BlockSpec((TQ_FOX, LANES), lambda b, p, i: (b * nq + i, COL_FQ // LANES + p)),
            pl.BlockSpec((seq, LANES), lambda b, p, i: (b, COL_FK // LANES + p)),
            pl.BlockSpec((seq, LANES), lambda b, p, i: (b, 0)),
            pl.BlockSpec((None, LANES, seq), lambda b, p, i: (b, p, 0)),
        ],
        out_specs=pl.BlockSpec((TQ_FOX, LANES), lambda b, p, i: (b * nq + i, p)),
        out_shape=jax.ShapeDtypeStruct((t, FOX_W), BF16),
        scratch_shapes=[
            pltpu.VMEM((2, 1, TQ_FOX), F32),
            pltpu.VMEM((2, HEAD_DIM + FOX_DEN_ROWS, TQ_FOX), F32),
            pltpu.VMEM((2, 2 * LANES, TQ_FOX), BF16),
            pltpu.VMEM((FOX_SLOTS, 1, TQ_FOX), F32),
        ] + [pltpu.VMEM((TK_FOX, TQ_FOX), F32)] * FOX_SLOTS + [
            pltpu.VMEM((TK_FOX, TQ_FOX), BF16),
            pltpu.VMEM((TK_FOX, TQ_FOX), BF16),
        ],
        compiler_params=pltpu.CompilerParams(
            dimension_semantics=("arbitrary", "arbitrary", "arbitrary")),
        name="fox",
    )(base, proj, proj, aug, v_t)


def _merge_kernel(oa_ref, ob_ref, ga_ref, gb_ref, x_ref, wa_ref, wb_ref, wo_ref, gain_ref,
                  x1_ref, h2_ref):
    ya = jnp.dot(oa_ref[...], wa_ref[...], preferred_element_type=F32)
    yb = jnp.dot(ob_ref[...], wb_ref[...], preferred_element_type=F32)
    def gate(ref):
        return 1.0 / (1.0 + jnp.exp(-ref[...].astype(F32)))

    merged = gate(ga_ref) * ya + gate(gb_ref) * yb
    x1 = x_ref[...] + jnp.dot(merged.astype(BF16), wo_ref[...], preferred_element_type=F32)
    x1_ref[...] = x1
    h2_ref[...] = _rmsnorm_rows(x1, gain_ref[...]).astype(BF16)


def _merge(o_a, o_b, proj, x2, wa, wb, wo, gain):
    t = x2.shape[0]
    tm = TM_MERGE
    const = dict(pipeline_mode=pl.Buffered(1))
    return pl.pallas_call(
        _merge_kernel,
        grid=(t // tm,),
        in_specs=[
            pl.BlockSpec((tm, SWA_Q_W), lambda i: (i, 0)),
            pl.BlockSpec((tm, FOX_W), lambda i: (i, 0)),
            pl.BlockSpec((tm, D_MODEL), lambda i: (i, COL_GA // D_MODEL)),
            pl.BlockSpec((tm, D_MODEL), lambda i: (i, COL_GB // D_MODEL)),
            pl.BlockSpec((tm, D_MODEL), lambda i: (i, 0)),
            pl.BlockSpec((SWA_Q_W, D_MODEL), lambda i: (0, 0), **const),
            pl.BlockSpec((FOX_W, D_MODEL), lambda i: (0, 0), **const),
            pl.BlockSpec((D_MODEL, D_MODEL), lambda i: (0, 0), **const),
            pl.BlockSpec((1, D_MODEL), lambda i: (0, 0), **const),
        ],
        out_specs=[
            pl.BlockSpec((tm, D_MODEL), lambda i: (i, 0)),
            pl.BlockSpec((tm, D_MODEL), lambda i: (i, 0)),
        ],
        out_shape=[
            jax.ShapeDtypeStruct((t, D_MODEL), F32),
            jax.ShapeDtypeStruct((t, D_MODEL), BF16),
        ],
        compiler_params=pltpu.CompilerParams(
            dimension_semantics=("arbitrary",),
            vmem_limit_bytes=56 * MIB),
        name="merge",
    )(o_a, o_b, proj, proj, x2, wa, wb, wo, gain)


def _mlp_kernel(h2_ref, x1_ref, wup_ref, wdn_ref, gain_ref, o_ref, *, final_norm):
    f = pl.program_id(1)

    @pl.when(f == 0)
    def _():
        o_ref[...] = x1_ref[...]

    u = jnp.maximum(jnp.dot(h2_ref[...], wup_ref[...], preferred_element_type=F32), 0.0)
    o_ref[...] += jnp.dot((u * u).astype(BF16), wdn_ref[...], preferred_element_type=F32)

    if final_norm:
        @pl.when(f == pl.num_programs(1) - 1)
        def _():
            o_ref[...] = _rmsnorm_rows(o_ref[...], gain_ref[...])


def _mlp(h2, x1, wup, wdn, gain, final_norm):
    t = x1.shape[0]
    tm, tf = TM_MLP, TF_MLP
    return pl.pallas_call(
        functools.partial(_mlp_kernel, final_norm=final_norm),
        grid=(t // tm, D_FF // tf),
        in_specs=[
            pl.BlockSpec((tm, D_MODEL), lambda i, f: (i, 0)),
            pl.BlockSpec((tm, D_MODEL), lambda i, f: (i, 0)),
            pl.BlockSpec((D_MODEL, tf), lambda i, f: (0, f)),
            pl.BlockSpec((tf, D_MODEL), lambda i, f: (f, 0)),
            pl.BlockSpec((1, D_MODEL), lambda i, f: (0, 0)),
        ],
        out_specs=pl.BlockSpec((tm, D_MODEL), lambda i, f: (i, 0)),
        out_shape=jax.ShapeDtypeStruct((t, D_MODEL), F32),
        compiler_params=pltpu.CompilerParams(
            dimension_semantics=("arbitrary", "arbitrary"),
            vmem_limit_bytes=48 * MIB),
        name="mlp",
    )(h2, x1, wup, wdn, gain)


def _pack_w_in(w, scale):
    o = 0
    a_q = w[:, o:o + SWA_Q_W] * scale; o += SWA_Q_W
    a_k = w[:, o:o + SWA_KV_HEADS * HEAD_DIM]; o += SWA_KV_HEADS * HEAD_DIM
    a_v = w[:, o:o + SWA_KV_HEADS * HEAD_DIM]; o += SWA_KV_HEADS * HEAD_DIM
    f_q = w[:, o:o + FOX_W] * (scale * LOG2E); o += FOX_W
    f_k = w[:, o:o + FOX_W]; o += FOX_W
    f_v = w[:, o:o + FOX_W]; o += FOX_W
    f_l = w[:, o:o + FOX_HEADS]; o += FOX_HEADS
    g_a = w[:, o:o + D_MODEL]; o += D_MODEL
    g_b = w[:, o:o + D_MODEL]; o += D_MODEL
    k0, k1 = a_k[:, :HEAD_DIM], a_k[:, HEAD_DIM:]
    pad = jnp.zeros((w.shape[0], LANES - FOX_HEADS), w.dtype)
    w_all = jnp.concatenate([a_q, f_q, g_a, g_b, f_k, k0, k0, k1, k1, a_v, f_l, pad],
                            axis=1).astype(BF16)
    return w_all, f_v.T.astype(BF16)


def kernel(x, positions, attn_norm, w_in, fox_f_bias, swa_sinks, w_branch_swa, w_branch_fox,
           w_out, mlp_norm, w_up, w_down, final_norm):
    batch, seq, d = x.shape
    depth = w_in.shape[0]
    assert d == D_MODEL and seq % TM_IN == 0 and seq % TQ_FOX == 0
    t = batch * seq
    scale = HEAD_DIM ** -0.5
    inv_freq = ROPE_THETA ** (-jnp.arange(0, HEAD_DIM, 2, dtype=F32) / HEAD_DIM)
    invf = jnp.tile(inv_freq, LANES // HALF)[None, :]
    pos2 = positions.reshape(t // LANES, LANES)
    x2 = x.reshape(t, d)
    for l in range(depth):
        w_all, w_vt = _pack_w_in(w_in[l], scale)
        fbias = jnp.pad(fox_f_bias[l].astype(F32), (0, LANES - FOX_HEADS))[None, :]
        proj, logf, v_t = _inproj(x2, pos2, invf, attn_norm[l][None, :].astype(F32), w_all, w_vt,
                                  fbias, batch, seq)
        aug, base = _decay(logf, batch, seq)
        base = base[:, :, 0, :FOX_HEADS].reshape(-1)
        o_a = _swa(proj, swa_sinks[l].astype(F32), batch, seq)
        o_b = _fox(proj, aug, base, v_t, batch, seq)
        x1, h2 = _merge(o_a, o_b, proj, x2,
                        w_branch_swa[l].astype(BF16), w_branch_fox[l].astype(BF16),
                        w_out[l].astype(BF16), mlp_norm[l][None, :].astype(F32))
        x2 = _mlp(h2, x1, w_up[l].astype(BF16), w_down[l].astype(BF16),
                  final_norm[None, :].astype(F32), final_norm=(l == depth - 1))
    return x2.reshape(batch, seq, d)
```

```python
import functools

import jax
import jax.numpy as jnp
from jax import lax
from jax.experimental import pallas as pl
from jax.experimental.pallas import tpu as pltpu

F32 = jnp.float32
BF16 = jnp.bfloat16

D_MODEL = 2048
HEAD_DIM = 64
HALF = HEAD_DIM // 2
SWA_Q_HEADS = 16
SWA_KV_HEADS = 2
SWA_GROUP = SWA_Q_HEADS // SWA_KV_HEADS
WINDOW = 128
FOX_HEADS = 16
D_FF = 4 * D_MODEL
ROPE_THETA = 10000.0
RMS_EPS = 1e-6
SWA_Q_W = SWA_Q_HEADS * HEAD_DIM
FOX_W = FOX_HEADS * HEAD_DIM

LANES = 128
NEG = -1e30
MIB = 1024 * 1024
LOG2E = 1.4426950408889634

COL_GA = 0
COL_GB = 2048
COL_FQ = 4096
COL_FK = 5120
D_PROJ = 6144
ROW_AQ = 0
ROW_FV = SWA_Q_W
SWA_KV_W = SWA_KV_HEADS * HEAD_DIM

TM_IN = 1024
TN_IN = 512
NJ_TOK = D_PROJ // TN_IN
J_LOGF = NJ_TOK
J_AQ = J_LOGF + 1
J_KV = J_AQ + SWA_Q_W // TN_IN
J_FV = J_KV + 1
NJ_IN = J_FV + FOX_W // TN_IN

TQ_SWA = 512
TQ_FOX = 1024
TK_FOX = 256
FOX_AHEAD = 2
FOX_CHAINS = 2 * (TQ_FOX // TK_FOX)
FOX_SLOTS = 4
assert FOX_CHAINS % FOX_SLOTS == 0 and FOX_AHEAD < FOX_SLOTS
FOX_ROWS = 16
FOX_DEN_ROWS = 16
TM_MERGE = 512
TM_MLP = 512
TF_MLP = 1024


def _rmsnorm_rows(x, gain):
    ms = jnp.mean(x * x, axis=-1, keepdims=True)
    return x * lax.rsqrt(ms + RMS_EPS) * gain


def _split3(x):
    hi = x.astype(BF16)
    r1 = x - hi.astype(F32)
    mid = r1.astype(BF16)
    lo = (r1 - mid.astype(F32)).astype(BF16)
    return hi, mid, lo


def _inproj_kernel(pos_ref, invf_ref, x_ref, gain_ref, w_ref, wfl_ref, wqv_ref, wkv_ref, fbias_ref,
                   proj_ref, logf_ref, qv_ref, kv_ref, h_sc, cos_sc, sin_sc):
    j = pl.program_id(1)

    @pl.when(j == 0)
    def _():
        def body(r, _):
            rows = pl.ds(pl.multiple_of(r * LANES, LANES), LANES)
            h_sc[rows, :] = _rmsnorm_rows(x_ref[rows, :], gain_ref[...]).astype(BF16)
            return 0
        lax.fori_loop(0, TM_IN // LANES, body, 0)
        ang = invf_ref[...] * pos_ref[...].astype(F32)
        cos_sc[...] = jnp.cos(ang)
        sin_sc[...] = jnp.sin(ang)

    def features(wt_ref):
        return lax.dot_general(wt_ref[...], h_sc[...], (((1,), (1,)), ((), ())),
                               preferred_element_type=F32)

    def rope_rows(acc, heads, out_ref):
        for hd in range(heads):
            r0 = hd * HEAD_DIM
            t1 = acc[r0:r0 + HALF, :]
            t2 = acc[r0 + HALF:r0 + HEAD_DIM, :]
            out_ref[r0:r0 + HALF, :] = (t1 * cos_sc[...] - t2 * sin_sc[...]).astype(BF16)
            out_ref[r0 + HALF:r0 + HEAD_DIM, :] = (t2 * cos_sc[...] + t1 * sin_sc[...]).astype(BF16)

    @pl.when(j < NJ_TOK)
    def _():
        proj_ref[...] = jnp.dot(h_sc[...], w_ref[...], preferred_element_type=F32).astype(BF16)

    @pl.when(j == J_LOGF)
    def _():
        z = jnp.dot(h_sc[...], wfl_ref[...], preferred_element_type=F32) + fbias_ref[...]
        logf_ref[...] = jnp.minimum(z, 0.0) - jnp.log1p(jnp.exp(-jnp.abs(z)))

    @pl.when((j >= J_AQ) & (j < J_KV))
    def _():
        rope_rows(features(wqv_ref), TN_IN // HEAD_DIM, qv_ref)

    @pl.when(j == J_KV)
    def _():
        acc = features(wkv_ref)
        rope_rows(acc, SWA_KV_HEADS, kv_ref)
        kv_ref[SWA_KV_W:2 * SWA_KV_W, :] = acc[SWA_KV_W:2 * SWA_KV_W, :].astype(BF16)

    @pl.when(j >= J_FV)
    def _():
        qv_ref[...] = features(wqv_ref).astype(BF16)


def _inproj(x2, pos3, invf, gain, w_tok, w_fl, w_qv, w_kv, fbias, batch, seq):
    t = x2.shape[0]
    nsb = seq // TM_IN

    def qv_blk(j):
        return jnp.where(j >= J_FV, j - J_FV + SWA_Q_W // TN_IN,
                         jnp.clip(j - J_AQ, 0, SWA_Q_W // TN_IN - 1))

    def tok_blk(j):
        return jnp.minimum(j, NJ_TOK - 1)

    return pl.pallas_call(
        _inproj_kernel,
        grid=(t // TM_IN, NJ_IN),
        in_specs=[
            pl.BlockSpec((None, 1, TM_IN), lambda i, j: (i, 0, 0)),
            pl.BlockSpec((HALF, 1), lambda i, j: (0, 0)),
            pl.BlockSpec((TM_IN, D_MODEL), lambda i, j: (i, 0)),
            pl.BlockSpec((1, D_MODEL), lambda i, j: (0, 0)),
            pl.BlockSpec((D_MODEL, TN_IN), lambda i, j: (0, tok_blk(j))),
            pl.BlockSpec((D_MODEL, LANES), lambda i, j: (0, 0)),
            pl.BlockSpec((TN_IN, D_MODEL), lambda i, j: (qv_blk(j), 0)),
            pl.BlockSpec((2 * SWA_KV_W, D_MODEL), lambda i, j: (0, 0)),
            pl.BlockSpec((1, LANES), lambda i, j: (0, 0)),
        ],
        out_specs=[
            pl.BlockSpec((TM_IN, TN_IN), lambda i, j: (i, tok_blk(j))),
            pl.BlockSpec((TM_IN, LANES), lambda i, j: (i, 0)),
            pl.BlockSpec((None, TN_IN, TM_IN), lambda i, j: (i // nsb, qv_blk(j), i % nsb)),
            pl.BlockSpec((None, 2 * SWA_KV_W, TM_IN), lambda i, j: (i // nsb, 0, i % nsb)),
        ],
        out_shape=[
            jax.ShapeDtypeStruct((t, D_PROJ), BF16),
            jax.ShapeDtypeStruct((t, LANES), F32),
            jax.ShapeDtypeStruct((batch, SWA_Q_W + FOX_W, seq), BF16),
            jax.ShapeDtypeStruct((batch, 2 * SWA_KV_W, seq), BF16),
        ],
        scratch_shapes=[
            pltpu.VMEM((TM_IN, D_MODEL), BF16),
            pltpu.VMEM((HALF, TM_IN), F32),
            pltpu.VMEM((HALF, TM_IN), F32),
        ],
        compiler_params=pltpu.CompilerParams(
            dimension_semantics=("arbitrary", "arbitrary"),
            vmem_limit_bytes=52 * MIB),
        name="inproj",
    )(pos3, invf, x2, gain, w_tok, w_fl, w_qv, w_kv, fbias)


def _decay_kernel(logf_ref, aug_ref, base_ref, carry_sc):
    sb = pl.program_id(1)

    @pl.when(sb == 0)
    def _():
        carry_sc[...] = jnp.zeros(carry_sc.shape, F32)

    x2 = logf_ref[...] * LOG2E
    row = lax.broadcasted_iota(jnp.int32, (TK_FOX, TK_FOX), 0)
    col = lax.broadcasted_iota(jnp.int32, (TK_FOX, TK_FOX), 1)
    tri = jnp.where(col <= row, 1.0, 0.0).astype(BF16)
    hi, mid, lo = _split3(x2)
    cl = (jnp.dot(tri, hi, preferred_element_type=F32)
          + jnp.dot(tri, mid, preferred_element_type=F32)
          + jnp.dot(tri, lo, preferred_element_type=F32))
    yh, ym, yl = _split3(-cl)
    lane = lax.broadcasted_iota(jnp.int32, (TK_FOX, LANES), 1)
    zero = jnp.zeros((TK_FOX, LANES), F32)
    pieces = jnp.where(lane < FOX_HEADS, yh.astype(F32),
                       jnp.where(lane < 2 * FOX_HEADS, pltpu.roll(ym.astype(F32), FOX_HEADS, 1),
                                 jnp.where(lane < 3 * FOX_HEADS,
                                           pltpu.roll(yl.astype(F32), 2 * FOX_HEADS, 1), zero)))
    aug_ref[...] = pieces.astype(BF16)
    base_ref[...] = -carry_sc[...]
    carry_sc[...] = carry_sc[...] + cl[TK_FOX - 1:TK_FOX, :]


def _decay(logf, batch, seq):
    t = logf.shape[0]
    nb = seq // TK_FOX
    return pl.pallas_call(
        _decay_kernel,
        grid=(batch, nb),
        in_specs=[pl.BlockSpec((TK_FOX, LANES), lambda b, s: (b * nb + s, 0))],
        out_specs=[
            pl.BlockSpec((TK_FOX, LANES), lambda b, s: (b * nb + s, 0)),
            pl.BlockSpec((None, None, 1, LANES), lambda b, s: (b, s, 0, 0)),
        ],
        out_shape=[
            jax.ShapeDtypeStruct((t, LANES), BF16),
            jax.ShapeDtypeStruct((batch, nb, 1, LANES), F32),
        ],
        scratch_shapes=[pltpu.VMEM((1, LANES), F32)],
        compiler_params=pltpu.CompilerParams(
            dimension_semantics=("arbitrary", "arbitrary")),
        name="decay",
    )(logf)


def _swa_kernel(sinks_ref, qt_ref, cur_ref, prev_ref, o_ref, ktok_sc):
    i = pl.program_id(1)
    nblk = TQ_SWA // WINDOW
    gw = SWA_GROUP * WINDOW
    kv_win = jnp.concatenate([prev_ref[...], cur_ref[...]], axis=1)
    ktok_sc[...] = kv_win[0:SWA_KV_W, :].astype(F32).T.astype(BF16)

    row = lax.broadcasted_iota(jnp.int32, (2 * WINDOW, gw), 0)
    qix = lax.broadcasted_iota(jnp.int32, (2 * WINDOW, gw), 1) % WINDOW
    allowed = (row > qix) & (row <= qix + WINDOW)
    bias_any = jnp.where(allowed, 0.0, NEG)
    bias_first = jnp.where(allowed & (row >= WINDOW), 0.0, NEG)
    bias_r0 = jnp.where(i == 0, bias_first, bias_any)
    zeros_q = jnp.zeros((HEAD_DIM, WINDOW), BF16)
    ones_rows = jnp.ones((FOX_DEN_ROWS, 2 * WINDOW), BF16)

    units = [(r, g) for r in range(nblk) for g in range(SWA_KV_HEADS)]

    def scores(unit):
        r, g = unit
        cols = slice(r * WINDOW, (r + 1) * WINDOW)
        tiles = []
        for hq in range(SWA_GROUP):
            h = g * SWA_GROUP + hq
            q_h = qt_ref[h * HEAD_DIM:(h + 1) * HEAD_DIM, cols]
            tiles.append(jnp.concatenate([q_h, zeros_q] if g == 0 else [zeros_q, q_h], axis=0))
        q_g = jnp.concatenate(tiles, axis=1)
        k_win = ktok_sc[r * WINDOW:(r + 2) * WINDOW, :]
        s = jnp.dot(k_win, q_g, preferred_element_type=F32)
        return s + (bias_r0 if r == 0 else bias_any)

    def finish(s, unit):
        r, g = unit
        sink = jnp.concatenate(
            [jnp.full((1, WINDOW), sinks_ref[g * SWA_GROUP + hq] * LOG2E, F32)
             for hq in range(SWA_GROUP)], axis=1)
        m = jnp.maximum(jnp.max(s, axis=0, keepdims=True), sink)
        pr = jnp.exp2(s - m).astype(BF16)
        v_rows = slice(SWA_KV_W + g * HEAD_DIM, SWA_KV_W + (g + 1) * HEAD_DIM)
        vt = jnp.concatenate([kv_win[v_rows, r * WINDOW:(r + 2) * WINDOW], ones_rows], axis=0)
        acc = jnp.dot(vt, pr, preferred_element_type=F32)
        den = acc[HEAD_DIM:HEAD_DIM + 1, :] + jnp.exp2(sink - m)
        o_t = acc[0:HEAD_DIM, :] * (1.0 / den)
        for pp in range(SWA_GROUP // 2):
            pair = jnp.concatenate([o_t[:, (2 * pp) * WINDOW:(2 * pp + 1) * WINDOW],
                                    o_t[:, (2 * pp + 1) * WINDOW:(2 * pp + 2) * WINDOW]], axis=0)
            c0 = (g * SWA_GROUP + 2 * pp) * HEAD_DIM
            o_ref[r * WINDOW:(r + 1) * WINDOW, c0:c0 + LANES] = pair.T.astype(BF16)

    pending = scores(units[0])
    for n, unit in enumerate(units):
        nxt = scores(units[n + 1]) if n + 1 < len(units) else None
        finish(pending, unit)
        pending = nxt


def _swa(qv_t, kv_t, sinks, batch, seq):
    nsq = seq // TQ_SWA
    per = TQ_SWA // WINDOW
    return pl.pallas_call(
        _swa_kernel,
        grid=(batch, nsq),
        in_specs=[
            pl.BlockSpec(memory_space=pltpu.SMEM),
            pl.BlockSpec((None, SWA_Q_W, TQ_SWA), lambda b, i: (b, ROW_AQ // SWA_Q_W, i)),
            pl.BlockSpec((None, 2 * SWA_KV_W, TQ_SWA), lambda b, i: (b, 0, i)),
            pl.BlockSpec((None, 2 * SWA_KV_W, WINDOW),
                         lambda b, i: (b, 0, jnp.maximum(i * per - 1, 0))),
        ],
        out_specs=pl.BlockSpec((TQ_SWA, SWA_Q_W), lambda b, i: (b * nsq + i, 0)),
        scratch_shapes=[pltpu.VMEM((WINDOW + TQ_SWA, SWA_KV_W), BF16)],
        out_shape=jax.ShapeDtypeStruct((batch * seq, SWA_Q_W), BF16),
        compiler_params=pltpu.CompilerParams(
            dimension_semantics=("arbitrary", "arbitrary")),
        name="swa",
    )(sinks, qv_t, kv_t, kv_t)


def _fox_kernel(base_ref, q_ref, k_ref, aug_ref, vt_ref, o_ref,
                m_sc, acc_sc, qt_sc, bm_sc, *bufs):
    b = pl.program_id(0)
    p = pl.program_id(1)
    qi = pl.program_id(2)
    s_bufs, p_bufs = bufs[:FOX_SLOTS], bufs[FOX_SLOTS:]
    dz = pl.multiple_of(jnp.minimum(qi, 0), FOX_ROWS)
    nb = k_ref.shape[0] // TK_FOX
    per = TQ_FOX // TK_FOX
    q = q_ref[...]
    lane = lax.broadcasted_iota(jnp.int32, (TQ_FOX, LANES), 1)
    low = lane < HEAD_DIM
    zero = jnp.zeros_like(q)
    for hh in range(2):
        h = 2 * p + hh
        sel = (lane == h) | (lane == h + FOX_HEADS) | (lane == h + 2 * FOX_HEADS)
        ones = jnp.where(sel, 1.0, 0.0)
        q_h = jnp.where(low if hh == 0 else ~low, q, zero).astype(F32)
        q_full = jnp.concatenate([q_h, ones], axis=1)
        qt_sc[hh] = q_full.T.astype(BF16)

    m_sc[...] = jnp.full(m_sc.shape, NEG, F32)
    acc_sc[...] = jnp.zeros(acc_sc.shape, F32)
    ones_rows = jnp.ones((FOX_DEN_ROWS, TK_FOX), BF16)

    def scores(blk, n, q_lo, diag):
        hh, slot, w = n % 2, n % FOX_SLOTS, TQ_FOX - q_lo
        ks = pl.ds(pl.multiple_of(blk * TQ_FOX + (n // 2) * TK_FOX, TK_FOX), TK_FOX)
        k_full = jnp.concatenate([k_ref[ks, :], aug_ref[ks, :]], axis=1)
        s = jnp.dot(k_full, qt_sc[hh, :, q_lo:], preferred_element_type=F32)
        if diag is not None:
            row = lax.broadcasted_iota(jnp.int32, s.shape, 0)
            col = lax.broadcasted_iota(jnp.int32, s.shape, 1)
            limit = 0 if diag is True else jnp.where(diag, 0, TK_FOX)
            s = jnp.where(row - col <= limit, s, NEG)
        s_bufs[slot][pl.ds(dz, TK_FOX), 0:w] = s
        bm_sc[slot, :, 0:w] = jnp.max(s, axis=0, keepdims=True)

    def finish(blk, n, q_lo):
        hh, slot, p_slot, w = n % 2, n % FOX_SLOTS, n % 2, TQ_FOX - q_lo
        ks = pl.ds(pl.multiple_of(blk * TQ_FOX + (n // 2) * TK_FOX, TK_FOX), TK_FOX)
        base = base_ref[(b * nb + blk * per + n // 2) * FOX_HEADS + 2 * p + hh]
        m_prev = m_sc[hh, :, q_lo:] - base
        m_new = jnp.maximum(m_prev, bm_sc[slot, :, 0:w])
        alpha = jnp.exp2(m_prev - m_new)
        for r in range(0, TK_FOX, FOX_ROWS):
            rows = pl.ds(dz + r, FOX_ROWS)
            pr = jnp.exp2(s_bufs[slot][rows, 0:w] - m_new)
            p_bufs[p_slot][rows, 0:w] = pr.astype(BF16)
        vt_h = jnp.concatenate([vt_ref[hh * HEAD_DIM:(hh + 1) * HEAD_DIM, ks], ones_rows], axis=0)
        acc_sc[hh, :, q_lo:] = alpha * acc_sc[hh, :, q_lo:] + jnp.dot(
            vt_h, p_bufs[p_slot][pl.ds(dz, TK_FOX), 0:w], preferred_element_type=F32)
        m_sc[hh, :, q_lo:] = m_new + base

    for n in range(FOX_AHEAD):
        scores(0, n, 0, qi == 0)

    def body(j, carry):
        for n in range(FOX_CHAINS):
            m = n + FOX_AHEAD
            if m < FOX_CHAINS:
                scores(j, m, 0, None)
            else:
                scores(j + 1, m - FOX_CHAINS, 0, j + 1 == qi)
            finish(j, n, 0)
        return carry

    lax.fori_loop(0, qi, body, 0)
    for n in range(FOX_CHAINS):
        m = n + FOX_AHEAD
        if m < FOX_CHAINS:
            scores(qi, m, (m // 2) * TK_FOX, True)
        finish(qi, n, (n // 2) * TK_FOX)

    o_t = jnp.concatenate(
        [acc_sc[hh, 0:HEAD_DIM, :] * (1.0 / acc_sc[hh, HEAD_DIM:HEAD_DIM + 1, :]) for hh in range(2)],
        axis=0)
    o_ref[...] = o_t.T.astype(BF16)


def _fox(proj, aug, base, qv_t, batch, seq):
    t = proj.shape[0]
    nq = seq // TQ_FOX
    pairs = FOX_HEADS // 2
    return pl.pallas_call(
        _fox_kernel,
        grid=(batch, pairs, nq),
        in_specs=[
            pl.BlockSpec(memory_space=pltpu.SMEM),
            pl.BlockSpec((TQ_FOX, LANES), lambda b, p, i: (b * nq + i, COL_FQ // LANES + p)),
            pl.BlockSpec((seq, LANES), lambda b, p, i: (b, COL_FK // LANES + p)),
            pl.BlockSpec((seq, LANES), lambda b, p, i: (b, 0)),
            pl.BlockSpec((None, LANES, seq), lambda b, p, i: (b, ROW_FV // LANES + p, 0)),
        ],
        out_specs=pl.BlockSpec((TQ_FOX, LANES), lambda b, p, i: (b * nq + i, p)),
        out_shape=jax.ShapeDtypeStruct((t, FOX_W), BF16),
        scratch_shapes=[
            pltpu.VMEM((2, 1, TQ_FOX), F32),
            pltpu.VMEM((2, HEAD_DIM + FOX_DEN_ROWS, TQ_FOX), F32),
            pltpu.VMEM((2, 2 * LANES, TQ_FOX), BF16),
            pltpu.VMEM((FOX_SLOTS, 1, TQ_FOX), F32),
        ] + [pltpu.VMEM((TK_FOX, TQ_FOX), F32)] * FOX_SLOTS + [
            pltpu.VMEM((TK_FOX, TQ_FOX), BF16),
            pltpu.VMEM((TK_FOX, TQ_FOX), BF16),
        ],
        compiler_params=pltpu.CompilerParams(
            dimension_semantics=("arbitrary", "arbitrary", "arbitrary")),
        name="fox",
    )(base, proj, proj, aug, qv_t)


def _merge_kernel(oa_ref, ob_ref, ga_ref, gb_ref, x_ref, wa_ref, wb_ref, wo_ref, gain_ref,
                  x1_ref, h2_ref):
    ya = jnp.dot(oa_ref[...], wa_ref[...], preferred_element_type=F32)
    yb = jnp.dot(ob_ref[...], wb_ref[...], preferred_element_type=F32)

    def gate(ref):
        return 1.0 / (1.0 + jnp.exp(-ref[...].astype(F32)))

    merged = gate(ga_ref) * ya + gate(gb_ref) * yb
    x1 = x_ref[...] + jnp.dot(merged.astype(BF16), wo_ref[...], preferred_element_type=F32)
    x1_ref[...] = x1
    h2_ref[...] = _rmsnorm_rows(x1, gain_ref[...]).astype(BF16)


def _merge(o_a, o_b, proj, x2, wa, wb, wo, gain):
    t = x2.shape[0]
    tm = TM_MERGE
    const = dict(pipeline_mode=pl.Buffered(1))
    return pl.pallas_call(
        _merge_kernel,
        grid=(t // tm,),
        in_specs=[
            pl.BlockSpec((tm, SWA_Q_W), lambda i: (i, 0)),
            pl.BlockSpec((tm, FOX_W), lambda i: (i, 0)),
            pl.BlockSpec((tm, D_MODEL), lambda i: (i, COL_GA // D_MODEL)),
            pl.BlockSpec((tm, D_MODEL), lambda i: (i, COL_GB // D_MODEL)),
            pl.BlockSpec((tm, D_MODEL), lambda i: (i, 0)),
            pl.BlockSpec((SWA_Q_W, D_MODEL), lambda i: (0, 0), **const),
            pl.BlockSpec((FOX_W, D_MODEL), lambda i: (0, 0), **const),
            pl.BlockSpec((D_MODEL, D_MODEL), lambda i: (0, 0), **const),
            pl.BlockSpec((1, D_MODEL), lambda i: (0, 0), **const),
        ],
        out_specs=[
            pl.BlockSpec((tm, D_MODEL), lambda i: (i, 0)),
            pl.BlockSpec((tm, D_MODEL), lambda i: (i, 0)),
        ],
        out_shape=[
            jax.ShapeDtypeStruct((t, D_MODEL), F32),
            jax.ShapeDtypeStruct((t, D_MODEL), BF16),
        ],
        compiler_params=pltpu.CompilerParams(
            dimension_semantics=("arbitrary",),
            vmem_limit_bytes=56 * MIB),
        name="merge",
    )(o_a, o_b, proj, proj, x2, wa, wb, wo, gain)


def _mlp_kernel(h2_ref, x1_ref, wup_ref, wdn_ref, gain_ref, o_ref, *, final_norm):
    f = pl.program_id(1)

    @pl.when(f == 0)
    def _():
        o_ref[...] = x1_ref[...]

    u = jnp.maximum(jnp.dot(h2_ref[...], wup_ref[...], preferred_element_type=F32), 0.0)
    o_ref[...] += jnp.dot((u * u).astype(BF16), wdn_ref[...], preferred_element_type=F32)

    if final_norm:
        @pl.when(f == pl.num_programs(1) - 1)
        def _():
            o_ref[...] = _rmsnorm_rows(o_ref[...], gain_ref[...])


def _mlp(h2, x1, wup, wdn, gain, final_norm):
    t = x1.shape[0]
    tm, tf = TM_MLP, TF_MLP
    return pl.pallas_call(
        functools.partial(_mlp_kernel, final_norm=final_norm),
        grid=(t // tm, D_FF // tf),
        in_specs=[
            pl.BlockSpec((tm, D_MODEL), lambda i, f: (i, 0)),
            pl.BlockSpec((tm, D_MODEL), lambda i, f: (i, 0)),
            pl.BlockSpec((D_MODEL, tf), lambda i, f: (0, f)),
            pl.BlockSpec((tf, D_MODEL), lambda i, f: (f, 0)),
            pl.BlockSpec((1, D_MODEL), lambda i, f: (0, 0)),
        ],
        out_specs=pl.BlockSpec((tm, D_MODEL), lambda i, f: (i, 0)),
        out_shape=jax.ShapeDtypeStruct((t, D_MODEL), F32),
        compiler_params=pltpu.CompilerParams(
            dimension_semantics=("arbitrary", "arbitrary"),
            vmem_limit_bytes=48 * MIB),
        name="mlp",
    )(h2, x1, wup, wdn, gain)


def _pack_w_in(w, scale):
    o = 0
    a_q = w[:, o:o + SWA_Q_W] * (scale * LOG2E); o += SWA_Q_W
    a_k = w[:, o:o + SWA_KV_W]; o += SWA_KV_W
    a_v = w[:, o:o + SWA_KV_W]; o += SWA_KV_W
    f_q = w[:, o:o + FOX_W] * (scale * LOG2E); o += FOX_W
    f_k = w[:, o:o + FOX_W]; o += FOX_W
    f_v = w[:, o:o + FOX_W]; o += FOX_W
    f_l = w[:, o:o + FOX_HEADS]; o += FOX_HEADS
    g_a = w[:, o:o + D_MODEL]; o += D_MODEL
    g_b = w[:, o:o + D_MODEL]; o += D_MODEL
    w_tok = jnp.concatenate([g_a, g_b, f_q, f_k], axis=1).astype(BF16)
    w_fl = jnp.pad(f_l, ((0, 0), (0, LANES - FOX_HEADS))).astype(BF16)
    w_qv = jnp.concatenate([a_q, f_v], axis=1).T.astype(BF16)
    w_kv = jnp.concatenate([a_k, a_v], axis=1).T.astype(BF16)
    return w_tok, w_fl, w_qv, w_kv


def kernel(x, positions, attn_norm, w_in, fox_f_bias, swa_sinks, w_branch_swa, w_branch_fox,
           w_out, mlp_norm, w_up, w_down, final_norm):
    batch, seq, d = x.shape
    depth = w_in.shape[0]
    assert d == D_MODEL and seq % TM_IN == 0 and seq % TQ_FOX == 0
    t = batch * seq
    scale = HEAD_DIM ** -0.5
    inv_freq = ROPE_THETA ** (-jnp.arange(0, HEAD_DIM, 2, dtype=F32) / HEAD_DIM)
    invf = inv_freq[:, None]
    pos3 = positions.reshape(t // TM_IN, 1, TM_IN)
    x2 = x.reshape(t, d)
    for l in range(depth):
        w_tok, w_fl, w_qv, w_kv = _pack_w_in(w_in[l], scale)
        fbias = jnp.pad(fox_f_bias[l].astype(F32), (0, LANES - FOX_HEADS))[None, :]
        proj, logf, qv_t, kv_t = _inproj(x2, pos3, invf, attn_norm[l][None, :].astype(F32),
                                         w_tok, w_fl, w_qv, w_kv, fbias, batch, seq)
        aug, base = _decay(logf, batch, seq)
        base = base[:, :, 0, :FOX_HEADS].reshape(-1)
        o_a = _swa(qv_t, kv_t, swa_sinks[l].astype(F32), batch, seq)
        o_b = _fox(proj, aug, base, qv_t, batch, seq)
        x1, h2 = _merge(o_a, o_b, proj, x2,
                        w_branch_swa[l].astype(BF16), w_branch_fox[l].astype(BF16),
                        w_out[l].astype(BF16), mlp_norm[l][None, :].astype(F32))
        x2 = _mlp(h2, x1, w_up[l].astype(BF16), w_down[l].astype(BF16),
                  final_norm[None, :].astype(F32), final_norm=(l == depth - 1))
    return x2.reshape(batch, seq, d)
```

```python
import functools

import jax
import jax.numpy as jnp
from jax import lax
from jax.experimental import pallas as pl
from jax.experimental.pallas import tpu as pltpu

F32 = jnp.float32
BF16 = jnp.bfloat16

D_MODEL = 2048
HEAD_DIM = 64
HALF = HEAD_DIM // 2
SWA_Q_HEADS = 16
SWA_KV_HEADS = 2
SWA_GROUP = SWA_Q_HEADS // SWA_KV_HEADS
WINDOW = 128
FOX_HEADS = 16
D_FF = 4 * D_MODEL
ROPE_THETA = 10000.0
RMS_EPS = 1e-6
SWA_Q_W = SWA_Q_HEADS * HEAD_DIM
FOX_W = FOX_HEADS * HEAD_DIM

LANES = 128
NEG = -1e30
MIB = 1024 * 1024
LOG2E = 1.4426950408889634

COL_GA = 0
COL_GB = 2048
COL_FK = 4096
D_PROJ = 5120
ROW_AQ = 0
ROW_FV = SWA_Q_W
ROW_FQ = SWA_Q_W + FOX_W
D_QV = ROW_FQ + FOX_W
SWA_KV_W = SWA_KV_HEADS * HEAD_DIM

TM_IN = 1024
TN_IN = 512
NJ_TOK = D_PROJ // TN_IN
J_AQ = NJ_TOK
J_KV = J_AQ + SWA_Q_W // TN_IN
J_FEAT = J_KV + 1
NJ_IN = J_FEAT + (D_QV - ROW_FV) // TN_IN

TQ_SWA = 512
TQ_FOX = 1024
TK_FOX = 256
FOX_AHEAD = 2
FOX_CHAINS = 2 * (TQ_FOX // TK_FOX)
FOX_SLOTS = 4
assert FOX_CHAINS % FOX_SLOTS == 0 and FOX_AHEAD < FOX_SLOTS
FOX_ROWS = 16
FOX_DEN_ROWS = 16
DECAY_BLOCKS = 4
TM_MERGE = 512
TM_MLP = 1024
TF_MLP = 512


def _rmsnorm_rows(x, gain):
    ms = jnp.mean(x * x, axis=-1, keepdims=True)
    return x * lax.rsqrt(ms + RMS_EPS) * gain


def _split3(x):
    hi = x.astype(BF16)
    r1 = x - hi.astype(F32)
    mid = r1.astype(BF16)
    lo = (r1 - mid.astype(F32)).astype(BF16)
    return hi, mid, lo


def _inproj_kernel(pos_ref, invf_ref, x_ref, gain_ref, w_ref, wqv_ref, wkv_ref, fbias_ref,
                   proj_ref, logf_ref, qv_ref, kv_ref, h_sc, cos_sc, sin_sc):
    j = pl.program_id(1)

    @pl.when(j == 0)
    def _():
        def body(r, _):
            rows = pl.ds(pl.multiple_of(r * LANES, LANES), LANES)
            h_sc[rows, :] = _rmsnorm_rows(x_ref[rows, :], gain_ref[...]).astype(BF16)
            return 0
        lax.fori_loop(0, TM_IN // LANES, body, 0)
        ang = invf_ref[...] * pos_ref[...].astype(F32)
        cos_sc[...] = jnp.cos(ang)
        sin_sc[...] = jnp.sin(ang)

    def features(wt_ref):
        return lax.dot_general(wt_ref[...], h_sc[...], (((1,), (1,)), ((), ())),
                               preferred_element_type=F32)

    def rope_rows(acc, heads, out_ref):
        for hd in range(heads):
            r0 = hd * HEAD_DIM
            t1 = acc[r0:r0 + HALF, :]
            t2 = acc[r0 + HALF:r0 + HEAD_DIM, :]
            out_ref[r0:r0 + HALF, :] = (t1 * cos_sc[...] - t2 * sin_sc[...]).astype(BF16)
            out_ref[r0 + HALF:r0 + HEAD_DIM, :] = (t2 * cos_sc[...] + t1 * sin_sc[...]).astype(BF16)

    @pl.when(j < NJ_TOK)
    def _():
        proj_ref[...] = jnp.dot(h_sc[...], w_ref[...], preferred_element_type=F32).astype(BF16)

    @pl.when((j >= J_AQ) & (j < J_KV))
    def _():
        rope_rows(features(wqv_ref), TN_IN // HEAD_DIM, qv_ref)

    @pl.when(j == J_KV)
    def _():
        acc = features(wkv_ref)
        rope_rows(acc, SWA_KV_HEADS, kv_ref)
        kv_ref[SWA_KV_W:2 * SWA_KV_W, :] = acc[SWA_KV_W:2 * SWA_KV_W, :].astype(BF16)
        z = acc[2 * SWA_KV_W:, :] + fbias_ref[...]
        logf_ref[...] = jnp.minimum(z, 0.0) - jnp.log1p(jnp.exp(-jnp.abs(z)))

    @pl.when(j >= J_FEAT)
    def _():
        qv_ref[...] = features(wqv_ref).astype(BF16)


def _inproj(x2, pos3, invf, gain, w_tok, w_qv, w_kv, fbias, batch, seq):
    t = x2.shape[0]
    nsb = seq // TM_IN
    kv_rows = 2 * SWA_KV_W

    def qv_blk(j):
        return jnp.where(j >= J_FEAT, j - J_FEAT + ROW_FV // TN_IN,
                         jnp.clip(j - J_AQ, 0, SWA_Q_W // TN_IN - 1))

    def tok_blk(j):
        return jnp.minimum(j, NJ_TOK - 1)

    return pl.pallas_call(
        _inproj_kernel,
        grid=(t // TM_IN, NJ_IN),
        in_specs=[
            pl.BlockSpec((None, 1, TM_IN), lambda i, j: (i, 0, 0)),
            pl.BlockSpec((HALF, 1), lambda i, j: (0, 0)),
            pl.BlockSpec((TM_IN, D_MODEL), lambda i, j: (i, 0)),
            pl.BlockSpec((1, D_MODEL), lambda i, j: (0, 0)),
            pl.BlockSpec((D_MODEL, TN_IN), lambda i, j: (0, tok_blk(j))),
            pl.BlockSpec((TN_IN, D_MODEL), lambda i, j: (qv_blk(j), 0)),
            pl.BlockSpec((kv_rows + FOX_HEADS, D_MODEL), lambda i, j: (0, 0)),
            pl.BlockSpec((FOX_HEADS, 1), lambda i, j: (0, 0)),
        ],
        out_specs=[
            pl.BlockSpec((TM_IN, TN_IN), lambda i, j: (i, tok_blk(j))),
            pl.BlockSpec((None, FOX_HEADS, TM_IN), lambda i, j: (i // nsb, 0, i % nsb)),
            pl.BlockSpec((None, TN_IN, TM_IN), lambda i, j: (i // nsb, qv_blk(j), i % nsb)),
            pl.BlockSpec((None, kv_rows, TM_IN), lambda i, j: (i // nsb, 0, i % nsb)),
        ],
        out_shape=[
            jax.ShapeDtypeStruct((t, D_PROJ), BF16),
            jax.ShapeDtypeStruct((batch, FOX_HEADS, seq), F32),
            jax.ShapeDtypeStruct((batch, D_QV, seq), BF16),
            jax.ShapeDtypeStruct((batch, kv_rows, seq), BF16),
        ],
        scratch_shapes=[
            pltpu.VMEM((TM_IN, D_MODEL), BF16),
            pltpu.VMEM((HALF, TM_IN), F32),
            pltpu.VMEM((HALF, TM_IN), F32),
        ],
        compiler_params=pltpu.CompilerParams(
            dimension_semantics=("arbitrary", "arbitrary"),
            vmem_limit_bytes=52 * MIB),
        name="inproj",
    )(pos3, invf, x2, gain, w_tok, w_qv, w_kv, fbias)


def _decay_kernel(logf_ref, aug_ref, base_ref, carry_sc):
    sb = pl.program_id(1)

    @pl.when(sb == 0)
    def _():
        carry_sc[...] = jnp.zeros(carry_sc.shape, F32)

    row = lax.broadcasted_iota(jnp.int32, (TK_FOX, TK_FOX), 0)
    col = lax.broadcasted_iota(jnp.int32, (TK_FOX, TK_FOX), 1)
    upper = jnp.where(row <= col, 1.0, 0.0).astype(BF16)
    pad = jnp.zeros((LANES - 3 * FOX_HEADS, TK_FOX), F32)
    for c in range(DECAY_BLOCKS):
        toks = slice(c * TK_FOX, (c + 1) * TK_FOX)
        hi, mid, lo = _split3(logf_ref[:, toks] * LOG2E)
        cl = (jnp.dot(hi, upper, preferred_element_type=F32)
              + jnp.dot(mid, upper, preferred_element_type=F32)
              + jnp.dot(lo, upper, preferred_element_type=F32))
        yh, ym, yl = _split3(-cl)
        pieces_t = jnp.concatenate([yh.astype(F32), ym.astype(F32), yl.astype(F32), pad], axis=0)
        aug_ref[toks, :] = pieces_t.T.astype(BF16)
        base_ref[:, c:c + 1] = -carry_sc[...]
        carry_sc[...] = carry_sc[...] + cl[:, TK_FOX - 1:TK_FOX]


def _decay(logf_t, batch, seq):
    toks = DECAY_BLOCKS * TK_FOX
    ns = seq // toks
    return pl.pallas_call(
        _decay_kernel,
        grid=(batch, ns),
        in_specs=[pl.BlockSpec((None, FOX_HEADS, toks), lambda b, s: (b, 0, s))],
        out_specs=[
            pl.BlockSpec((toks, LANES), lambda b, s: (b * ns + s, 0)),
            pl.BlockSpec((None, None, FOX_HEADS, DECAY_BLOCKS), lambda b, s: (b, s, 0, 0)),
        ],
        out_shape=[
            jax.ShapeDtypeStruct((batch * seq, LANES), BF16),
            jax.ShapeDtypeStruct((batch, ns, FOX_HEADS, DECAY_BLOCKS), F32),
        ],
        scratch_shapes=[pltpu.VMEM((FOX_HEADS, 1), F32)],
        compiler_params=pltpu.CompilerParams(
            dimension_semantics=("arbitrary", "arbitrary")),
        name="decay",
    )(logf_t)


def _swa_kernel(sinks_ref, qt_ref, cur_ref, prev_ref, o_ref, ktok_sc):
    i = pl.program_id(1)
    nblk = TQ_SWA // WINDOW
    gw = SWA_GROUP * WINDOW
    kv_win = jnp.concatenate([prev_ref[...], cur_ref[...]], axis=1)
    ktok_sc[...] = kv_win[0:SWA_KV_W, :].astype(F32).T.astype(BF16)

    row = lax.broadcasted_iota(jnp.int32, (2 * WINDOW, gw), 0)
    qix = lax.broadcasted_iota(jnp.int32, (2 * WINDOW, gw), 1) % WINDOW
    allowed = (row > qix) & (row <= qix + WINDOW)
    bias_any = jnp.where(allowed, 0.0, NEG)
    bias_first = jnp.where(allowed & (row >= WINDOW), 0.0, NEG)
    bias_r0 = jnp.where(i == 0, bias_first, bias_any)
    zeros_q = jnp.zeros((HEAD_DIM, WINDOW), BF16)
    ones_rows = jnp.ones((FOX_DEN_ROWS, 2 * WINDOW), BF16)

    units = [(r, g) for r in range(nblk) for g in range(SWA_KV_HEADS)]

    def scores(unit):
        r, g = unit
        cols = slice(r * WINDOW, (r + 1) * WINDOW)
        tiles = []
        for hq in range(SWA_GROUP):
            h = g * SWA_GROUP + hq
            q_h = qt_ref[h * HEAD_DIM:(h + 1) * HEAD_DIM, cols]
            tiles.append(jnp.concatenate([q_h, zeros_q] if g == 0 else [zeros_q, q_h], axis=0))
        q_g = jnp.concatenate(tiles, axis=1)
        k_win = ktok_sc[r * WINDOW:(r + 2) * WINDOW, :]
        s = jnp.dot(k_win, q_g, preferred_element_type=F32)
        return s + (bias_r0 if r == 0 else bias_any)

    def finish(s, unit):
        r, g = unit
        sink = jnp.concatenate(
            [jnp.full((1, WINDOW), sinks_ref[g * SWA_GROUP + hq] * LOG2E, F32)
             for hq in range(SWA_GROUP)], axis=1)
        m = jnp.maximum(jnp.max(s, axis=0, keepdims=True), sink)
        pr = jnp.exp2(s - m).astype(BF16)
        v_rows = slice(SWA_KV_W + g * HEAD_DIM, SWA_KV_W + (g + 1) * HEAD_DIM)
        vt = jnp.concatenate([kv_win[v_rows, r * WINDOW:(r + 2) * WINDOW], ones_rows], axis=0)
        acc = jnp.dot(vt, pr, preferred_element_type=F32)
        den = acc[HEAD_DIM:HEAD_DIM + 1, :] + jnp.exp2(sink - m)
        o_t = acc[0:HEAD_DIM, :] * (1.0 / den)
        for pp in range(SWA_GROUP // 2):
            pair = jnp.concatenate([o_t[:, (2 * pp) * WINDOW:(2 * pp + 1) * WINDOW],
                                    o_t[:, (2 * pp + 1) * WINDOW:(2 * pp + 2) * WINDOW]], axis=0)
            c0 = (g * SWA_GROUP + 2 * pp) * HEAD_DIM
            o_ref[r * WINDOW:(r + 1) * WINDOW, c0:c0 + LANES] = pair.T.astype(BF16)

    pending = scores(units[0])
    for n, unit in enumerate(units):
        nxt = scores(units[n + 1]) if n + 1 < len(units) else None
        finish(pending, unit)
        pending = nxt


def _swa(qv_t, kv_t, sinks, batch, seq):
    nsq = seq // TQ_SWA
    per = TQ_SWA // WINDOW
    return pl.pallas_call(
        _swa_kernel,
        grid=(batch, nsq),
        in_specs=[
            pl.BlockSpec(memory_space=pltpu.SMEM),
            pl.BlockSpec((None, SWA_Q_W, TQ_SWA), lambda b, i: (b, ROW_AQ // SWA_Q_W, i)),
            pl.BlockSpec((None, 2 * SWA_KV_W, TQ_SWA), lambda b, i: (b, 0, i)),
            pl.BlockSpec((None, 2 * SWA_KV_W, WINDOW),
                         lambda b, i: (b, 0, jnp.maximum(i * per - 1, 0))),
        ],
        out_specs=pl.BlockSpec((TQ_SWA, SWA_Q_W), lambda b, i: (b * nsq + i, 0)),
        scratch_shapes=[pltpu.VMEM((WINDOW + TQ_SWA, SWA_KV_W), BF16)],
        out_shape=jax.ShapeDtypeStruct((batch * seq, SWA_Q_W), BF16),
        compiler_params=pltpu.CompilerParams(
            dimension_semantics=("arbitrary", "arbitrary")),
        name="swa",
    )(sinks, qv_t, kv_t, kv_t)


def _fox_kernel(base_ref, q_ref, k_ref, aug_ref, vt_ref, o_ref,
                m_sc, acc_sc, qt_sc, bm_sc, *bufs):
    b = pl.program_id(0)
    p = pl.program_id(1)
    qi = pl.program_id(2)
    s_bufs, p_bufs = bufs[:FOX_SLOTS], bufs[FOX_SLOTS:]
    dz = pl.multiple_of(jnp.minimum(qi, 0), FOX_ROWS)
    nb = k_ref.shape[0] // TK_FOX
    per = TQ_FOX // TK_FOX
    q_t = q_ref[...]
    feat = lax.broadcasted_iota(jnp.int32, (LANES, TQ_FOX), 0)
    zero = jnp.zeros_like(q_t)
    for hh in range(2):
        h = 2 * p + hh
        sel = (feat == h) | (feat == h + FOX_HEADS) | (feat == h + 2 * FOX_HEADS)
        qt_sc[hh, 0:LANES, :] = jnp.where((feat // HEAD_DIM) == hh, q_t, zero)
        qt_sc[hh, LANES:2 * LANES, :] = jnp.where(sel, 1.0, 0.0).astype(BF16)

    m_sc[...] = jnp.full(m_sc.shape, NEG, F32)
    acc_sc[...] = jnp.zeros(acc_sc.shape, F32)
    ones_rows = jnp.ones((FOX_DEN_ROWS, TK_FOX), BF16)

    def scores(blk, n, q_lo, diag):
        hh, slot, w = n % 2, n % FOX_SLOTS, TQ_FOX - q_lo
        ks = pl.ds(pl.multiple_of(blk * TQ_FOX + (n // 2) * TK_FOX, TK_FOX), TK_FOX)
        k_full = jnp.concatenate([k_ref[ks, :], aug_ref[ks, :]], axis=1)
        s = jnp.dot(k_full, qt_sc[hh, :, q_lo:], preferred_element_type=F32)
        if diag is not None:
            row = lax.broadcasted_iota(jnp.int32, s.shape, 0)
            col = lax.broadcasted_iota(jnp.int32, s.shape, 1)
            limit = 0 if diag is True else jnp.where(diag, 0, TK_FOX)
            s = jnp.where(row - col <= limit, s, NEG)
        s_bufs[slot][pl.ds(dz, TK_FOX), 0:w] = s
        bm_sc[slot, :, 0:w] = jnp.max(s, axis=0, keepdims=True)

    def finish(blk, n, q_lo):
        hh, slot, p_slot, w = n % 2, n % FOX_SLOTS, n % 2, TQ_FOX - q_lo
        ks = pl.ds(pl.multiple_of(blk * TQ_FOX + (n // 2) * TK_FOX, TK_FOX), TK_FOX)
        base = base_ref[(b * nb + blk * per + n // 2) * FOX_HEADS + 2 * p + hh]
        m_prev = m_sc[hh, :, q_lo:] - base
        m_new = jnp.maximum(m_prev, bm_sc[slot, :, 0:w])
        alpha = jnp.exp2(m_prev - m_new)
        for r in range(0, TK_FOX, FOX_ROWS):
            rows = pl.ds(dz + r, FOX_ROWS)
            pr = jnp.exp2(s_bufs[slot][rows, 0:w] - m_new)
            p_bufs[p_slot][rows, 0:w] = pr.astype(BF16)
        vt_h = jnp.concatenate([vt_ref[hh * HEAD_DIM:(hh + 1) * HEAD_DIM, ks], ones_rows], axis=0)
        acc_sc[hh, :, q_lo:] = alpha * acc_sc[hh, :, q_lo:] + jnp.dot(
            vt_h, p_bufs[p_slot][pl.ds(dz, TK_FOX), 0:w], preferred_element_type=F32)
        m_sc[hh, :, q_lo:] = m_new + base

    for n in range(FOX_AHEAD):
        scores(0, n, 0, qi == 0)

    def body(j, carry):
        for n in range(FOX_CHAINS):
            m = n + FOX_AHEAD
            if m < FOX_CHAINS:
                scores(j, m, 0, None)
            else:
                scores(j + 1, m - FOX_CHAINS, 0, j + 1 == qi)
            finish(j, n, 0)
        return carry

    lax.fori_loop(0, qi, body, 0)
    for n in range(FOX_CHAINS):
        m = n + FOX_AHEAD
        if m < FOX_CHAINS:
            scores(qi, m, (m // 2) * TK_FOX, True)
        finish(qi, n, (n // 2) * TK_FOX)

    o_t = jnp.concatenate(
        [acc_sc[hh, 0:HEAD_DIM, :] * (1.0 / acc_sc[hh, HEAD_DIM:HEAD_DIM + 1, :]) for hh in range(2)],
        axis=0)
    o_ref[...] = o_t.T.astype(BF16)


def _fox(proj, aug, base, qv_t, batch, seq):
    t = proj.shape[0]
    nq = seq // TQ_FOX
    pairs = FOX_HEADS // 2
    return pl.pallas_call(
        _fox_kernel,
        grid=(batch, pairs, nq),
        in_specs=[
            pl.BlockSpec(memory_space=pltpu.SMEM),
            pl.BlockSpec((None, LANES, TQ_FOX), lambda b, p, i: (b, ROW_FQ // LANES + p, i)),
            pl.BlockSpec((seq, LANES), lambda b, p, i: (b, COL_FK // LANES + p)),
            pl.BlockSpec((seq, LANES), lambda b, p, i: (b, 0)),
            pl.BlockSpec((None, LANES, seq), lambda b, p, i: (b, ROW_FV // LANES + p, 0)),
        ],
        out_specs=pl.BlockSpec((TQ_FOX, LANES), lambda b, p, i: (b * nq + i, p)),
        out_shape=jax.ShapeDtypeStruct((t, FOX_W), BF16),
        scratch_shapes=[
            pltpu.VMEM((2, 1, TQ_FOX), F32),
            pltpu.VMEM((2, HEAD_DIM + FOX_DEN_ROWS, TQ_FOX), F32),
            pltpu.VMEM((2, 2 * LANES, TQ_FOX), BF16),
            pltpu.VMEM((FOX_SLOTS, 1, TQ_FOX), F32),
        ] + [pltpu.VMEM((TK_FOX, TQ_FOX), F32)] * FOX_SLOTS + [
            pltpu.VMEM((TK_FOX, TQ_FOX), BF16),
            pltpu.VMEM((TK_FOX, TQ_FOX), BF16),
        ],
        compiler_params=pltpu.CompilerParams(
            dimension_semantics=("arbitrary", "arbitrary", "arbitrary")),
        name="fox",
    )(base, qv_t, proj, aug, qv_t)


def _merge_kernel(oa_ref, ob_ref, ga_ref, gb_ref, x_ref, wa_ref, wb_ref, wo_ref, gain_ref,
                  x1_ref, h2_ref):
    ya = jnp.dot(oa_ref[...], wa_ref[...], preferred_element_type=F32)
    yb = jnp.dot(ob_ref[...], wb_ref[...], preferred_element_type=F32)

    def gate(ref):
        return 1.0 / (1.0 + jnp.exp(-ref[...].astype(F32)))

    merged = gate(ga_ref) * ya + gate(gb_ref) * yb
    x1 = x_ref[...] + jnp.dot(merged.astype(BF16), wo_ref[...], preferred_element_type=F32)
    x1_ref[...] = x1
    h2_ref[...] = _rmsnorm_rows(x1, gain_ref[...]).astype(BF16)


def _merge(o_a, o_b, proj, x2, wa, wb, wo, gain):
    t = x2.shape[0]
    tm = TM_MERGE
    const = dict(pipeline_mode=pl.Buffered(1))
    return pl.pallas_call(
        _merge_kernel,
        grid=(t // tm,),
        in_specs=[
            pl.BlockSpec((tm, SWA_Q_W), lambda i: (i, 0)),
            pl.BlockSpec((tm, FOX_W), lambda i: (i, 0)),
            pl.BlockSpec((tm, D_MODEL), lambda i: (i, COL_GA // D_MODEL)),
            pl.BlockSpec((tm, D_MODEL), lambda i: (i, COL_GB // D_MODEL)),
            pl.BlockSpec((tm, D_MODEL), lambda i: (i, 0)),
            pl.BlockSpec((SWA_Q_W, D_MODEL), lambda i: (0, 0), **const),
            pl.BlockSpec((FOX_W, D_MODEL), lambda i: (0, 0), **const),
            pl.BlockSpec((D_MODEL, D_MODEL), lambda i: (0, 0), **const),
            pl.BlockSpec((1, D_MODEL), lambda i: (0, 0), **const),
        ],
        out_specs=[
            pl.BlockSpec((tm, D_MODEL), lambda i: (i, 0)),
            pl.BlockSpec((tm, D_MODEL), lambda i: (i, 0)),
        ],
        out_shape=[
            jax.ShapeDtypeStruct((t, D_MODEL), F32),
            jax.ShapeDtypeStruct((t, D_MODEL), BF16),
        ],
        compiler_params=pltpu.CompilerParams(
            dimension_semantics=("arbitrary",),
            vmem_limit_bytes=56 * MIB),
        name="merge",
    )(o_a, o_b, proj, proj, x2, wa, wb, wo, gain)


def _mlp_kernel(h2_ref, x1_ref, wup_ref, wdn_ref, gain_ref, o_ref, *, final_norm):
    f = pl.program_id(1)

    @pl.when(f == 0)
    def _():
        o_ref[...] = x1_ref[...]

    u = jnp.maximum(jnp.dot(h2_ref[...], wup_ref[...], preferred_element_type=F32), 0.0)
    o_ref[...] += jnp.dot((u * u).astype(BF16), wdn_ref[...], preferred_element_type=F32)

    if final_norm:
        @pl.when(f == pl.num_programs(1) - 1)
        def _():
            o_ref[...] = _rmsnorm_rows(o_ref[...], gain_ref[...])


def _mlp(h2, x1, wup, wdn, gain, final_norm):
    t = x1.shape[0]
    tm, tf = TM_MLP, TF_MLP
    return pl.pallas_call(
        functools.partial(_mlp_kernel, final_norm=final_norm),
        grid=(t // tm, D_FF // tf),
        in_specs=[
            pl.BlockSpec((tm, D_MODEL), lambda i, f: (i, 0)),
            pl.BlockSpec((tm, D_MODEL), lambda i, f: (i, 0)),
            pl.BlockSpec((D_MODEL, tf), lambda i, f: (0, f)),
            pl.BlockSpec((tf, D_MODEL), lambda i, f: (f, 0)),
            pl.BlockSpec((1, D_MODEL), lambda i, f: (0, 0)),
        ],
        out_specs=pl.BlockSpec((tm, D_MODEL), lambda i, f: (i, 0)),
        out_shape=jax.ShapeDtypeStruct((t, D_MODEL), F32),
        compiler_params=pltpu.CompilerParams(
            dimension_semantics=("arbitrary", "arbitrary"),
            vmem_limit_bytes=58 * MIB),
        name="mlp",
    )(h2, x1, wup, wdn, gain)


def _pack_w_in(w, scale):
    o = 0
    a_q = w[:, o:o + SWA_Q_W] * (scale * LOG2E); o += SWA_Q_W
    a_k = w[:, o:o + SWA_KV_W]; o += SWA_KV_W
    a_v = w[:, o:o + SWA_KV_W]; o += SWA_KV_W
    f_q = w[:, o:o + FOX_W] * (scale * LOG2E); o += FOX_W
    f_k = w[:, o:o + FOX_W]; o += FOX_W
    f_v = w[:, o:o + FOX_W]; o += FOX_W
    f_l = w[:, o:o + FOX_HEADS]; o += FOX_HEADS
    g_a = w[:, o:o + D_MODEL]; o += D_MODEL
    g_b = w[:, o:o + D_MODEL]; o += D_MODEL
    w_tok = jnp.concatenate([g_a, g_b, f_k], axis=1).astype(BF16)
    w_qv = jnp.concatenate([a_q, f_v, f_q], axis=1).T.astype(BF16)
    w_kv = jnp.concatenate([a_k, a_v, f_l], axis=1).T.astype(BF16)
    return w_tok, w_qv, w_kv


def kernel(x, positions, attn_norm, w_in, fox_f_bias, swa_sinks, w_branch_swa, w_branch_fox,
           w_out, mlp_norm, w_up, w_down, final_norm):
    batch, seq, d = x.shape
    depth = w_in.shape[0]
    assert d == D_MODEL and seq % TM_IN == 0 and seq % TQ_FOX == 0
    t = batch * seq
    scale = HEAD_DIM ** -0.5
    inv_freq = ROPE_THETA ** (-jnp.arange(0, HEAD_DIM, 2, dtype=F32) / HEAD_DIM)
    invf = inv_freq[:, None]
    pos3 = positions.reshape(t // TM_IN, 1, TM_IN)
    x2 = x.reshape(t, d)
    for l in range(depth):
        w_tok, w_qv, w_kv = _pack_w_in(w_in[l], scale)
        fbias = fox_f_bias[l].astype(F32)[:, None]
        proj, logf_t, qv_t, kv_t = _inproj(x2, pos3, invf, attn_norm[l][None, :].astype(F32),
                                           w_tok, w_qv, w_kv, fbias, batch, seq)
        aug, base = _decay(logf_t, batch, seq)
        base = base.transpose(0, 1, 3, 2).reshape(-1)
        o_a = _swa(qv_t, kv_t, swa_sinks[l].astype(F32), batch, seq)
        o_b = _fox(proj, aug, base, qv_t, batch, seq)
        x1, h2 = _merge(o_a, o_b, proj, x2,
                        w_branch_swa[l].astype(BF16), w_branch_fox[l].astype(BF16),
                        w_out[l].astype(BF16), mlp_norm[l][None, :].astype(F32))
        x2 = _mlp(h2, x1, w_up[l].astype(BF16), w_down[l].astype(BF16),
                  final_norm[None, :].astype(F32), final_norm=(l == depth - 1))
    return x2.reshape(batch, seq, d)
```

```python
import functools

import jax
import jax.numpy as jnp
from jax import lax
from jax.experimental import pallas as pl
from jax.experimental.pallas import tpu as pltpu

F32 = jnp.float32
BF16 = jnp.bfloat16

D_MODEL = 2048
HEAD_DIM = 64
HALF = HEAD_DIM // 2
SWA_Q_HEADS = 16
SWA_KV_HEADS = 2
SWA_GROUP = SWA_Q_HEADS // SWA_KV_HEADS
WINDOW = 128
FOX_HEADS = 16
D_FF = 4 * D_MODEL
ROPE_THETA = 10000.0
RMS_EPS = 1e-6
SWA_Q_W = SWA_Q_HEADS * HEAD_DIM
FOX_W = FOX_HEADS * HEAD_DIM

LANES = 128
NEG = -1e30
MIB = 1024 * 1024
LOG2E = 1.4426950408889634

COL_GA = 0
COL_GB = 2048
COL_FK = 4096
D_PROJ = 5120
ROW_AQ = 0
ROW_FV = SWA_Q_W
ROW_FQ = SWA_Q_W + FOX_W
D_QV = ROW_FQ + FOX_W
SWA_KV_W = SWA_KV_HEADS * HEAD_DIM

TM_IN = 1024
TN_IN = 1024
NJ_TOK = D_PROJ // TN_IN
J_AQ = NJ_TOK
J_KV = J_AQ + SWA_Q_W // TN_IN
J_FEAT = J_KV + 1
NJ_IN = J_FEAT + (D_QV - ROW_FV) // TN_IN
assert 2 * (NJ_TOK - 1) >= TM_IN // LANES

TQ_SWA = 512
TQ_FOX = 1024
TK_FOX = 256
FOX_AHEAD = 2
FOX_CHAINS = 2 * (TQ_FOX // TK_FOX)
FOX_SLOTS = 4
assert FOX_CHAINS % FOX_SLOTS == 0 and FOX_AHEAD < FOX_SLOTS
FOX_ROWS = 16
FOX_DEN_ROWS = 16
DECAY_BLOCKS = 4
TM_MERGE = 512
TM_MLP = 512
TF_MLP = 1024


def _rmsnorm_rows(x, gain):
    ms = jnp.mean(x * x, axis=-1, keepdims=True)
    return x * lax.rsqrt(ms + RMS_EPS) * gain


def _split3(x):
    hi = x.astype(BF16)
    r1 = x - hi.astype(F32)
    mid = r1.astype(BF16)
    lo = (r1 - mid.astype(F32)).astype(BF16)
    return hi, mid, lo


def _inproj_kernel(pos_ref, invf_ref, x_ref, gain_ref, w_ref, wqv_ref, wkv_ref, fbias_ref,
                   proj_ref, logf_ref, qv_ref, kv_ref, h_sc, hn_sc, cos_sc, sin_sc):
    i = pl.program_id(0)
    j = pl.program_id(1)
    nchunk = TM_IN // LANES

    def norm_chunk(dst, c):
        rows = pl.ds(pl.multiple_of(c * LANES, LANES), LANES)
        dst[rows, :] = _rmsnorm_rows(x_ref[rows, :], gain_ref[...]).astype(BF16)

    @pl.when(j == 0)
    def _():
        @pl.when(i == 0)
        def _():
            def body(r, _):
                norm_chunk(h_sc, r)
                return 0
            lax.fori_loop(0, nchunk, body, 0)

        @pl.when(i > 0)
        def _():
            h_sc[...] = hn_sc[...]

        ang = invf_ref[...] * pos_ref[...].astype(F32)
        cos_sc[...] = jnp.cos(ang)
        sin_sc[...] = jnp.sin(ang)

    def features(wt_ref):
        return lax.dot_general(wt_ref[...], h_sc[...], (((1,), (1,)), ((), ())),
                               preferred_element_type=F32)

    def rope_rows(acc, heads, out_ref):
        for hd in range(heads):
            r0 = hd * HEAD_DIM
            t1 = acc[r0:r0 + HALF, :]
            t2 = acc[r0 + HALF:r0 + HEAD_DIM, :]
            out_ref[r0:r0 + HALF, :] = (t1 * cos_sc[...] - t2 * sin_sc[...]).astype(BF16)
            out_ref[r0 + HALF:r0 + HEAD_DIM, :] = (t2 * cos_sc[...] + t1 * sin_sc[...]).astype(BF16)

    @pl.when(j < NJ_TOK)
    def _():
        proj_ref[...] = jnp.dot(h_sc[...], w_ref[...], preferred_element_type=F32).astype(BF16)
        c0 = jnp.clip(2 * (j - (i == 0).astype(jnp.int32)), 0, nchunk - 2)
        norm_chunk(hn_sc, c0)
        norm_chunk(hn_sc, c0 + 1)

    @pl.when((j >= J_AQ) & (j < J_KV))
    def _():
        rope_rows(features(wqv_ref), TN_IN // HEAD_DIM, qv_ref)

    @pl.when(j == J_KV)
    def _():
        acc = features(wkv_ref)
        rope_rows(acc, SWA_KV_HEADS, kv_ref)
        kv_ref[SWA_KV_W:2 * SWA_KV_W, :] = acc[SWA_KV_W:2 * SWA_KV_W, :].astype(BF16)
        z = acc[2 * SWA_KV_W:, :] + fbias_ref[...]
        logf_ref[...] = jnp.minimum(z, 0.0) - jnp.log1p(jnp.exp(-jnp.abs(z)))

    @pl.when(j >= J_FEAT)
    def _():
        qv_ref[...] = features(wqv_ref).astype(BF16)


def _inproj(x2, pos3, invf, gain, w_tok, w_qv, w_kv, fbias, batch, seq):
    t = x2.shape[0]
    nsb = seq // TM_IN
    kv_rows = 2 * SWA_KV_W

    def qv_blk(j):
        return jnp.where(j >= J_FEAT, j - J_FEAT + ROW_FV // TN_IN,
                         jnp.clip(j - J_AQ, 0, SWA_Q_W // TN_IN - 1))

    def tok_blk(j):
        return jnp.minimum(j, NJ_TOK - 1)

    def x_blk(i, j):
        return jnp.where((i == 0) & (j == 0), 0, jnp.minimum(i + 1, t // TM_IN - 1))

    return pl.pallas_call(
        _inproj_kernel,
        grid=(t // TM_IN, NJ_IN),
        in_specs=[
            pl.BlockSpec((None, 1, TM_IN), lambda i, j: (i, 0, 0)),
            pl.BlockSpec((HALF, 1), lambda i, j: (0, 0)),
            pl.BlockSpec((TM_IN, D_MODEL), lambda i, j: (x_blk(i, j), 0)),
            pl.BlockSpec((1, D_MODEL), lambda i, j: (0, 0)),
            pl.BlockSpec((D_MODEL, TN_IN), lambda i, j: (0, tok_blk(j))),
            pl.BlockSpec((TN_IN, D_MODEL), lambda i, j: (qv_blk(j), 0)),
            pl.BlockSpec((kv_rows + FOX_HEADS, D_MODEL), lambda i, j: (0, 0)),
            pl.BlockSpec((FOX_HEADS, 1), lambda i, j: (0, 0)),
        ],
        out_specs=[
            pl.BlockSpec((TM_IN, TN_IN), lambda i, j: (i, tok_blk(j))),
            pl.BlockSpec((None, FOX_HEADS, TM_IN), lambda i, j: (i // nsb, 0, i % nsb)),
            pl.BlockSpec((None, TN_IN, TM_IN), lambda i, j: (i // nsb, qv_blk(j), i % nsb)),
            pl.BlockSpec((None, kv_rows, TM_IN), lambda i, j: (i // nsb, 0, i % nsb)),
        ],
        out_shape=[
            jax.ShapeDtypeStruct((t, D_PROJ), BF16),
            jax.ShapeDtypeStruct((batch, FOX_HEADS, seq), F32),
            jax.ShapeDtypeStruct((batch, D_QV, seq), BF16),
            jax.ShapeDtypeStruct((batch, kv_rows, seq), BF16),
        ],
        scratch_shapes=[
            pltpu.VMEM((TM_IN, D_MODEL), BF16),
            pltpu.VMEM((TM_IN, D_MODEL), BF16),
            pltpu.VMEM((HALF, TM_IN), F32),
            pltpu.VMEM((HALF, TM_IN), F32),
        ],
        compiler_params=pltpu.CompilerParams(
            dimension_semantics=("arbitrary", "arbitrary"),
            vmem_limit_bytes=57 * MIB),
        name="inproj",
    )(pos3, invf, x2, gain, w_tok, w_qv, w_kv, fbias)


def _decay_kernel(logf_ref, aug_ref, base_ref, carry_sc):
    sb = pl.program_id(1)

    @pl.when(sb == 0)
    def _():
        carry_sc[...] = jnp.zeros(carry_sc.shape, F32)

    row = lax.broadcasted_iota(jnp.int32, (TK_FOX, TK_FOX), 0)
    col = lax.broadcasted_iota(jnp.int32, (TK_FOX, TK_FOX), 1)
    upper = jnp.where(row <= col, 1.0, 0.0).astype(BF16)
    pad = jnp.zeros((LANES - 3 * FOX_HEADS, TK_FOX), F32)
    for c in range(DECAY_BLOCKS):
        toks = slice(c * TK_FOX, (c + 1) * TK_FOX)
        hi, mid, lo = _split3(logf_ref[:, toks] * LOG2E)
        cl = (jnp.dot(hi, upper, preferred_element_type=F32)
              + jnp.dot(mid, upper, preferred_element_type=F32)
              + jnp.dot(lo, upper, preferred_element_type=F32))
        yh, ym, yl = _split3(-cl)
        pieces_t = jnp.concatenate([yh.astype(F32), ym.astype(F32), yl.astype(F32), pad], axis=0)
        aug_ref[toks, :] = pieces_t.T.astype(BF16)
        base_ref[:, c:c + 1] = -carry_sc[...]
        carry_sc[...] = carry_sc[...] + cl[:, TK_FOX - 1:TK_FOX]


def _decay(logf_t, batch, seq):
    toks = DECAY_BLOCKS * TK_FOX
    ns = seq // toks
    return pl.pallas_call(
        _decay_kernel,
        grid=(batch, ns),
        in_specs=[pl.BlockSpec((None, FOX_HEADS, toks), lambda b, s: (b, 0, s))],
        out_specs=[
            pl.BlockSpec((toks, LANES), lambda b, s: (b * ns + s, 0)),
            pl.BlockSpec((None, None, FOX_HEADS, DECAY_BLOCKS), lambda b, s: (b, s, 0, 0)),
        ],
        out_shape=[
            jax.ShapeDtypeStruct((batch * seq, LANES), BF16),
            jax.ShapeDtypeStruct((batch, ns, FOX_HEADS, DECAY_BLOCKS), F32),
        ],
        scratch_shapes=[pltpu.VMEM((FOX_HEADS, 1), F32)],
        compiler_params=pltpu.CompilerParams(
            dimension_semantics=("arbitrary", "arbitrary")),
        name="decay",
    )(logf_t)


def _swa_kernel(sinks_ref, qt_ref, cur_ref, prev_ref, o_ref, ktok_sc):
    i = pl.program_id(1)
    nblk = TQ_SWA // WINDOW
    gw = SWA_GROUP * WINDOW
    kv_win = jnp.concatenate([prev_ref[...], cur_ref[...]], axis=1)
    ktok_sc[...] = kv_win[0:SWA_KV_W, :].astype(F32).T.astype(BF16)

    row = lax.broadcasted_iota(jnp.int32, (2 * WINDOW, gw), 0)
    qix = lax.broadcasted_iota(jnp.int32, (2 * WINDOW, gw), 1) % WINDOW
    allowed = (row > qix) & (row <= qix + WINDOW)
    bias_any = jnp.where(allowed, 0.0, NEG)
    bias_first = jnp.where(allowed & (row >= WINDOW), 0.0, NEG)
    bias_r0 = jnp.where(i == 0, bias_first, bias_any)
    zeros_q = jnp.zeros((HEAD_DIM, WINDOW), BF16)
    ones_rows = jnp.ones((FOX_DEN_ROWS, 2 * WINDOW), BF16)

    units = [(r, g) for r in range(nblk) for g in range(SWA_KV_HEADS)]

    def scores(unit):
        r, g = unit
        cols = slice(r * WINDOW, (r + 1) * WINDOW)
        tiles = []
        for hq in range(SWA_GROUP):
            h = g * SWA_GROUP + hq
            q_h = qt_ref[h * HEAD_DIM:(h + 1) * HEAD_DIM, cols]
            tiles.append(jnp.concatenate([q_h, zeros_q] if g == 0 else [zeros_q, q_h], axis=0))
        q_g = jnp.concatenate(tiles, axis=1)
        k_win = ktok_sc[r * WINDOW:(r + 2) * WINDOW, :]
        s = jnp.dot(k_win, q_g, preferred_element_type=F32)
        return s + (bias_r0 if r == 0 else bias_any)

    def finish(s, unit):
        r, g = unit
        sink = jnp.concatenate(
            [jnp.full((1, WINDOW), sinks_ref[g * SWA_GROUP + hq] * LOG2E, F32)
             for hq in range(SWA_GROUP)], axis=1)
        m = jnp.maximum(jnp.max(s, axis=0, keepdims=True), sink)
        pr = jnp.exp2(s - m).astype(BF16)
        v_rows = slice(SWA_KV_W + g * HEAD_DIM, SWA_KV_W + (g + 1) * HEAD_DIM)
        vt = jnp.concatenate([kv_win[v_rows, r * WINDOW:(r + 2) * WINDOW], ones_rows], axis=0)
        acc = jnp.dot(vt, pr, preferred_element_type=F32)
        den = acc[HEAD_DIM:HEAD_DIM + 1, :] + jnp.exp2(sink - m)
        o_t = acc[0:HEAD_DIM, :] * (1.0 / den)
        for pp in range(SWA_GROUP // 2):
            pair = jnp.concatenate([o_t[:, (2 * pp) * WINDOW:(2 * pp + 1) * WINDOW],
                                    o_t[:, (2 * pp + 1) * WINDOW:(2 * pp + 2) * WINDOW]], axis=0)
            c0 = (g * SWA_GROUP + 2 * pp) * HEAD_DIM
            o_ref[r * WINDOW:(r + 1) * WINDOW, c0:c0 + LANES] = pair.T.astype(BF16)

    pending = scores(units[0])
    for n, unit in enumerate(units):
        nxt = scores(units[n + 1]) if n + 1 < len(units) else None
        finish(pending, unit)
        pending = nxt


def _swa(qv_t, kv_t, sinks, batch, seq):
    nsq = seq // TQ_SWA
    per = TQ_SWA // WINDOW
    return pl.pallas_call(
        _swa_kernel,
        grid=(batch, nsq),
        in_specs=[
            pl.BlockSpec(memory_space=pltpu.SMEM),
            pl.BlockSpec((None, SWA_Q_W, TQ_SWA), lambda b, i: (b, ROW_AQ // SWA_Q_W, i)),
            pl.BlockSpec((None, 2 * SWA_KV_W, TQ_SWA), lambda b, i: (b, 0, i)),
            pl.BlockSpec((None, 2 * SWA_KV_W, WINDOW),
                         lambda b, i: (b, 0, jnp.maximum(i * per - 1, 0))),
        ],
        out_specs=pl.BlockSpec((TQ_SWA, SWA_Q_W), lambda b, i: (b * nsq + i, 0)),
        scratch_shapes=[pltpu.VMEM((WINDOW + TQ_SWA, SWA_KV_W), BF16)],
        out_shape=jax.ShapeDtypeStruct((batch * seq, SWA_Q_W), BF16),
        compiler_params=pltpu.CompilerParams(
            dimension_semantics=("arbitrary", "arbitrary")),
        name="swa",
    )(sinks, qv_t, kv_t, kv_t)


def _fox_kernel(base_ref, q_ref, k_ref, aug_ref, vt_ref, o_ref,
                m_sc, acc_sc, qt_sc, bm_sc, *bufs):
    b = pl.program_id(0)
    p = pl.program_id(1)
    qi = pl.program_id(2)
    s_bufs, p_bufs = bufs[:FOX_SLOTS], bufs[FOX_SLOTS:]
    dz = pl.multiple_of(jnp.minimum(qi, 0), FOX_ROWS)
    nb = k_ref.shape[0] // TK_FOX
    per = TQ_FOX // TK_FOX
    q_t = q_ref[...]
    feat = lax.broadcasted_iota(jnp.int32, (LANES, TQ_FOX), 0)
    zero = jnp.zeros_like(q_t)
    for hh in range(2):
        h = 2 * p + hh
        sel = (feat == h) | (feat == h + FOX_HEADS) | (feat == h + 2 * FOX_HEADS)
        qt_sc[hh, 0:LANES, :] = jnp.where((feat // HEAD_DIM) == hh, q_t, zero)
        qt_sc[hh, LANES:2 * LANES, :] = jnp.where(sel, 1.0, 0.0).astype(BF16)

    m_sc[...] = jnp.full(m_sc.shape, NEG, F32)
    acc_sc[...] = jnp.zeros(acc_sc.shape, F32)
    ones_rows = jnp.ones((FOX_DEN_ROWS, TK_FOX), BF16)

    def scores(blk, n, q_lo, diag):
        hh, slot, w = n % 2, n % FOX_SLOTS, TQ_FOX - q_lo
        ks = pl.ds(pl.multiple_of(blk * TQ_FOX + (n // 2) * TK_FOX, TK_FOX), TK_FOX)
        k_full = jnp.concatenate([k_ref[ks, :], aug_ref[ks, :]], axis=1)
        s = jnp.dot(k_full, qt_sc[hh, :, q_lo:], preferred_element_type=F32)
        if diag is not None:
            row = lax.broadcasted_iota(jnp.int32, s.shape, 0)
            col = lax.broadcasted_iota(jnp.int32, s.shape, 1)
            limit = 0 if diag is True else jnp.where(diag, 0, TK_FOX)
            s = jnp.where(row - col <= limit, s, NEG)
        s_bufs[slot][pl.ds(dz, TK_FOX), 0:w] = s
        bm_sc[slot, :, 0:w] = jnp.max(s, axis=0, keepdims=True)

    def finish(blk, n, q_lo):
        hh, slot, p_slot, w = n % 2, n % FOX_SLOTS, n % 2, TQ_FOX - q_lo
        ks = pl.ds(pl.multiple_of(blk * TQ_FOX + (n // 2) * TK_FOX, TK_FOX), TK_FOX)
        base = base_ref[(b * nb + blk * per + n // 2) * FOX_HEADS + 2 * p + hh]
        m_prev = m_sc[hh, :, q_lo:] - base
        m_new = jnp.maximum(m_prev, bm_sc[slot, :, 0:w])
        alpha = jnp.exp2(m_prev - m_new)
        for r in range(0, TK_FOX, FOX_ROWS):
            rows = pl.ds(dz + r, FOX_ROWS)
            pr = jnp.exp2(s_bufs[slot][rows, 0:w] - m_new)
            p_bufs[p_slot][rows, 0:w] = pr.astype(BF16)
        vt_h = jnp.concatenate([vt_ref[hh * HEAD_DIM:(hh + 1) * HEAD_DIM, ks], ones_rows], axis=0)
        acc_sc[hh, :, q_lo:] = alpha * acc_sc[hh, :, q_lo:] + jnp.dot(
            vt_h, p_bufs[p_slot][pl.ds(dz, TK_FOX), 0:w], preferred_element_type=F32)
        m_sc[hh, :, q_lo:] = m_new + base

    for n in range(FOX_AHEAD):
        scores(0, n, 0, qi == 0)

    def body(j, carry):
        for n in range(FOX_CHAINS):
            m = n + FOX_AHEAD
            if m < FOX_CHAINS:
                scores(j, m, 0, None)
            else:
                scores(j + 1, m - FOX_CHAINS, 0, j + 1 == qi)
            finish(j, n, 0)
        return carry

    lax.fori_loop(0, qi, body, 0)
    for n in range(FOX_CHAINS):
        m = n + FOX_AHEAD
        if m < FOX_CHAINS:
            scores(qi, m, (m // 2) * TK_FOX, True)
        finish(qi, n, (n // 2) * TK_FOX)

    o_t = jnp.concatenate(
        [acc_sc[hh, 0:HEAD_DIM, :] * (1.0 / acc_sc[hh, HEAD_DIM:HEAD_DIM + 1, :]) for hh in range(2)],
        axis=0)
    o_ref[...] = o_t.T.astype(BF16)


def _fox(proj, aug, base, qv_t, batch, seq):
    t = proj.shape[0]
    nq = seq // TQ_FOX
    pairs = FOX_HEADS // 2
    return pl.pallas_call(
        _fox_kernel,
        grid=(batch, pairs, nq),
        in_specs=[
            pl.BlockSpec(memory_space=pltpu.SMEM),
            pl.BlockSpec((None, LANES, TQ_FOX), lambda b, p, i: (b, ROW_FQ // LANES + p, i)),
            pl.BlockSpec((seq, LANES), lambda b, p, i: (b, COL_FK // LANES + p)),
            pl.BlockSpec((seq, LANES), lambda b, p, i: (b, 0)),
            pl.BlockSpec((None, LANES, seq), lambda b, p, i: (b, ROW_FV // LANES + p, 0)),
        ],
        out_specs=pl.BlockSpec((TQ_FOX, LANES), lambda b, p, i: (b * nq + i, p)),
        out_shape=jax.ShapeDtypeStruct((t, FOX_W), BF16),
        scratch_shapes=[
            pltpu.VMEM((2, 1, TQ_FOX), F32),
            pltpu.VMEM((2, HEAD_DIM + FOX_DEN_ROWS, TQ_FOX), F32),
            pltpu.VMEM((2, 2 * LANES, TQ_FOX), BF16),
            pltpu.VMEM((FOX_SLOTS, 1, TQ_FOX), F32),
        ] + [pltpu.VMEM((TK_FOX, TQ_FOX), F32)] * FOX_SLOTS + [
            pltpu.VMEM((TK_FOX, TQ_FOX), BF16),
            pltpu.VMEM((TK_FOX, TQ_FOX), BF16),
        ],
        compiler_params=pltpu.CompilerParams(
            dimension_semantics=("arbitrary", "arbitrary", "arbitrary")),
        name="fox",
    )(base, qv_t, proj, aug, qv_t)


def _merge_kernel(oa_ref, ob_ref, ga_ref, gb_ref, x_ref, wa_ref, wb_ref, wo_ref, gain_ref,
                  x1_ref, h2_ref):
    ya = jnp.dot(oa_ref[...], wa_ref[...], preferred_element_type=F32)
    yb = jnp.dot(ob_ref[...], wb_ref[...], preferred_element_type=F32)

    def gate(ref):
        return 1.0 / (1.0 + jnp.exp(-ref[...].astype(F32)))

    merged = gate(ga_ref) * ya + gate(gb_ref) * yb
    x1 = x_ref[...] + jnp.dot(merged.astype(BF16), wo_ref[...], preferred_element_type=F32)
    x1_ref[...] = x1
    h2_ref[...] = _rmsnorm_rows(x1, gain_ref[...]).astype(BF16)


def _merge(o_a, o_b, proj, x2, wa, wb, wo, gain):
    t = x2.shape[0]
    tm = TM_MERGE
    const = dict(pipeline_mode=pl.Buffered(1))
    return pl.pallas_call(
        _merge_kernel,
        grid=(t // tm,),
        in_specs=[
            pl.BlockSpec((tm, SWA_Q_W), lambda i: (i, 0)),
            pl.BlockSpec((tm, FOX_W), lambda i: (i, 0)),
            pl.BlockSpec((tm, D_MODEL), lambda i: (i, COL_GA // D_MODEL)),
            pl.BlockSpec((tm, D_MODEL), lambda i: (i, COL_GB // D_MODEL)),
            pl.BlockSpec((tm, D_MODEL), lambda i: (i, 0)),
            pl.BlockSpec((SWA_Q_W, D_MODEL), lambda i: (0, 0), **const),
            pl.BlockSpec((FOX_W, D_MODEL), lambda i: (0, 0), **const),
            pl.BlockSpec((D_MODEL, D_MODEL), lambda i: (0, 0), **const),
            pl.BlockSpec((1, D_MODEL), lambda i: (0, 0), **const),
        ],
        out_specs=[
            pl.BlockSpec((tm, D_MODEL), lambda i: (i, 0)),
            pl.BlockSpec((tm, D_MODEL), lambda i: (i, 0)),
        ],
        out_shape=[
            jax.ShapeDtypeStruct((t, D_MODEL), F32),
            jax.ShapeDtypeStruct((t, D_MODEL), BF16),
        ],
        compiler_params=pltpu.CompilerParams(
            dimension_semantics=("arbitrary",),
            vmem_limit_bytes=56 * MIB),
        name="merge",
    )(o_a, o_b, proj, proj, x2, wa, wb, wo, gain)


def _mlp_kernel(h2_ref, x1_ref, wup_ref, wdn_ref, gain_ref, o_ref, *, final_norm):
    f = pl.program_id(1)

    @pl.when(f == 0)
    def _():
        o_ref[...] = x1_ref[...]

    u = jnp.maximum(jnp.dot(h2_ref[...], wup_ref[...], preferred_element_type=F32), 0.0)
    o_ref[...] += jnp.dot((u * u).astype(BF16), wdn_ref[...], preferred_element_type=F32)

    if final_norm:
        @pl.when(f == pl.num_programs(1) - 1)
        def _():
            o_ref[...] = _rmsnorm_rows(o_ref[...], gain_ref[...])


def _mlp(h2, x1, wup, wdn, gain, final_norm):
    t = x1.shape[0]
    tm, tf = TM_MLP, TF_MLP
    return pl.pallas_call(
        functools.partial(_mlp_kernel, final_norm=final_norm),
        grid=(t // tm, D_FF // tf),
        in_specs=[
            pl.BlockSpec((tm, D_MODEL), lambda i, f: (i, 0)),
            pl.BlockSpec((tm, D_MODEL), lambda i, f: (i, 0)),
            pl.BlockSpec((D_MODEL, tf), lambda i, f: (0, f)),
            pl.BlockSpec((tf, D_MODEL), lambda i, f: (f, 0)),
            pl.BlockSpec((1, D_MODEL), lambda i, f: (0, 0)),
        ],
        out_specs=pl.BlockSpec((tm, D_MODEL), lambda i, f: (i, 0)),
        out_shape=jax.ShapeDtypeStruct((t, D_MODEL), F32),
        compiler_params=pltpu.CompilerParams(
            dimension_semantics=("arbitrary", "arbitrary"),
            vmem_limit_bytes=48 * MIB),
        name="mlp",
    )(h2, x1, wup, wdn, gain)


def _pack_w_in(w, scale):
    o = 0
    a_q = w[:, o:o + SWA_Q_W] * (scale * LOG2E); o += SWA_Q_W
    a_k = w[:, o:o + SWA_KV_W]; o += SWA_KV_W
    a_v = w[:, o:o + SWA_KV_W]; o += SWA_KV_W
    f_q = w[:, o:o + FOX_W] * (scale * LOG2E); o += FOX_W
    f_k = w[:, o:o + FOX_W]; o += FOX_W
    f_v = w[:, o:o + FOX_W]; o += FOX_W
    f_l = w[:, o:o + FOX_HEADS]; o += FOX_HEADS
    g_a = w[:, o:o + D_MODEL]; o += D_MODEL
    g_b = w[:, o:o + D_MODEL]; o += D_MODEL
    w_tok = jnp.concatenate([g_a, g_b, f_k], axis=1).astype(BF16)
    w_qv = jnp.concatenate([a_q, f_v, f_q], axis=1).T.astype(BF16)
    w_kv = jnp.concatenate([a_k, a_v, f_l], axis=1).T.astype(BF16)
    return w_tok, w_qv, w_kv


def kernel(x, positions, attn_norm, w_in, fox_f_bias, swa_sinks, w_branch_swa, w_branch_fox,
           w_out, mlp_norm, w_up, w_down, final_norm):
    batch, seq, d = x.shape
    depth = w_in.shape[0]
    assert d == D_MODEL and seq % TM_IN == 0 and seq % TQ_FOX == 0
    t = batch * seq
    scale = HEAD_DIM ** -0.5
    inv_freq = ROPE_THETA ** (-jnp.arange(0, HEAD_DIM, 2, dtype=F32) / HEAD_DIM)
    invf = inv_freq[:, None]
    pos3 = positions.reshape(t // TM_IN, 1, TM_IN)
    x2 = x.reshape(t, d)
    for l in range(depth):
        w_tok, w_qv, w_kv = _pack_w_in(w_in[l], scale)
        fbias = fox_f_bias[l].astype(F32)[:, None]
        proj, logf_t, qv_t, kv_t = _inproj(x2, pos3, invf, attn_norm[l][None, :].astype(F32),
                                           w_tok, w_qv, w_kv, fbias, batch, seq)
        aug, base = _decay(logf_t, batch, seq)
        base = base.transpose(0, 1, 3, 2).reshape(-1)
        o_a = _swa(qv_t, kv_t, swa_sinks[l].astype(F32), batch, seq)
        o_b = _fox(proj, aug, base, qv_t, batch, seq)
        x1, h2 = _merge(o_a, o_b, proj, x2,
                        w_branch_swa[l].astype(BF16), w_branch_fox[l].astype(BF16),
                        w_out[l].astype(BF16), mlp_norm[l][None, :].astype(F32))
        x2 = _mlp(h2, x1, w_up[l].astype(BF16), w_down[l].astype(BF16),
                  final_norm[None, :].astype(F32), final_norm=(l == depth - 1))
    return x2.reshape(batch, seq, d)
```

```python
import functools

import jax
import jax.numpy as jnp
from jax import lax
from jax.experimental import pallas as pl
from jax.experimental.pallas import tpu as pltpu

F32 = jnp.float32
BF16 = jnp.bfloat16

D_MODEL = 2048
HEAD_DIM = 64
HALF = HEAD_DIM // 2
SWA_Q_HEADS = 16
SWA_KV_HEADS = 2
SWA_GROUP = SWA_Q_HEADS // SWA_KV_HEADS
WINDOW = 128
FOX_HEADS = 16
D_FF = 4 * D_MODEL
ROPE_THETA = 10000.0
RMS_EPS = 1e-6
SWA_Q_W = SWA_Q_HEADS * HEAD_DIM
FOX_W = FOX_HEADS * HEAD_DIM

LANES = 128
NEG = -1e30
MIB = 1024 * 1024
LOG2E = 1.4426950408889634

COL_GA = 0
COL_GB = 2048
COL_FK = 4096
D_PROJ = 5120
ROW_AQ = 0
ROW_FV = SWA_Q_W
ROW_FQ = SWA_Q_W + FOX_W
D_QV = ROW_FQ + FOX_W
SWA_KV_W = SWA_KV_HEADS * HEAD_DIM

TM_IN = 1024
TN_IN = 1024
NJ_TOK = D_PROJ // TN_IN
J_AQ = NJ_TOK
J_KV = J_AQ + SWA_Q_W // TN_IN
J_FEAT = J_KV + 1
NJ_IN = J_FEAT + (D_QV - ROW_FV) // TN_IN
assert 2 * (NJ_TOK - 1) >= TM_IN // LANES

TQ_SWA = 512
TQ_FOX = 1024
TK_FOX = 256
FOX_AHEAD = 2
FOX_CHAINS = 2 * (TQ_FOX // TK_FOX)
FOX_SLOTS = 4
assert FOX_CHAINS % FOX_SLOTS == 0 and FOX_AHEAD < FOX_SLOTS
FOX_ROWS = 32
FOX_PIECES = 2
FOX_DEN_ROWS = 16
DECAY_BLOCKS = 4
TM_MERGE = 512
TM_MLP = 512
TF_MLP = 1024


def _rmsnorm_rows(x, gain):
    ms = jnp.mean(x * x, axis=-1, keepdims=True)
    return x * lax.rsqrt(ms + RMS_EPS) * gain


def _split3(x):
    hi = x.astype(BF16)
    r1 = x - hi.astype(F32)
    mid = r1.astype(BF16)
    lo = (r1 - mid.astype(F32)).astype(BF16)
    return hi, mid, lo


def _inproj_kernel(pos_ref, invf_ref, x_ref, gain_ref, w_ref, wqv_ref, wkv_ref, fbias_ref,
                   proj_ref, logf_ref, qv_ref, kv_ref, h_sc, hn_sc, cos_sc, sin_sc):
    i = pl.program_id(0)
    j = pl.program_id(1)
    nchunk = TM_IN // LANES

    def norm_chunk(dst, c):
        rows = pl.ds(pl.multiple_of(c * LANES, LANES), LANES)
        dst[rows, :] = _rmsnorm_rows(x_ref[rows, :], gain_ref[...]).astype(BF16)

    @pl.when(j == 0)
    def _():
        @pl.when(i == 0)
        def _():
            def body(r, _):
                norm_chunk(h_sc, r)
                return 0
            lax.fori_loop(0, nchunk, body, 0)

        @pl.when(i > 0)
        def _():
            h_sc[...] = hn_sc[...]

        ang = invf_ref[...] * pos_ref[...].astype(F32)
        cos_sc[...] = jnp.cos(ang)
        sin_sc[...] = jnp.sin(ang)

    def features(wt_ref):
        return lax.dot_general(wt_ref[...], h_sc[...], (((1,), (1,)), ((), ())),
                               preferred_element_type=F32)

    def rope_rows(acc, heads, out_ref):
        for hd in range(heads):
            r0 = hd * HEAD_DIM
            t1 = acc[r0:r0 + HALF, :]
            t2 = acc[r0 + HALF:r0 + HEAD_DIM, :]
            out_ref[r0:r0 + HALF, :] = (t1 * cos_sc[...] - t2 * sin_sc[...]).astype(BF16)
            out_ref[r0 + HALF:r0 + HEAD_DIM, :] = (t2 * cos_sc[...] + t1 * sin_sc[...]).astype(BF16)

    @pl.when(j < NJ_TOK)
    def _():
        proj_ref[...] = jnp.dot(h_sc[...], w_ref[...], preferred_element_type=F32).astype(BF16)
        c0 = jnp.clip(2 * (j - (i == 0).astype(jnp.int32)), 0, nchunk - 2)
        norm_chunk(hn_sc, c0)
        norm_chunk(hn_sc, c0 + 1)

    @pl.when((j >= J_AQ) & (j < J_KV))
    def _():
        rope_rows(features(wqv_ref), TN_IN // HEAD_DIM, qv_ref)

    @pl.when(j == J_KV)
    def _():
        acc = features(wkv_ref)
        rope_rows(acc, SWA_KV_HEADS, kv_ref)
        kv_ref[SWA_KV_W:2 * SWA_KV_W, :] = acc[SWA_KV_W:2 * SWA_KV_W, :].astype(BF16)
        z = acc[2 * SWA_KV_W:, :] + fbias_ref[...]
        logf_ref[...] = jnp.minimum(z, 0.0) - jnp.log1p(jnp.exp(-jnp.abs(z)))

    @pl.when(j >= J_FEAT)
    def _():
        qv_ref[...] = features(wqv_ref).astype(BF16)


def _inproj(x2, pos3, invf, gain, w_tok, w_qv, w_kv, fbias, batch, seq):
    t = x2.shape[0]
    nsb = seq // TM_IN
    kv_rows = 2 * SWA_KV_W

    def qv_blk(j):
        return jnp.where(j >= J_FEAT, j - J_FEAT + ROW_FV // TN_IN,
                         jnp.clip(j - J_AQ, 0, SWA_Q_W // TN_IN - 1))

    def tok_blk(j):
        return jnp.minimum(j, NJ_TOK - 1)

    def x_blk(i, j):
        return jnp.where((i == 0) & (j == 0), 0, jnp.minimum(i + 1, t // TM_IN - 1))

    return pl.pallas_call(
        _inproj_kernel,
        grid=(t // TM_IN, NJ_IN),
        in_specs=[
            pl.BlockSpec((None, 1, TM_IN), lambda i, j: (i, 0, 0)),
            pl.BlockSpec((HALF, 1), lambda i, j: (0, 0)),
            pl.BlockSpec((TM_IN, D_MODEL), lambda i, j: (x_blk(i, j), 0)),
            pl.BlockSpec((1, D_MODEL), lambda i, j: (0, 0)),
            pl.BlockSpec((D_MODEL, TN_IN), lambda i, j: (0, tok_blk(j))),
            pl.BlockSpec((TN_IN, D_MODEL), lambda i, j: (qv_blk(j), 0)),
            pl.BlockSpec((kv_rows + FOX_HEADS, D_MODEL), lambda i, j: (0, 0)),
            pl.BlockSpec((FOX_HEADS, 1), lambda i, j: (0, 0)),
        ],
        out_specs=[
            pl.BlockSpec((TM_IN, TN_IN), lambda i, j: (i, tok_blk(j))),
            pl.BlockSpec((None, FOX_HEADS, TM_IN), lambda i, j: (i // nsb, 0, i % nsb)),
            pl.BlockSpec((None, TN_IN, TM_IN), lambda i, j: (i // nsb, qv_blk(j), i % nsb)),
            pl.BlockSpec((None, kv_rows, TM_IN), lambda i, j: (i // nsb, 0, i % nsb)),
        ],
        out_shape=[
            jax.ShapeDtypeStruct((t, D_PROJ), BF16),
            jax.ShapeDtypeStruct((batch, FOX_HEADS, seq), F32),
            jax.ShapeDtypeStruct((batch, D_QV, seq), BF16),
            jax.ShapeDtypeStruct((batch, kv_rows, seq), BF16),
        ],
        scratch_shapes=[
            pltpu.VMEM((TM_IN, D_MODEL), BF16),
            pltpu.VMEM((TM_IN, D_MODEL), BF16),
            pltpu.VMEM((HALF, TM_IN), F32),
            pltpu.VMEM((HALF, TM_IN), F32),
        ],
        compiler_params=pltpu.CompilerParams(
            dimension_semantics=("arbitrary", "arbitrary"),
            vmem_limit_bytes=57 * MIB),
        name="inproj",
    )(pos3, invf, x2, gain, w_tok, w_qv, w_kv, fbias)


def _decay_kernel(logf_ref, aug_ref, base_ref, carry_sc):
    sb = pl.program_id(1)

    @pl.when(sb == 0)
    def _():
        carry_sc[...] = jnp.zeros(carry_sc.shape, F32)

    row = lax.broadcasted_iota(jnp.int32, (TK_FOX, TK_FOX), 0)
    col = lax.broadcasted_iota(jnp.int32, (TK_FOX, TK_FOX), 1)
    upper = jnp.where(row <= col, 1.0, 0.0).astype(BF16)
    pad = jnp.zeros((LANES - 3 * FOX_HEADS, TK_FOX), F32)
    for c in range(DECAY_BLOCKS):
        toks = slice(c * TK_FOX, (c + 1) * TK_FOX)
        hi, mid, lo = _split3(logf_ref[:, toks] * LOG2E)
        cl = (jnp.dot(hi, upper, preferred_element_type=F32)
              + jnp.dot(mid, upper, preferred_element_type=F32)
              + jnp.dot(lo, upper, preferred_element_type=F32))
        yh, ym, yl = _split3(-cl)
        pieces_t = jnp.concatenate([yh.astype(F32), ym.astype(F32), yl.astype(F32), pad], axis=0)
        aug_ref[toks, :] = pieces_t.T.astype(BF16)
        base_ref[:, c:c + 1] = -carry_sc[...]
        carry_sc[...] = carry_sc[...] + cl[:, TK_FOX - 1:TK_FOX]


def _decay(logf_t, batch, seq):
    toks = DECAY_BLOCKS * TK_FOX
    ns = seq // toks
    return pl.pallas_call(
        _decay_kernel,
        grid=(batch, ns),
        in_specs=[pl.BlockSpec((None, FOX_HEADS, toks), lambda b, s: (b, 0, s))],
        out_specs=[
            pl.BlockSpec((toks, LANES), lambda b, s: (b * ns + s, 0)),
            pl.BlockSpec((None, None, FOX_HEADS, DECAY_BLOCKS), lambda b, s: (b, s, 0, 0)),
        ],
        out_shape=[
            jax.ShapeDtypeStruct((batch * seq, LANES), BF16),
            jax.ShapeDtypeStruct((batch, ns, FOX_HEADS, DECAY_BLOCKS), F32),
        ],
        scratch_shapes=[pltpu.VMEM((FOX_HEADS, 1), F32)],
        compiler_params=pltpu.CompilerParams(
            dimension_semantics=("arbitrary", "arbitrary")),
        name="decay",
    )(logf_t)


def _swa_kernel(sinks_ref, qt_ref, cur_ref, prev_ref, o_ref, ktok_sc):
    i = pl.program_id(1)
    nblk = TQ_SWA // WINDOW
    gw = SWA_GROUP * WINDOW
    kv_win = jnp.concatenate([prev_ref[...], cur_ref[...]], axis=1)
    ktok_sc[...] = kv_win[0:SWA_KV_W, :].astype(F32).T.astype(BF16)

    row = lax.broadcasted_iota(jnp.int32, (2 * WINDOW, gw), 0)
    qix = lax.broadcasted_iota(jnp.int32, (2 * WINDOW, gw), 1) % WINDOW
    allowed = (row > qix) & (row <= qix + WINDOW)
    bias_any = jnp.where(allowed, 0.0, NEG)
    bias_first = jnp.where(allowed & (row >= WINDOW), 0.0, NEG)
    bias_r0 = jnp.where(i == 0, bias_first, bias_any)
    zeros_q = jnp.zeros((HEAD_DIM, WINDOW), BF16)
    ones_rows = jnp.ones((FOX_DEN_ROWS, 2 * WINDOW), BF16)

    units = [(r, g) for r in range(nblk) for g in range(SWA_KV_HEADS)]

    def scores(unit):
        r, g = unit
        cols = slice(r * WINDOW, (r + 1) * WINDOW)
        tiles = []
        for hq in range(SWA_GROUP):
            h = g * SWA_GROUP + hq
            q_h = qt_ref[h * HEAD_DIM:(h + 1) * HEAD_DIM, cols]
            tiles.append(jnp.concatenate([q_h, zeros_q] if g == 0 else [zeros_q, q_h], axis=0))
        q_g = jnp.concatenate(tiles, axis=1)
        k_win = ktok_sc[r * WINDOW:(r + 2) * WINDOW, :]
        s = jnp.dot(k_win, q_g, preferred_element_type=F32)
        return s + (bias_r0 if r == 0 else bias_any)

    def finish(s, unit):
        r, g = unit
        sink = jnp.concatenate(
            [jnp.full((1, WINDOW), sinks_ref[g * SWA_GROUP + hq] * LOG2E, F32)
             for hq in range(SWA_GROUP)], axis=1)
        m = jnp.maximum(jnp.max(s, axis=0, keepdims=True), sink)
        pr = jnp.exp2(s - m).astype(BF16)
        v_rows = slice(SWA_KV_W + g * HEAD_DIM, SWA_KV_W + (g + 1) * HEAD_DIM)
        vt = jnp.concatenate([kv_win[v_rows, r * WINDOW:(r + 2) * WINDOW], ones_rows], axis=0)
        acc = jnp.dot(vt, pr, preferred_element_type=F32)
        den = acc[HEAD_DIM:HEAD_DIM + 1, :] + jnp.exp2(sink - m)
        o_t = acc[0:HEAD_DIM, :] * (1.0 / den)
        for pp in range(SWA_GROUP // 2):
            pair = jnp.concatenate([o_t[:, (2 * pp) * WINDOW:(2 * pp + 1) * WINDOW],
                                    o_t[:, (2 * pp + 1) * WINDOW:(2 * pp + 2) * WINDOW]], axis=0)
            c0 = (g * SWA_GROUP + 2 * pp) * HEAD_DIM
            o_ref[r * WINDOW:(r + 1) * WINDOW, c0:c0 + LANES] = pair.T.astype(BF16)

    pending = scores(units[0])
    for n, unit in enumerate(units):
        nxt = scores(units[n + 1]) if n + 1 < len(units) else None
        finish(pending, unit)
        pending = nxt


def _swa(qv_t, kv_t, sinks, batch, seq):
    nsq = seq // TQ_SWA
    per = TQ_SWA // WINDOW
    return pl.pallas_call(
        _swa_kernel,
        grid=(batch, nsq),
        in_specs=[
            pl.BlockSpec(memory_space=pltpu.SMEM),
            pl.BlockSpec((None, SWA_Q_W, TQ_SWA), lambda b, i: (b, ROW_AQ // SWA_Q_W, i)),
            pl.BlockSpec((None, 2 * SWA_KV_W, TQ_SWA), lambda b, i: (b, 0, i)),
            pl.BlockSpec((None, 2 * SWA_KV_W, WINDOW),
                         lambda b, i: (b, 0, jnp.maximum(i * per - 1, 0))),
        ],
        out_specs=pl.BlockSpec((TQ_SWA, SWA_Q_W), lambda b, i: (b * nsq + i, 0)),
        scratch_shapes=[pltpu.VMEM((WINDOW + TQ_SWA, SWA_KV_W), BF16)],
        out_shape=jax.ShapeDtypeStruct((batch * seq, SWA_Q_W), BF16),
        compiler_params=pltpu.CompilerParams(
            dimension_semantics=("arbitrary", "arbitrary")),
        name="swa",
    )(sinks, qv_t, kv_t, kv_t)


def _fox_kernel(base_ref, q_ref, k_ref, aug_ref, vt_ref, o_ref,
                m_sc, ml_sc, al_sc, acc_sc, qt_sc, bm_sc, *bufs):
    b = pl.program_id(0)
    p = pl.program_id(1)
    qi = pl.program_id(2)
    s_bufs, p_bufs = bufs[:FOX_SLOTS], bufs[FOX_SLOTS:]
    dz = pl.multiple_of(jnp.minimum(qi, 0), FOX_ROWS)
    nb = k_ref.shape[0] // TK_FOX
    per = TQ_FOX // TK_FOX
    q_t = q_ref[...]
    feat = lax.broadcasted_iota(jnp.int32, (LANES, TQ_FOX), 0)
    zero = jnp.zeros_like(q_t)
    for hh in range(2):
        h = 2 * p + hh
        sel = (feat == h) | (feat == h + FOX_HEADS) | (feat == h + 2 * FOX_HEADS)
        qt_sc[hh, 0:LANES, :] = jnp.where((feat // HEAD_DIM) == hh, q_t, zero)
        qt_sc[hh, LANES:2 * LANES, :] = jnp.where(sel, 1.0, 0.0).astype(BF16)

    m_sc[...] = jnp.full(m_sc.shape, NEG, F32)
    acc_sc[...] = jnp.zeros(acc_sc.shape, F32)
    ones_rows = jnp.ones((FOX_DEN_ROWS, TK_FOX), BF16)

    def pieces(q_lo):
        step = TQ_FOX // FOX_PIECES
        return [(max(q_lo, u * step), (u + 1) * step) for u in range(FOX_PIECES)
                if (u + 1) * step > q_lo]

    def key_rows(blk, n):
        return pl.ds(pl.multiple_of(blk * TQ_FOX + (n // 2) * TK_FOX, TK_FOX), TK_FOX)

    def scores(blk, n, q_lo, diag, lo, hi):
        hh, slot = n % 2, n % FOX_SLOTS
        ks = key_rows(blk, n)
        k_full = jnp.concatenate([k_ref[ks, :], aug_ref[ks, :]], axis=1)
        s = jnp.dot(k_full, qt_sc[hh, :, lo:hi], preferred_element_type=F32)
        if diag is not None:
            row = lax.broadcasted_iota(jnp.int32, s.shape, 0)
            col = lax.broadcasted_iota(jnp.int32, s.shape, 1)
            limit = lo - q_lo if diag is True else jnp.where(diag, lo - q_lo, TK_FOX)
            s = jnp.where(row - col <= limit, s, NEG)
        s_bufs[slot][pl.ds(dz, TK_FOX), lo:hi] = s
        bm_sc[slot, :, lo:hi] = jnp.max(s, axis=0, keepdims=True)

    def stats(blk, n, q_lo):
        hh, slot = n % 2, n % FOX_SLOTS
        base = base_ref[(b * nb + blk * per + n // 2) * FOX_HEADS + 2 * p + hh]
        m_prev = m_sc[hh, :, q_lo:] - base
        m_new = jnp.maximum(m_prev, bm_sc[slot, :, q_lo:])
        al_sc[hh, :, q_lo:] = jnp.exp2(m_prev - m_new)
        ml_sc[hh, :, q_lo:] = m_new
        m_sc[hh, :, q_lo:] = m_new + base

    def probs(n, lo, hi):
        hh, slot, p_slot = n % 2, n % FOX_SLOTS, n % 2
        m_new = ml_sc[hh, :, lo:hi]
        for r in range(0, TK_FOX, FOX_ROWS):
            rows = pl.ds(dz + r, FOX_ROWS)
            pr = jnp.exp2(s_bufs[slot][rows, lo:hi] - m_new)
            p_bufs[p_slot][rows, lo:hi] = pr.astype(BF16)

    def pv(blk, n, lo, hi):
        hh, p_slot = n % 2, n % 2
        vt_h = jnp.concatenate(
            [vt_ref[hh * HEAD_DIM:(hh + 1) * HEAD_DIM, key_rows(blk, n)], ones_rows], axis=0)
        acc_sc[hh, :, lo:hi] = al_sc[hh, :, lo:hi] * acc_sc[hh, :, lo:hi] + jnp.dot(
            vt_h, p_bufs[p_slot][pl.ds(dz, TK_FOX), lo:hi], preferred_element_type=F32)

    def chain_step(blk, n, q_lo, ahead):
        stats(blk, n, q_lo)
        mine = pieces(q_lo)
        theirs = ahead[1] if ahead is not None else []
        for u in range(max(len(mine), len(theirs))):
            if u < len(mine):
                probs(n, *mine[u])
            if u < len(theirs):
                ahead[0](*theirs[u])
            if u < len(mine):
                pv(blk, n, *mine[u])

    def ahead_of(blk, n, q_lo, diag):
        return (functools.partial(scores, blk, n, q_lo, diag), pieces(q_lo))

    for n in range(FOX_AHEAD):
        for lo, hi in pieces(0):
            scores(0, n, 0, qi == 0, lo, hi)

    def body(j, carry):
        for n in range(FOX_CHAINS):
            m = n + FOX_AHEAD
            if m < FOX_CHAINS:
                ahead = ahead_of(j, m, 0, None)
            else:
                ahead = ahead_of(j + 1, m - FOX_CHAINS, 0, j + 1 == qi)
            chain_step(j, n, 0, ahead)
        return carry

    lax.fori_loop(0, qi, body, 0)
    for n in range(FOX_CHAINS):
        m = n + FOX_AHEAD
        ahead = ahead_of(qi, m, (m // 2) * TK_FOX, True) if m < FOX_CHAINS else None
        chain_step(qi, n, (n // 2) * TK_FOX, ahead)

    o_t = jnp.concatenate(
        [acc_sc[hh, 0:HEAD_DIM, :] * (1.0 / acc_sc[hh, HEAD_DIM:HEAD_DIM + 1, :]) for hh in range(2)],
        axis=0)
    o_ref[...] = o_t.T.astype(BF16)


def _fox(proj, aug, base, qv_t, batch, seq):
    t = proj.shape[0]
    nq = seq // TQ_FOX
    pairs = FOX_HEADS // 2
    return pl.pallas_call(
        _fox_kernel,
        grid=(batch, pairs, nq),
        in_specs=[
            pl.BlockSpec(memory_space=pltpu.SMEM),
            pl.BlockSpec((None, LANES, TQ_FOX), lambda b, p, i: (b, ROW_FQ // LANES + p, i)),
            pl.BlockSpec((seq, LANES), lambda b, p, i: (b, COL_FK // LANES + p)),
            pl.BlockSpec((seq, LANES), lambda b, p, i: (b, 0)),
            pl.BlockSpec((None, LANES, seq), lambda b, p, i: (b, ROW_FV // LANES + p, 0)),
        ],
        out_specs=pl.BlockSpec((TQ_FOX, LANES), lambda b, p, i: (b * nq + i, p)),
        out_shape=jax.ShapeDtypeStruct((t, FOX_W), BF16),
        scratch_shapes=[
            pltpu.VMEM((2, 1, TQ_FOX), F32),
            pltpu.VMEM((2, 1, TQ_FOX), F32),
            pltpu.VMEM((2, 1, TQ_FOX), F32),
            pltpu.VMEM((2, HEAD_DIM + FOX_DEN_ROWS, TQ_FOX), F32),
            pltpu.VMEM((2, 2 * LANES, TQ_FOX), BF16),
            pltpu.VMEM((FOX_SLOTS, 1, TQ_FOX), F32),
        ] + [pltpu.VMEM((TK_FOX, TQ_FOX), F32)] * FOX_SLOTS + [
            pltpu.VMEM((TK_FOX, TQ_FOX), BF16),
            pltpu.VMEM((TK_FOX, TQ_FOX), BF16),
        ],
        compiler_params=pltpu.CompilerParams(
            dimension_semantics=("arbitrary", "arbitrary", "arbitrary")),
        name="fox",
    )(base, qv_t, proj, aug, qv_t)


def _merge_kernel(oa_ref, ob_ref, ga_ref, gb_ref, x_ref, wa_ref, wb_ref, wo_ref, gain_ref,
                  x1_ref, h2_ref):
    ya = jnp.dot(oa_ref[...], wa_ref[...], preferred_element_type=F32)
    yb = jnp.dot(ob_ref[...], wb_ref[...], preferred_element_type=F32)

    def gate(ref):
        return 1.0 / (1.0 + jnp.exp(-ref[...].astype(F32)))

    merged = gate(ga_ref) * ya + gate(gb_ref) * yb
    x1 = x_ref[...] + jnp.dot(merged.astype(BF16), wo_ref[...], preferred_element_type=F32)
    x1_ref[...] = x1
    h2_ref[...] = _rmsnorm_rows(x1, gain_ref[...]).astype(BF16)


def _merge(o_a, o_b, proj, x2, wa, wb, wo, gain):
    t = x2.shape[0]
    tm = TM_MERGE
    const = dict(pipeline_mode=pl.Buffered(1))
    return pl.pallas_call(
        _merge_kernel,
        grid=(t // tm,),
        in_specs=[
            pl.BlockSpec((tm, SWA_Q_W), lambda i: (i, 0)),
            pl.BlockSpec((tm, FOX_W), lambda i: (i, 0)),
            pl.BlockSpec((tm, D_MODEL), lambda i: (i, COL_GA // D_MODEL)),
            pl.BlockSpec((tm, D_MODEL), lambda i: (i, COL_GB // D_MODEL)),
            pl.BlockSpec((tm, D_MODEL), lambda i: (i, 0)),
            pl.BlockSpec((SWA_Q_W, D_MODEL), lambda i: (0, 0), **const),
            pl.BlockSpec((FOX_W, D_MODEL), lambda i: (0, 0), **const),
            pl.BlockSpec((D_MODEL, D_MODEL), lambda i: (0, 0), **const),
            pl.BlockSpec((1, D_MODEL), lambda i: (0, 0), **const),
        ],
        out_specs=[
            pl.BlockSpec((tm, D_MODEL), lambda i: (i, 0)),
            pl.BlockSpec((tm, D_MODEL), lambda i: (i, 0)),
        ],
        out_shape=[
            jax.ShapeDtypeStruct((t, D_MODEL), F32),
            jax.ShapeDtypeStruct((t, D_MODEL), BF16),
        ],
        compiler_params=pltpu.CompilerParams(
            dimension_semantics=("arbitrary",),
            vmem_limit_bytes=56 * MIB),
        name="merge",
    )(o_a, o_b, proj, proj, x2, wa, wb, wo, gain)


def _mlp_kernel(h2_ref, x1_ref, wup_ref, wdn_ref, gain_ref, o_ref, *, final_norm):
    f = pl.program_id(1)
    last = pl.num_programs(1) - 1

    def delta():
        u = jnp.maximum(jnp.dot(h2_ref[...], wup_ref[...], preferred_element_type=F32), 0.0)
        return jnp.dot((u * u).astype(BF16), wdn_ref[...], preferred_element_type=F32)

    @pl.when(f == 0)
    def _():
        o_ref[...] = x1_ref[...] + delta()

    @pl.when((f > 0) & (f < last))
    def _():
        o_ref[...] += delta()

    @pl.when(f == last)
    def _():
        y = o_ref[...] + delta()
        o_ref[...] = _rmsnorm_rows(y, gain_ref[...]) if final_norm else y


def _mlp(h2, x1, wup, wdn, gain, final_norm):
    t = x1.shape[0]
    tm, tf = TM_MLP, TF_MLP
    return pl.pallas_call(
        functools.partial(_mlp_kernel, final_norm=final_norm),
        grid=(t // tm, D_FF // tf),
        in_specs=[
            pl.BlockSpec((tm, D_MODEL), lambda i, f: (i, 0)),
            pl.BlockSpec((tm, D_MODEL), lambda i, f: (i, 0)),
            pl.BlockSpec((D_MODEL, tf), lambda i, f: (0, f)),
            pl.BlockSpec((tf, D_MODEL), lambda i, f: (f, 0)),
            pl.BlockSpec((1, D_MODEL), lambda i, f: (0, 0)),
        ],
        out_specs=pl.BlockSpec((tm, D_MODEL), lambda i, f: (i, 0)),
        out_shape=jax.ShapeDtypeStruct((t, D_MODEL), F32),
        compiler_params=pltpu.CompilerParams(
            dimension_semantics=("arbitrary", "arbitrary"),
            vmem_limit_bytes=48 * MIB),
        name="mlp",
    )(h2, x1, wup, wdn, gain)


def _pack_w_in(w, scale):
    o = 0
    a_q = w[:, o:o + SWA_Q_W] * (scale * LOG2E); o += SWA_Q_W
    a_k = w[:, o:o + SWA_KV_W]; o += SWA_KV_W
    a_v = w[:, o:o + SWA_KV_W]; o += SWA_KV_W
    f_q = w[:, o:o + FOX_W] * (scale * LOG2E); o += FOX_W
    f_k = w[:, o:o + FOX_W]; o += FOX_W
    f_v = w[:, o:o + FOX_W]; o += FOX_W
    f_l = w[:, o:o + FOX_HEADS]; o += FOX_HEADS
    g_a = w[:, o:o + D_MODEL]; o += D_MODEL
    g_b = w[:, o:o + D_MODEL]; o += D_MODEL
    w_tok = jnp.concatenate([g_a, g_b, f_k], axis=1).astype(BF16)
    w_qv = jnp.concatenate([a_q, f_v, f_q], axis=1).T.astype(BF16)
    w_kv = jnp.concatenate([a_k, a_v, f_l], axis=1).T.astype(BF16)
    return w_tok, w_qv, w_kv


def kernel(x, positions, attn_norm, w_in, fox_f_bias, swa_sinks, w_branch_swa, w_branch_fox,
           w_out, mlp_norm, w_up, w_down, final_norm):
    batch, seq, d = x.shape
    depth = w_in.shape[0]
    assert d == D_MODEL and seq % TM_IN == 0 and seq % TQ_FOX == 0
    t = batch * seq
    scale = HEAD_DIM ** -0.5
    inv_freq = ROPE_THETA ** (-jnp.arange(0, HEAD_DIM, 2, dtype=F32) / HEAD_DIM)
    invf = inv_freq[:, None]
    pos3 = positions.reshape(t // TM_IN, 1, TM_IN)
    x2 = x.reshape(t, d)
    for l in range(depth):
        w_tok, w_qv, w_kv = _pack_w_in(w_in[l], scale)
        fbias = fox_f_bias[l].astype(F32)[:, None]
        proj, logf_t, qv_t, kv_t = _inproj(x2, pos3, invf, attn_norm[l][None, :].astype(F32),
                                           w_tok, w_qv, w_kv, fbias, batch, seq)
        aug, base = _decay(logf_t, batch, seq)
        base = base.transpose(0, 1, 3, 2).reshape(-1)
        o_a = _swa(qv_t, kv_t, swa_sinks[l].astype(F32), batch, seq)
        o_b = _fox(proj, aug, base, qv_t, batch, seq)
        x1, h2 = _merge(o_a, o_b, proj, x2,
                        w_branch_swa[l].astype(BF16), w_branch_fox[l].astype(BF16),
                        w_out[l].astype(BF16), mlp_norm[l][None, :].astype(F32))
        x2 = _mlp(h2, x1, w_up[l].astype(BF16), w_down[l].astype(BF16),
                  final_norm[None, :].astype(F32), final_norm=(l == depth - 1))
    return x2.reshape(batch, seq, d)
```

```python
import functools

import jax
import jax.numpy as jnp
from jax import lax
from jax.experimental import pallas as pl
from jax.experimental.pallas import tpu as pltpu

F32 = jnp.float32
BF16 = jnp.bfloat16

D_MODEL = 2048
HEAD_DIM = 64
HALF = HEAD_DIM // 2
SWA_Q_HEADS = 16
SWA_KV_HEADS = 2
SWA_GROUP = SWA_Q_HEADS // SWA_KV_HEADS
WINDOW = 128
FOX_HEADS = 16
D_FF = 4 * D_MODEL
ROPE_THETA = 10000.0
RMS_EPS = 1e-6
SWA_Q_W = SWA_Q_HEADS * HEAD_DIM
FOX_W = FOX_HEADS * HEAD_DIM

LANES = 128
NEG = -1e30
MIB = 1024 * 1024
LOG2E = 1.4426950408889634

COL_GA = 0
COL_GB = 2048
COL_FK = 4096
D_PROJ = 5120
ROW_AQ = 0
ROW_FV = SWA_Q_W
ROW_FQ = SWA_Q_W + FOX_W
D_QV = ROW_FQ + FOX_W
SWA_KV_W = SWA_KV_HEADS * HEAD_DIM

TM_IN = 1024
TN_IN = 1024
NJ_TOK = D_PROJ // TN_IN
J_AQ = NJ_TOK
J_KV = J_AQ + SWA_Q_W // TN_IN
J_FEAT = J_KV + 1
NJ_IN = J_FEAT + (D_QV - ROW_FV) // TN_IN
assert 2 * (NJ_TOK - 1) >= TM_IN // LANES

TQ_SWA = 512
TQ_FOX = 1024
TK_FOX = 256
FOX_AHEAD = 2
FOX_CHAINS = 2 * (TQ_FOX // TK_FOX)
FOX_SLOTS = 4
assert FOX_CHAINS % FOX_SLOTS == 0 and FOX_AHEAD < FOX_SLOTS
FOX_ROWS = 32
FOX_PIECES = 2
FOX_DEN_ROWS = 16
DECAY_BLOCKS = 4
TM_MERGE = 512
TM_MLP = 512
TF_MLP = 1024


def _rmsnorm_rows(x, gain):
    ms = jnp.mean(x * x, axis=-1, keepdims=True)
    return x * lax.rsqrt(ms + RMS_EPS) * gain


def _split3(x):
    hi = x.astype(BF16)
    r1 = x - hi.astype(F32)
    mid = r1.astype(BF16)
    lo = (r1 - mid.astype(F32)).astype(BF16)
    return hi, mid, lo


def _inproj_kernel(pos_ref, invf_ref, x_ref, gain_ref, w_ref, wqv_ref, wkv_ref, fbias_ref,
                   proj_ref, logf_ref, qv_ref, kv_ref, h_sc, hn_sc, cos_sc, sin_sc):
    i = pl.program_id(0)
    j = pl.program_id(1)
    nchunk = TM_IN // LANES

    def norm_chunk(dst, c):
        rows = pl.ds(pl.multiple_of(c * LANES, LANES), LANES)
        dst[rows, :] = _rmsnorm_rows(x_ref[rows, :], gain_ref[...]).astype(BF16)

    @pl.when(j == 0)
    def _():
        @pl.when(i == 0)
        def _():
            def body(r, _):
                norm_chunk(h_sc, r)
                return 0
            lax.fori_loop(0, nchunk, body, 0)

        @pl.when(i > 0)
        def _():
            h_sc[...] = hn_sc[...]

        ang = invf_ref[...] * pos_ref[...].astype(F32)
        cos_sc[...] = jnp.cos(ang)
        sin_sc[...] = jnp.sin(ang)

    def features(wt_ref):
        return lax.dot_general(wt_ref[...], h_sc[...], (((1,), (1,)), ((), ())),
                               preferred_element_type=F32)

    def rope_rows(acc, heads, out_ref):
        for hd in range(heads):
            r0 = hd * HEAD_DIM
            t1 = acc[r0:r0 + HALF, :]
            t2 = acc[r0 + HALF:r0 + HEAD_DIM, :]
            out_ref[r0:r0 + HALF, :] = (t1 * cos_sc[...] - t2 * sin_sc[...]).astype(BF16)
            out_ref[r0 + HALF:r0 + HEAD_DIM, :] = (t2 * cos_sc[...] + t1 * sin_sc[...]).astype(BF16)

    @pl.when(j < NJ_TOK)
    def _():
        proj_ref[...] = jnp.dot(h_sc[...], w_ref[...], preferred_element_type=F32).astype(BF16)
        c0 = jnp.clip(2 * (j - (i == 0).astype(jnp.int32)), 0, nchunk - 2)
        norm_chunk(hn_sc, c0)
        norm_chunk(hn_sc, c0 + 1)

    @pl.when((j >= J_AQ) & (j < J_KV))
    def _():
        rope_rows(features(wqv_ref), TN_IN // HEAD_DIM, qv_ref)

    @pl.when(j == J_KV)
    def _():
        acc = features(wkv_ref)
        rope_rows(acc, SWA_KV_HEADS, kv_ref)
        kv_ref[SWA_KV_W:2 * SWA_KV_W, :] = acc[SWA_KV_W:2 * SWA_KV_W, :].astype(BF16)
        z = acc[2 * SWA_KV_W:, :] + fbias_ref[...]
        logf_ref[...] = jnp.minimum(z, 0.0) - jnp.log1p(jnp.exp(-jnp.abs(z)))

    @pl.when(j >= J_FEAT)
    def _():
        qv_ref[...] = features(wqv_ref).astype(BF16)


def _inproj(x2, pos3, invf, gain, w_tok, w_qv, w_kv, fbias, batch, seq):
    t = x2.shape[0]
    nsb = seq // TM_IN
    kv_rows = 2 * SWA_KV_W

    def qv_blk(j):
        return jnp.where(j >= J_FEAT, j - J_FEAT + ROW_FV // TN_IN,
                         jnp.clip(j - J_AQ, 0, SWA_Q_W // TN_IN - 1))

    def tok_blk(j):
        return jnp.minimum(j, NJ_TOK - 1)

    def x_blk(i, j):
        return jnp.where((i == 0) & (j == 0), 0, jnp.minimum(i + 1, t // TM_IN - 1))

    return pl.pallas_call(
        _inproj_kernel,
        grid=(t // TM_IN, NJ_IN),
        in_specs=[
            pl.BlockSpec((None, 1, TM_IN), lambda i, j: (i, 0, 0)),
            pl.BlockSpec((HALF, 1), lambda i, j: (0, 0)),
            pl.BlockSpec((TM_IN, D_MODEL), lambda i, j: (x_blk(i, j), 0)),
            pl.BlockSpec((1, D_MODEL), lambda i, j: (0, 0)),
            pl.BlockSpec((D_MODEL, TN_IN), lambda i, j: (0, tok_blk(j))),
            pl.BlockSpec((TN_IN, D_MODEL), lambda i, j: (qv_blk(j), 0)),
            pl.BlockSpec((kv_rows + FOX_HEADS, D_MODEL), lambda i, j: (0, 0)),
            pl.BlockSpec((FOX_HEADS, 1), lambda i, j: (0, 0)),
        ],
        out_specs=[
            pl.BlockSpec((TM_IN, TN_IN), lambda i, j: (i, tok_blk(j))),
            pl.BlockSpec((None, FOX_HEADS, TM_IN), lambda i, j: (i // nsb, 0, i % nsb)),
            pl.BlockSpec((None, TN_IN, TM_IN), lambda i, j: (i // nsb, qv_blk(j), i % nsb)),
            pl.BlockSpec((None, kv_rows, TM_IN), lambda i, j: (i // nsb, 0, i % nsb)),
        ],
        out_shape=[
            jax.ShapeDtypeStruct((t, D_PROJ), BF16),
            jax.ShapeDtypeStruct((batch, FOX_HEADS, seq), F32),
            jax.ShapeDtypeStruct((batch, D_QV, seq), BF16),
            jax.ShapeDtypeStruct((batch, kv_rows, seq), BF16),
        ],
        scratch_shapes=[
            pltpu.VMEM((TM_IN, D_MODEL), BF16),
            pltpu.VMEM((TM_IN, D_MODEL), BF16),
            pltpu.VMEM((HALF, TM_IN), F32),
            pltpu.VMEM((HALF, TM_IN), F32),
        ],
        compiler_params=pltpu.CompilerParams(
            dimension_semantics=("arbitrary", "arbitrary"),
            vmem_limit_bytes=57 * MIB),
        name="inproj",
    )(pos3, invf, x2, gain, w_tok, w_qv, w_kv, fbias)


def _decay_kernel(logf_ref, aug_ref, base_ref, carry_sc):
    sb = pl.program_id(1)

    @pl.when(sb == 0)
    def _():
        carry_sc[...] = jnp.zeros(carry_sc.shape, F32)

    row = lax.broadcasted_iota(jnp.int32, (TK_FOX, TK_FOX), 0)
    col = lax.broadcasted_iota(jnp.int32, (TK_FOX, TK_FOX), 1)
    upper = jnp.where(row <= col, 1.0, 0.0).astype(BF16)
    pad = jnp.zeros((LANES - 3 * FOX_HEADS, TK_FOX), F32)
    for c in range(DECAY_BLOCKS):
        toks = slice(c * TK_FOX, (c + 1) * TK_FOX)
        hi, mid, lo = _split3(logf_ref[:, toks] * LOG2E)
        cl = (jnp.dot(hi, upper, preferred_element_type=F32)
              + jnp.dot(mid, upper, preferred_element_type=F32)
              + jnp.dot(lo, upper, preferred_element_type=F32))
        yh, ym, yl = _split3(-cl)
        pieces_t = jnp.concatenate([yh.astype(F32), ym.astype(F32), yl.astype(F32), pad], axis=0)
        aug_ref[toks, :] = pieces_t.T.astype(BF16)
        base_ref[:, c:c + 1] = -carry_sc[...]
        carry_sc[...] = carry_sc[...] + cl[:, TK_FOX - 1:TK_FOX]


def _decay(logf_t, batch, seq):
    toks = DECAY_BLOCKS * TK_FOX
    ns = seq // toks
    return pl.pallas_call(
        _decay_kernel,
        grid=(batch, ns),
        in_specs=[pl.BlockSpec((None, FOX_HEADS, toks), lambda b, s: (b, 0, s))],
        out_specs=[
            pl.BlockSpec((toks, LANES), lambda b, s: (b * ns + s, 0)),
            pl.BlockSpec((None, None, FOX_HEADS, DECAY_BLOCKS), lambda b, s: (b, s, 0, 0)),
        ],
        out_shape=[
            jax.ShapeDtypeStruct((batch * seq, LANES), BF16),
            jax.ShapeDtypeStruct((batch, ns, FOX_HEADS, DECAY_BLOCKS), F32),
        ],
        scratch_shapes=[pltpu.VMEM((FOX_HEADS, 1), F32)],
        compiler_params=pltpu.CompilerParams(
            dimension_semantics=("arbitrary", "arbitrary")),
        name="decay",
    )(logf_t)


def _swa_kernel(sinks_ref, qt_ref, cur_ref, prev_ref, o_ref, ktok_sc):
    i = pl.program_id(1)
    nblk = TQ_SWA // WINDOW
    gw = SWA_GROUP * WINDOW
    kv_win = jnp.concatenate([prev_ref[...], cur_ref[...]], axis=1)
    ktok_sc[...] = kv_win[0:SWA_KV_W, :].astype(F32).T.astype(BF16)

    row = lax.broadcasted_iota(jnp.int32, (2 * WINDOW, gw), 0)
    qix = lax.broadcasted_iota(jnp.int32, (2 * WINDOW, gw), 1) % WINDOW
    allowed = (row > qix) & (row <= qix + WINDOW)
    bias_any = jnp.where(allowed, 0.0, NEG)
    bias_first = jnp.where(allowed & (row >= WINDOW), 0.0, NEG)
    bias_r0 = jnp.where(i == 0, bias_first, bias_any)
    zeros_q = jnp.zeros((HEAD_DIM, WINDOW), BF16)
    ones_rows = jnp.ones((FOX_DEN_ROWS, 2 * WINDOW), BF16)

    units = [(r, g) for r in range(nblk) for g in range(SWA_KV_HEADS)]

    def scores(unit):
        r, g = unit
        cols = slice(r * WINDOW, (r + 1) * WINDOW)
        tiles = []
        for hq in range(SWA_GROUP):
            h = g * SWA_GROUP + hq
            q_h = qt_ref[h * HEAD_DIM:(h + 1) * HEAD_DIM, cols]
            tiles.append(jnp.concatenate([q_h, zeros_q] if g == 0 else [zeros_q, q_h], axis=0))
        q_g = jnp.concatenate(tiles, axis=1)
        k_win = ktok_sc[r * WINDOW:(r + 2) * WINDOW, :]
        s = jnp.dot(k_win, q_g, preferred_element_type=F32)
        return s + (bias_r0 if r == 0 else bias_any)

    def finish(s, unit):
        r, g = unit
        sink = jnp.concatenate(
            [jnp.full((1, WINDOW), sinks_ref[g * SWA_GROUP + hq] * LOG2E, F32)
             for hq in range(SWA_GROUP)], axis=1)
        m = jnp.maximum(jnp.max(s, axis=0, keepdims=True), sink)
        pr = jnp.exp2(s - m).astype(BF16)
        v_rows = slice(SWA_KV_W + g * HEAD_DIM, SWA_KV_W + (g + 1) * HEAD_DIM)
        vt = jnp.concatenate([kv_win[v_rows, r * WINDOW:(r + 2) * WINDOW], ones_rows], axis=0)
        acc = jnp.dot(vt, pr, preferred_element_type=F32)
        den = acc[HEAD_DIM:HEAD_DIM + 1, :] + jnp.exp2(sink - m)
        o_t = acc[0:HEAD_DIM, :] * (1.0 / den)
        for pp in range(SWA_GROUP // 2):
            pair = jnp.concatenate([o_t[:, (2 * pp) * WINDOW:(2 * pp + 1) * WINDOW],
                                    o_t[:, (2 * pp + 1) * WINDOW:(2 * pp + 2) * WINDOW]], axis=0)
            c0 = (g * SWA_GROUP + 2 * pp) * HEAD_DIM
            o_ref[r * WINDOW:(r + 1) * WINDOW, c0:c0 + LANES] = pair.T.astype(BF16)

    pending = scores(units[0])
    for n, unit in enumerate(units):
        nxt = scores(units[n + 1]) if n + 1 < len(units) else None
        finish(pending, unit)
        pending = nxt


def _swa(qv_t, kv_t, sinks, batch, seq):
    nsq = seq // TQ_SWA
    per = TQ_SWA // WINDOW
    return pl.pallas_call(
        _swa_kernel,
        grid=(batch, nsq),
        in_specs=[
            pl.BlockSpec(memory_space=pltpu.SMEM),
            pl.BlockSpec((None, SWA_Q_W, TQ_SWA), lambda b, i: (b, ROW_AQ // SWA_Q_W, i)),
            pl.BlockSpec((None, 2 * SWA_KV_W, TQ_SWA), lambda b, i: (b, 0, i)),
            pl.BlockSpec((None, 2 * SWA_KV_W, WINDOW),
                         lambda b, i: (b, 0, jnp.maximum(i * per - 1, 0))),
        ],
        out_specs=pl.BlockSpec((TQ_SWA, SWA_Q_W), lambda b, i: (b * nsq + i, 0)),
        scratch_shapes=[pltpu.VMEM((WINDOW + TQ_SWA, SWA_KV_W), BF16)],
        out_shape=jax.ShapeDtypeStruct((batch * seq, SWA_Q_W), BF16),
        compiler_params=pltpu.CompilerParams(
            dimension_semantics=("arbitrary", "arbitrary")),
        name="swa",
    )(sinks, qv_t, kv_t, kv_t)


def _fox_kernel(base_ref, q_ref, k_ref, aug_ref, vt_ref, o_ref,
                m_sc, ml_sc, al_sc, acc_sc, qt_sc, bm_sc, *bufs):
    b = pl.program_id(0)
    p = pl.program_id(1)
    qi = pl.program_id(2)
    s_bufs, p_bufs = bufs[:FOX_SLOTS], bufs[FOX_SLOTS:]
    dz = pl.multiple_of(jnp.minimum(qi, 0), FOX_ROWS)
    nb = k_ref.shape[0] // TK_FOX
    per = TQ_FOX // TK_FOX
    q_t = q_ref[...]
    feat = lax.broadcasted_iota(jnp.int32, (LANES, TQ_FOX), 0)
    zero = jnp.zeros_like(q_t)
    for hh in range(2):
        h = 2 * p + hh
        sel = (feat == h) | (feat == h + FOX_HEADS) | (feat == h + 2 * FOX_HEADS)
        qt_sc[hh, 0:LANES, :] = jnp.where((feat // HEAD_DIM) == hh, q_t, zero)
        qt_sc[hh, LANES:2 * LANES, :] = jnp.where(sel, 1.0, 0.0).astype(BF16)

    m_sc[...] = jnp.full(m_sc.shape, NEG, F32)
    acc_sc[...] = jnp.zeros(acc_sc.shape, F32)
    ones_rows = jnp.ones((FOX_DEN_ROWS, TK_FOX), BF16)

    def pieces(q_lo):
        step = TQ_FOX // FOX_PIECES
        return [(max(q_lo, u * step), (u + 1) * step) for u in range(FOX_PIECES)
                if (u + 1) * step > q_lo]

    def key_rows(blk, n):
        return pl.ds(pl.multiple_of(blk * TQ_FOX + (n // 2) * TK_FOX, TK_FOX), TK_FOX)

    def scores(blk, n, q_lo, diag, lo, hi):
        hh, slot = n % 2, n % FOX_SLOTS
        ks = key_rows(blk, n)
        k_full = jnp.concatenate([k_ref[ks, :], aug_ref[ks, :]], axis=1)
        s = jnp.dot(k_full, qt_sc[hh, :, lo:hi], preferred_element_type=F32)
        if diag is not None:
            row = lax.broadcasted_iota(jnp.int32, s.shape, 0)
            col = lax.broadcasted_iota(jnp.int32, s.shape, 1)
            limit = lo - q_lo if diag is True else jnp.where(diag, lo - q_lo, TK_FOX)
            s = jnp.where(row - col <= limit, s, NEG)
        s_bufs[slot][pl.ds(dz, TK_FOX), lo:hi] = s
        bm_sc[slot, :, lo:hi] = jnp.max(s, axis=0, keepdims=True)

    def stats(blk, n, q_lo):
        hh, slot = n % 2, n % FOX_SLOTS
        base = base_ref[(b * nb + blk * per + n // 2) * FOX_HEADS + 2 * p + hh]
        m_prev = m_sc[hh, :, q_lo:] - base
        m_new = jnp.maximum(m_prev, bm_sc[slot, :, q_lo:])
        al_sc[hh, :, q_lo:] = jnp.exp2(m_prev - m_new)
        ml_sc[hh, :, q_lo:] = m_new
        m_sc[hh, :, q_lo:] = m_new + base

    def probs(n, lo, hi):
        hh, slot, p_slot = n % 2, n % FOX_SLOTS, n % 2
        m_new = ml_sc[hh, :, lo:hi]
        for r in range(0, TK_FOX, FOX_ROWS):
            rows = pl.ds(dz + r, FOX_ROWS)
            p_bufs[p_slot][rows, lo:hi] = jnp.exp2((s_bufs[slot][rows, lo:hi] - m_new).astype(BF16))

    def pv(blk, n, lo, hi):
        hh, p_slot = n % 2, n % 2
        vt_h = jnp.concatenate(
            [vt_ref[hh * HEAD_DIM:(hh + 1) * HEAD_DIM, key_rows(blk, n)], ones_rows], axis=0)
        acc_sc[hh, :, lo:hi] = al_sc[hh, :, lo:hi] * acc_sc[hh, :, lo:hi] + jnp.dot(
            vt_h, p_bufs[p_slot][pl.ds(dz, TK_FOX), lo:hi], preferred_element_type=F32)

    def chain_step(blk, n, q_lo, ahead):
        stats(blk, n, q_lo)
        mine = pieces(q_lo)
        theirs = ahead[1] if ahead is not None else []
        for u in range(max(len(mine), len(theirs))):
            if u < len(mine):
                probs(n, *mine[u])
            if u < len(theirs):
                ahead[0](*theirs[u])
            if u < len(mine):
                pv(blk, n, *mine[u])

    def ahead_of(blk, n, q_lo, diag):
        return (functools.partial(scores, blk, n, q_lo, diag), pieces(q_lo))

    for n in range(FOX_AHEAD):
        for lo, hi in pieces(0):
            scores(0, n, 0, qi == 0, lo, hi)

    def body(j, carry):
        for n in range(FOX_CHAINS):
            m = n + FOX_AHEAD
            if m < FOX_CHAINS:
                ahead = ahead_of(j, m, 0, None)
            else:
                ahead = ahead_of(j + 1, m - FOX_CHAINS, 0, j + 1 == qi)
            chain_step(j, n, 0, ahead)
        return carry

    lax.fori_loop(0, qi, body, 0)
    for n in range(FOX_CHAINS):
        m = n + FOX_AHEAD
        ahead = ahead_of(qi, m, (m // 2) * TK_FOX, True) if m < FOX_CHAINS else None
        chain_step(qi, n, (n // 2) * TK_FOX, ahead)

    o_t = jnp.concatenate(
        [acc_sc[hh, 0:HEAD_DIM, :] * (1.0 / acc_sc[hh, HEAD_DIM:HEAD_DIM + 1, :]) for hh in range(2)],
        axis=0)
    o_ref[...] = o_t.T.astype(BF16)


def _fox(proj, aug, base, qv_t, batch, seq):
    t = proj.shape[0]
    nq = seq // TQ_FOX
    pairs = FOX_HEADS // 2
    return pl.pallas_call(
        _fox_kernel,
        grid=(batch, pairs, nq),
        in_specs=[
            pl.BlockSpec(memory_space=pltpu.SMEM),
            pl.BlockSpec((None, LANES, TQ_FOX), lambda b, p, i: (b, ROW_FQ // LANES + p, i)),
            pl.BlockSpec((seq, LANES), lambda b, p, i: (b, COL_FK // LANES + p)),
            pl.BlockSpec((seq, LANES), lambda b, p, i: (b, 0)),
            pl.BlockSpec((None, LANES, seq), lambda b, p, i: (b, ROW_FV // LANES + p, 0)),
        ],
        out_specs=pl.BlockSpec((TQ_FOX, LANES), lambda b, p, i: (b * nq + i, p)),
        out_shape=jax.ShapeDtypeStruct((t, FOX_W), BF16),
        scratch_shapes=[
            pltpu.VMEM((2, 1, TQ_FOX), F32),
            pltpu.VMEM((2, 1, TQ_FOX), F32),
            pltpu.VMEM((2, 1, TQ_FOX), F32),
            pltpu.VMEM((2, HEAD_DIM + FOX_DEN_ROWS, TQ_FOX), F32),
            pltpu.VMEM((2, 2 * LANES, TQ_FOX), BF16),
            pltpu.VMEM((FOX_SLOTS, 1, TQ_FOX), F32),
        ] + [pltpu.VMEM((TK_FOX, TQ_FOX), F32)] * FOX_SLOTS + [
            pltpu.VMEM((TK_FOX, TQ_FOX), BF16),
            pltpu.VMEM((TK_FOX, TQ_FOX), BF16),
        ],
        compiler_params=pltpu.CompilerParams(
            dimension_semantics=("arbitrary", "arbitrary", "arbitrary")),
        name="fox",
    )(base, qv_t, proj, aug, qv_t)


def _merge_kernel(oa_ref, ob_ref, ga_ref, gb_ref, x_ref, wa_ref, wb_ref, wo_ref, gain_ref,
                  x1_ref, h2_ref):
    ya = jnp.dot(oa_ref[...], wa_ref[...], preferred_element_type=F32)
    yb = jnp.dot(ob_ref[...], wb_ref[...], preferred_element_type=F32)

    def gate(ref):
        return 1.0 / (1.0 + jnp.exp(-ref[...].astype(F32)))

    merged = gate(ga_ref) * ya + gate(gb_ref) * yb
    x1 = x_ref[...] + jnp.dot(merged.astype(BF16), wo_ref[...], preferred_element_type=F32)
    x1_ref[...] = x1
    h2_ref[...] = _rmsnorm_rows(x1, gain_ref[...]).astype(BF16)


def _merge(o_a, o_b, proj, x2, wa, wb, wo, gain):
    t = x2.shape[0]
    tm = TM_MERGE
    const = dict(pipeline_mode=pl.Buffered(1))
    return pl.pallas_call(
        _merge_kernel,
        grid=(t // tm,),
        in_specs=[
            pl.BlockSpec((tm, SWA_Q_W), lambda i: (i, 0)),
            pl.BlockSpec((tm, FOX_W), lambda i: (i, 0)),
            pl.BlockSpec((tm, D_MODEL), lambda i: (i, COL_GA // D_MODEL)),
            pl.BlockSpec((tm, D_MODEL), lambda i: (i, COL_GB // D_MODEL)),
            pl.BlockSpec((tm, D_MODEL), lambda i: (i, 0)),
            pl.BlockSpec((SWA_Q_W, D_MODEL), lambda i: (0, 0), **const),
            pl.BlockSpec((FOX_W, D_MODEL), lambda i: (0, 0), **const),
            pl.BlockSpec((D_MODEL, D_MODEL), lambda i: (0, 0), **const),
            pl.BlockSpec((1, D_MODEL), lambda i: (0, 0), **const),
        ],
        out_specs=[
            pl.BlockSpec((tm, D_MODEL), lambda i: (i, 0)),
            pl.BlockSpec((tm, D_MODEL), lambda i: (i, 0)),
        ],
        out_shape=[
            jax.ShapeDtypeStruct((t, D_MODEL), F32),
            jax.ShapeDtypeStruct((t, D_MODEL), BF16),
        ],
        compiler_params=pltpu.CompilerParams(
            dimension_semantics=("arbitrary",),
            vmem_limit_bytes=56 * MIB),
        name="merge",
    )(o_a, o_b, proj, proj, x2, wa, wb, wo, gain)


def _mlp_kernel(h2_ref, x1_ref, wup_ref, wdn_ref, gain_ref, o_ref, *, final_norm):
    f = pl.program_id(1)
    last = pl.num_programs(1) - 1

    def delta():
        u = jnp.maximum(jnp.dot(h2_ref[...], wup_ref[...], preferred_element_type=F32), 0.0)
        return jnp.dot((u * u).astype(BF16), wdn_ref[...], preferred_element_type=F32)

    @pl.when(f == 0)
    def _():
        o_ref[...] = x1_ref[...] + delta()

    @pl.when((f > 0) & (f < last))
    def _():
        o_ref[...] += delta()

    @pl.when(f == last)
    def _():
        y = o_ref[...] + delta()
        o_ref[...] = _rmsnorm_rows(y, gain_ref[...]) if final_norm else y


def _mlp(h2, x1, wup, wdn, gain, final_norm):
    t = x1.shape[0]
    tm, tf = TM_MLP, TF_MLP
    return pl.pallas_call(
        functools.partial(_mlp_kernel, final_norm=final_norm),
        grid=(t // tm, D_FF // tf),
        in_specs=[
            pl.BlockSpec((tm, D_MODEL), lambda i, f: (i, 0)),
            pl.BlockSpec((tm, D_MODEL), lambda i, f: (i, 0)),
            pl.BlockSpec((D_MODEL, tf), lambda i, f: (0, f)),
            pl.BlockSpec((tf, D_MODEL), lambda i, f: (f, 0)),
            pl.BlockSpec((1, D_MODEL), lambda i, f: (0, 0)),
        ],
        out_specs=pl.BlockSpec((tm, D_MODEL), lambda i, f: (i, 0)),
        out_shape=jax.ShapeDtypeStruct((t, D_MODEL), F32),
        compiler_params=pltpu.CompilerParams(
            dimension_semantics=("arbitrary", "arbitrary"),
            vmem_limit_bytes=48 * MIB),
        name="mlp",
    )(h2, x1, wup, wdn, gain)


def _pack_w_in(w, scale):
    o = 0
    a_q = w[:, o:o + SWA_Q_W] * (scale * LOG2E); o += SWA_Q_W
    a_k = w[:, o:o + SWA_KV_W]; o += SWA_KV_W
    a_v = w[:, o:o + SWA_KV_W]; o += SWA_KV_W
    f_q = w[:, o:o + FOX_W] * (scale * LOG2E); o += FOX_W
    f_k = w[:, o:o + FOX_W]; o += FOX_W
    f_v = w[:, o:o + FOX_W]; o += FOX_W
    f_l = w[:, o:o + FOX_HEADS]; o += FOX_HEADS
    g_a = w[:, o:o + D_MODEL]; o += D_MODEL
    g_b = w[:, o:o + D_MODEL]; o += D_MODEL
    w_tok = jnp.concatenate([g_a, g_b, f_k], axis=1).astype(BF16)
    w_qv = jnp.concatenate([a_q, f_v, f_q], axis=1).T.astype(BF16)
    w_kv = jnp.concatenate([a_k, a_v, f_l], axis=1).T.astype(BF16)
    return w_tok, w_qv, w_kv


def kernel(x, positions, attn_norm, w_in, fox_f_bias, swa_sinks, w_branch_swa, w_branch_fox,
           w_out, mlp_norm, w_up, w_down, final_norm):
    batch, seq, d = x.shape
    depth = w_in.shape[0]
    assert d == D_MODEL and seq % TM_IN == 0 and seq % TQ_FOX == 0
    t = batch * seq
    scale = HEAD_DIM ** -0.5
    inv_freq = ROPE_THETA ** (-jnp.arange(0, HEAD_DIM, 2, dtype=F32) / HEAD_DIM)
    invf = inv_freq[:, None]
    pos3 = positions.reshape(t // TM_IN, 1, TM_IN)
    x2 = x.reshape(t, d)
    for l in range(depth):
        w_tok, w_qv, w_kv = _pack_w_in(w_in[l], scale)
        fbias = fox_f_bias[l].astype(F32)[:, None]
        proj, logf_t, qv_t, kv_t = _inproj(x2, pos3, invf, attn_norm[l][None, :].astype(F32),
                                           w_tok, w_qv, w_kv, fbias, batch, seq)
        aug, base = _decay(logf_t, batch, seq)
        base = base.transpose(0, 1, 3, 2).reshape(-1)
        o_a = _swa(qv_t, kv_t, swa_sinks[l].astype(F32), batch, seq)
        o_b = _fox(proj, aug, base, qv_t, batch, seq)
        x1, h2 = _merge(o_a, o_b, proj, x2,
                        w_branch_swa[l].astype(BF16), w_branch_fox[l].astype(BF16),
                        w_out[l].astype(BF16), mlp_norm[l][None, :].astype(F32))
        x2 = _mlp(h2, x1, w_up[l].astype(BF16), w_down[l].astype(BF16),
                  final_norm[None, :].astype(F32), final_norm=(l == depth - 1))
    return x2.reshape(batch, seq, d)
```

```python
import functools

import jax
import jax.numpy as jnp
from jax import lax
from jax.experimental import pallas as pl
from jax.experimental.pallas import tpu as pltpu

F32 = jnp.float32
BF16 = jnp.bfloat16

D_MODEL = 2048
HEAD_DIM = 64
HALF = HEAD_DIM // 2
SWA_Q_HEADS = 16
SWA_KV_HEADS = 2
SWA_GROUP = SWA_Q_HEADS // SWA_KV_HEADS
WINDOW = 128
FOX_HEADS = 16
D_FF = 4 * D_MODEL
ROPE_THETA = 10000.0
RMS_EPS = 1e-6
SWA_Q_W = SWA_Q_HEADS * HEAD_DIM
FOX_W = FOX_HEADS * HEAD_DIM

LANES = 128
NEG = -1e30
MIB = 1024 * 1024
LOG2E = 1.4426950408889634

COL_GA = 0
COL_GB = 2048
COL_FK = 4096
D_PROJ = 5120
ROW_AQ = 0
ROW_FV = SWA_Q_W
ROW_FQ = SWA_Q_W + FOX_W
D_QV = ROW_FQ + FOX_W
SWA_KV_W = SWA_KV_HEADS * HEAD_DIM

TM_IN = 1024
TN_IN = 1024
NJ_TOK = D_PROJ // TN_IN
J_AQ = NJ_TOK
J_KV = J_AQ + SWA_Q_W // TN_IN
J_FEAT = J_KV + 1
NJ_IN = J_FEAT + (D_QV - ROW_FV) // TN_IN
assert 2 * (NJ_TOK - 1) >= TM_IN // LANES

TQ_SWA = 512
TQ_FOX = 1024
TK_FOX = 256
FOX_AHEAD = 2
FOX_CHAINS = 2 * (TQ_FOX // TK_FOX)
FOX_SLOTS = 4
assert FOX_CHAINS % FOX_SLOTS == 0 and FOX_AHEAD < FOX_SLOTS
FOX_ROWS = 32
FOX_PIECES = 2
FOX_DEN_ROWS = 64
SWA_DEN_ROWS = 16
DECAY_BLOCKS = 4
TM_MERGE = 512
TM_MLP = 512
TF_MLP = 1024


def _rmsnorm_rows(x, gain):
    ms = jnp.mean(x * x, axis=-1, keepdims=True)
    return x * lax.rsqrt(ms + RMS_EPS) * gain


def _split3(x):
    hi = x.astype(BF16)
    r1 = x - hi.astype(F32)
    mid = r1.astype(BF16)
    lo = (r1 - mid.astype(F32)).astype(BF16)
    return hi, mid, lo


def _inproj_kernel(pos_ref, invf_ref, x_ref, gain_ref, w_ref, wqv_ref, wkv_ref, fbias_ref,
                   proj_ref, logf_ref, qv_ref, kv_ref, h_sc, hn_sc, cos_sc, sin_sc):
    i = pl.program_id(0)
    j = pl.program_id(1)
    nchunk = TM_IN // LANES

    def norm_chunk(dst, c):
        rows = pl.ds(pl.multiple_of(c * LANES, LANES), LANES)
        dst[rows, :] = _rmsnorm_rows(x_ref[rows, :], gain_ref[...]).astype(BF16)

    @pl.when(j == 0)
    def _():
        @pl.when(i == 0)
        def _():
            def body(r, _):
                norm_chunk(h_sc, r)
                return 0
            lax.fori_loop(0, nchunk, body, 0)

        @pl.when(i > 0)
        def _():
            h_sc[...] = hn_sc[...]

        ang = invf_ref[...] * pos_ref[...].astype(F32)
        cos_sc[...] = jnp.cos(ang)
        sin_sc[...] = jnp.sin(ang)

    def features(wt_ref):
        return lax.dot_general(wt_ref[...], h_sc[...], (((1,), (1,)), ((), ())),
                               preferred_element_type=F32)

    def rope_rows(acc, heads, out_ref):
        for hd in range(heads):
            r0 = hd * HEAD_DIM
            t1 = acc[r0:r0 + HALF, :]
            t2 = acc[r0 + HALF:r0 + HEAD_DIM, :]
            out_ref[r0:r0 + HALF, :] = (t1 * cos_sc[...] - t2 * sin_sc[...]).astype(BF16)
            out_ref[r0 + HALF:r0 + HEAD_DIM, :] = (t2 * cos_sc[...] + t1 * sin_sc[...]).astype(BF16)

    @pl.when(j < NJ_TOK)
    def _():
        proj_ref[...] = jnp.dot(h_sc[...], w_ref[...], preferred_element_type=F32).astype(BF16)
        c0 = jnp.clip(2 * (j - (i == 0).astype(jnp.int32)), 0, nchunk - 2)
        norm_chunk(hn_sc, c0)
        norm_chunk(hn_sc, c0 + 1)

    @pl.when((j >= J_AQ) & (j < J_KV))
    def _():
        rope_rows(features(wqv_ref), TN_IN // HEAD_DIM, qv_ref)

    @pl.when(j == J_KV)
    def _():
        acc = features(wkv_ref)
        rope_rows(acc, SWA_KV_HEADS, kv_ref)
        kv_ref[SWA_KV_W:2 * SWA_KV_W, :] = acc[SWA_KV_W:2 * SWA_KV_W, :].astype(BF16)
        z = acc[2 * SWA_KV_W:, :] + fbias_ref[...]
        logf_ref[...] = jnp.minimum(z, 0.0) - jnp.log1p(jnp.exp(-jnp.abs(z)))

    @pl.when(j >= J_FEAT)
    def _():
        qv_ref[...] = features(wqv_ref).astype(BF16)


def _inproj(x2, pos3, invf, gain, w_tok, w_qv, w_kv, fbias, batch, seq):
    t = x2.shape[0]
    nsb = seq // TM_IN
    kv_rows = 2 * SWA_KV_W

    def qv_blk(j):
        return jnp.where(j >= J_FEAT, j - J_FEAT + ROW_FV // TN_IN,
                         jnp.clip(j - J_AQ, 0, SWA_Q_W // TN_IN - 1))

    def tok_blk(j):
        return jnp.minimum(j, NJ_TOK - 1)

    def x_blk(i, j):
        return jnp.where((i == 0) & (j == 0), 0, jnp.minimum(i + 1, t // TM_IN - 1))

    return pl.pallas_call(
        _inproj_kernel,
        grid=(t // TM_IN, NJ_IN),
        in_specs=[
            pl.BlockSpec((None, 1, TM_IN), lambda i, j: (i, 0, 0)),
            pl.BlockSpec((HALF, 1), lambda i, j: (0, 0)),
            pl.BlockSpec((TM_IN, D_MODEL), lambda i, j: (x_blk(i, j), 0)),
            pl.BlockSpec((1, D_MODEL), lambda i, j: (0, 0)),
            pl.BlockSpec((D_MODEL, TN_IN), lambda i, j: (0, tok_blk(j))),
            pl.BlockSpec((TN_IN, D_MODEL), lambda i, j: (qv_blk(j), 0)),
            pl.BlockSpec((kv_rows + FOX_HEADS, D_MODEL), lambda i, j: (0, 0)),
            pl.BlockSpec((FOX_HEADS, 1), lambda i, j: (0, 0)),
        ],
        out_specs=[
            pl.BlockSpec((TM_IN, TN_IN), lambda i, j: (i, tok_blk(j))),
            pl.BlockSpec((None, FOX_HEADS, TM_IN), lambda i, j: (i // nsb, 0, i % nsb)),
            pl.BlockSpec((None, TN_IN, TM_IN), lambda i, j: (i // nsb, qv_blk(j), i % nsb)),
            pl.BlockSpec((None, kv_rows, TM_IN), lambda i, j: (i // nsb, 0, i % nsb)),
        ],
        out_shape=[
            jax.ShapeDtypeStruct((t, D_PROJ), BF16),
            jax.ShapeDtypeStruct((batch, FOX_HEADS, seq), F32),
            jax.ShapeDtypeStruct((batch, D_QV, seq), BF16),
            jax.ShapeDtypeStruct((batch, kv_rows, seq), BF16),
        ],
        scratch_shapes=[
            pltpu.VMEM((TM_IN, D_MODEL), BF16),
            pltpu.VMEM((TM_IN, D_MODEL), BF16),
            pltpu.VMEM((HALF, TM_IN), F32),
            pltpu.VMEM((HALF, TM_IN), F32),
        ],
        compiler_params=pltpu.CompilerParams(
            dimension_semantics=("arbitrary", "arbitrary"),
            vmem_limit_bytes=57 * MIB),
        name="inproj",
    )(pos3, invf, x2, gain, w_tok, w_qv, w_kv, fbias)


def _decay_kernel(logf_ref, aug_ref, base_ref, carry_sc):
    sb = pl.program_id(1)

    @pl.when(sb == 0)
    def _():
        carry_sc[...] = jnp.zeros(carry_sc.shape, F32)

    row = lax.broadcasted_iota(jnp.int32, (TK_FOX, TK_FOX), 0)
    col = lax.broadcasted_iota(jnp.int32, (TK_FOX, TK_FOX), 1)
    upper = jnp.where(row <= col, 1.0, 0.0).astype(BF16)
    pad = jnp.zeros((LANES - 3 * FOX_HEADS, TK_FOX), F32)
    for c in range(DECAY_BLOCKS):
        toks = slice(c * TK_FOX, (c + 1) * TK_FOX)
        hi, mid, lo = _split3(logf_ref[:, toks] * LOG2E)
        cl = (jnp.dot(hi, upper, preferred_element_type=F32)
              + jnp.dot(mid, upper, preferred_element_type=F32)
              + jnp.dot(lo, upper, preferred_element_type=F32))
        yh, ym, yl = _split3(-cl)
        pieces_t = jnp.concatenate([yh.astype(F32), ym.astype(F32), yl.astype(F32), pad], axis=0)
        aug_ref[toks, :] = pieces_t.T.astype(BF16)
        base_ref[:, c:c + 1] = -carry_sc[...]
        carry_sc[...] = carry_sc[...] + cl[:, TK_FOX - 1:TK_FOX]


def _decay(logf_t, batch, seq):
    toks = DECAY_BLOCKS * TK_FOX
    ns = seq // toks
    return pl.pallas_call(
        _decay_kernel,
        grid=(batch, ns),
        in_specs=[pl.BlockSpec((None, FOX_HEADS, toks), lambda b, s: (b, 0, s))],
        out_specs=[
            pl.BlockSpec((toks, LANES), lambda b, s: (b * ns + s, 0)),
            pl.BlockSpec((None, None, FOX_HEADS, DECAY_BLOCKS), lambda b, s: (b, s, 0, 0)),
        ],
        out_shape=[
            jax.ShapeDtypeStruct((batch * seq, LANES), BF16),
            jax.ShapeDtypeStruct((batch, ns, FOX_HEADS, DECAY_BLOCKS), F32),
        ],
        scratch_shapes=[pltpu.VMEM((FOX_HEADS, 1), F32)],
        compiler_params=pltpu.CompilerParams(
            dimension_semantics=("arbitrary", "arbitrary")),
        name="decay",
    )(logf_t)


def _swa_kernel(sinks_ref, qt_ref, cur_ref, prev_ref, o_ref, ktok_sc):
    i = pl.program_id(1)
    nblk = TQ_SWA // WINDOW
    gw = SWA_GROUP * WINDOW
    kv_win = jnp.concatenate([prev_ref[...], cur_ref[...]], axis=1)
    ktok_sc[...] = kv_win[0:SWA_KV_W, :].astype(F32).T.astype(BF16)

    row = lax.broadcasted_iota(jnp.int32, (2 * WINDOW, gw), 0)
    qix = lax.broadcasted_iota(jnp.int32, (2 * WINDOW, gw), 1) % WINDOW
    allowed = (row > qix) & (row <= qix + WINDOW)
    bias_any = jnp.where(allowed, 0.0, NEG)
    bias_first = jnp.where(allowed & (row >= WINDOW), 0.0, NEG)
    bias_r0 = jnp.where(i == 0, bias_first, bias_any)
    zeros_q = jnp.zeros((HEAD_DIM, WINDOW), BF16)
    ones_rows = jnp.ones((SWA_DEN_ROWS, 2 * WINDOW), BF16)

    units = [(r, g) for r in range(nblk) for g in range(SWA_KV_HEADS)]

    def scores(unit):
        r, g = unit
        cols = slice(r * WINDOW, (r + 1) * WINDOW)
        tiles = []
        for hq in range(SWA_GROUP):
            h = g * SWA_GROUP + hq
            q_h = qt_ref[h * HEAD_DIM:(h + 1) * HEAD_DIM, cols]
            tiles.append(jnp.concatenate([q_h, zeros_q] if g == 0 else [zeros_q, q_h], axis=0))
        q_g = jnp.concatenate(tiles, axis=1)
        k_win = ktok_sc[r * WINDOW:(r + 2) * WINDOW, :]
        s = jnp.dot(k_win, q_g, preferred_element_type=F32)
        return s + (bias_r0 if r == 0 else bias_any)

    def finish(s, unit):
        r, g = unit
        sink = jnp.concatenate(
            [jnp.full((1, WINDOW), sinks_ref[g * SWA_GROUP + hq] * LOG2E, F32)
             for hq in range(SWA_GROUP)], axis=1)
        m = jnp.maximum(jnp.max(s, axis=0, keepdims=True), sink)
        pr = jnp.exp2(s - m).astype(BF16)
        v_rows = slice(SWA_KV_W + g * HEAD_DIM, SWA_KV_W + (g + 1) * HEAD_DIM)
        vt = jnp.concatenate([kv_win[v_rows, r * WINDOW:(r + 2) * WINDOW], ones_rows], axis=0)
        acc = jnp.dot(vt, pr, preferred_element_type=F32)
        den = acc[HEAD_DIM:HEAD_DIM + 1, :] + jnp.exp2(sink - m)
        o_t = acc[0:HEAD_DIM, :] * (1.0 / den)
        for pp in range(SWA_GROUP // 2):
            pair = jnp.concatenate([o_t[:, (2 * pp) * WINDOW:(2 * pp + 1) * WINDOW],
                                    o_t[:, (2 * pp + 1) * WINDOW:(2 * pp + 2) * WINDOW]], axis=0)
            c0 = (g * SWA_GROUP + 2 * pp) * HEAD_DIM
            o_ref[r * WINDOW:(r + 1) * WINDOW, c0:c0 + LANES] = pair.T.astype(BF16)

    pending = scores(units[0])
    for n, unit in enumerate(units):
        nxt = scores(units[n + 1]) if n + 1 < len(units) else None
        finish(pending, unit)
        pending = nxt


def _swa(qv_t, kv_t, sinks, batch, seq):
    nsq = seq // TQ_SWA
    per = TQ_SWA // WINDOW
    return pl.pallas_call(
        _swa_kernel,
        grid=(batch, nsq),
        in_specs=[
            pl.BlockSpec(memory_space=pltpu.SMEM),
            pl.BlockSpec((None, SWA_Q_W, TQ_SWA), lambda b, i: (b, ROW_AQ // SWA_Q_W, i)),
            pl.BlockSpec((None, 2 * SWA_KV_W, TQ_SWA), lambda b, i: (b, 0, i)),
            pl.BlockSpec((None, 2 * SWA_KV_W, WINDOW),
                         lambda b, i: (b, 0, jnp.maximum(i * per - 1, 0))),
        ],
        out_specs=pl.BlockSpec((TQ_SWA, SWA_Q_W), lambda b, i: (b * nsq + i, 0)),
        scratch_shapes=[pltpu.VMEM((WINDOW + TQ_SWA, SWA_KV_W), BF16)],
        out_shape=jax.ShapeDtypeStruct((batch * seq, SWA_Q_W), BF16),
        compiler_params=pltpu.CompilerParams(
            dimension_semantics=("arbitrary", "arbitrary")),
        name="swa",
    )(sinks, qv_t, kv_t, kv_t)


def _fox_kernel(base_ref, q_ref, k_ref, aug_ref, vt_ref, o_ref,
                m_sc, ml_sc, al_sc, acc_sc, qt_sc, bm_sc, *bufs):
    b = pl.program_id(0)
    p = pl.program_id(1)
    qi = pl.program_id(2)
    s_bufs, p_bufs = bufs[:FOX_SLOTS], bufs[FOX_SLOTS:]
    dz = pl.multiple_of(jnp.minimum(qi, 0), FOX_ROWS)
    nb = k_ref.shape[0] // TK_FOX
    per = TQ_FOX // TK_FOX
    q_t = q_ref[...]
    feat = lax.broadcasted_iota(jnp.int32, (LANES, TQ_FOX), 0)
    zero = jnp.zeros_like(q_t)
    for hh in range(2):
        h = 2 * p + hh
        sel = (feat == h) | (feat == h + FOX_HEADS) | (feat == h + 2 * FOX_HEADS)
        qt_sc[hh, 0:LANES, :] = jnp.where((feat // HEAD_DIM) == hh, q_t, zero)
        qt_sc[hh, LANES:2 * LANES, :] = jnp.where(sel, 1.0, 0.0).astype(BF16)

    m_sc[...] = jnp.full(m_sc.shape, NEG, F32)
    acc_sc[...] = jnp.zeros(acc_sc.shape, F32)
    ones_rows = jnp.ones((FOX_DEN_ROWS, TK_FOX), BF16)

    def pieces(q_lo):
        step = TQ_FOX // FOX_PIECES
        return [(max(q_lo, u * step), (u + 1) * step) for u in range(FOX_PIECES)
                if (u + 1) * step > q_lo]

    def key_rows(blk, n):
        return pl.ds(pl.multiple_of(blk * TQ_FOX + (n // 2) * TK_FOX, TK_FOX), TK_FOX)

    def scores(blk, n, q_lo, diag, lo, hi):
        hh, slot = n % 2, n % FOX_SLOTS
        ks = key_rows(blk, n)
        k_full = jnp.concatenate([k_ref[ks, :], aug_ref[ks, :]], axis=1)
        s = jnp.dot(k_full, qt_sc[hh, :, lo:hi], preferred_element_type=F32)
        if diag is not None:
            row = lax.broadcasted_iota(jnp.int32, s.shape, 0)
            col = lax.broadcasted_iota(jnp.int32, s.shape, 1)
            limit = lo - q_lo if diag is True else jnp.where(diag, lo - q_lo, TK_FOX)
            s = jnp.where(row - col <= limit, s, NEG)
        s_bufs[slot][pl.ds(dz, TK_FOX), lo:hi] = s
        bm_sc[slot, :, lo:hi] = jnp.max(s, axis=0, keepdims=True)

    def stats(blk, n, q_lo):
        hh, slot = n % 2, n % FOX_SLOTS
        base = base_ref[(b * nb + blk * per + n // 2) * FOX_HEADS + 2 * p + hh]
        m_prev = m_sc[hh, :, q_lo:] - base
        m_new = jnp.maximum(m_prev, bm_sc[slot, :, q_lo:])
        al_sc[hh, :, q_lo:] = jnp.exp2(m_prev - m_new)
        ml_sc[hh, :, q_lo:] = m_new
        m_sc[hh, :, q_lo:] = m_new + base

    def probs(n, lo, hi):
        hh, slot, p_slot = n % 2, n % FOX_SLOTS, n % 2
        m_new = ml_sc[hh, :, lo:hi]
        for r in range(0, TK_FOX, FOX_ROWS):
            rows = pl.ds(dz + r, FOX_ROWS)
            pr = jnp.exp2(s_bufs[slot][rows, lo:hi] - m_new)
            p_bufs[p_slot][rows, lo:hi] = pr.astype(BF16)

    def pv(blk, n, lo, hi):
        hh, p_slot = n % 2, n % 2
        vt_h = jnp.concatenate(
            [vt_ref[hh * HEAD_DIM:(hh + 1) * HEAD_DIM, key_rows(blk, n)], ones_rows], axis=0)
        acc_sc[hh, :, lo:hi] = al_sc[hh, :, lo:hi] * acc_sc[hh, :, lo:hi] + jnp.dot(
            vt_h, p_bufs[p_slot][pl.ds(dz, TK_FOX), lo:hi], preferred_element_type=F32)

    def chain_step(blk, n, q_lo, ahead):
        stats(blk, n, q_lo)
        mine = pieces(q_lo)
        theirs = ahead[1] if ahead is not None else []
        for u in range(max(len(mine), len(theirs))):
            if u < len(mine):
                probs(n, *mine[u])
            if u < len(theirs):
                ahead[0](*theirs[u])
            if u < len(mine):
                pv(blk, n, *mine[u])

    def ahead_of(blk, n, q_lo, diag):
        return (functools.partial(scores, blk, n, q_lo, diag), pieces(q_lo))

    for n in range(FOX_AHEAD):
        for lo, hi in pieces(0):
            scores(0, n, 0, qi == 0, lo, hi)

    def body(j, carry):
        for n in range(FOX_CHAINS):
            m = n + FOX_AHEAD
            if m < FOX_CHAINS:
                ahead = ahead_of(j, m, 0, None)
            else:
                ahead = ahead_of(j + 1, m - FOX_CHAINS, 0, j + 1 == qi)
            chain_step(j, n, 0, ahead)
        return carry

    lax.fori_loop(0, qi, body, 0)
    for n in range(FOX_CHAINS):
        m = n + FOX_AHEAD
        ahead = ahead_of(qi, m, (m // 2) * TK_FOX, True) if m < FOX_CHAINS else None
        chain_step(qi, n, (n // 2) * TK_FOX, ahead)

    o_t = jnp.concatenate(
        [acc_sc[hh, 0:HEAD_DIM, :] * (1.0 / acc_sc[hh, HEAD_DIM:HEAD_DIM + 1, :]) for hh in range(2)],
        axis=0)
    o_ref[...] = o_t.T.astype(BF16)


def _fox(proj, aug, base, qv_t, batch, seq):
    t = proj.shape[0]
    nq = seq // TQ_FOX
    pairs = FOX_HEADS // 2
    return pl.pallas_call(
        _fox_kernel,
        grid=(batch, pairs, nq),
        in_specs=[
            pl.BlockSpec(memory_space=pltpu.SMEM),
            pl.BlockSpec((None, LANES, TQ_FOX), lambda b, p, i: (b, ROW_FQ // LANES + p, i)),
            pl.BlockSpec((seq, LANES), lambda b, p, i: (b, COL_FK // LANES + p)),
            pl.BlockSpec((seq, LANES), lambda b, p, i: (b, 0)),
            pl.BlockSpec((None, LANES, seq), lambda b, p, i: (b, ROW_FV // LANES + p, 0)),
        ],
        out_specs=pl.BlockSpec((TQ_FOX, LANES), lambda b, p, i: (b * nq + i, p)),
        out_shape=jax.ShapeDtypeStruct((t, FOX_W), BF16),
        scratch_shapes=[
            pltpu.VMEM((2, 1, TQ_FOX), F32),
            pltpu.VMEM((2, 1, TQ_FOX), F32),
            pltpu.VMEM((2, 1, TQ_FOX), F32),
            pltpu.VMEM((2, HEAD_DIM + FOX_DEN_ROWS, TQ_FOX), F32),
            pltpu.VMEM((2, 2 * LANES, TQ_FOX), BF16),
            pltpu.VMEM((FOX_SLOTS, 1, TQ_FOX), F32),
        ] + [pltpu.VMEM((TK_FOX, TQ_FOX), F32)] * FOX_SLOTS + [
            pltpu.VMEM((TK_FOX, TQ_FOX), BF16),
            pltpu.VMEM((TK_FOX, TQ_FOX), BF16),
        ],
        compiler_params=pltpu.CompilerParams(
            dimension_semantics=("arbitrary", "arbitrary", "arbitrary")),
        name="fox",
    )(base, qv_t, proj, aug, qv_t)


def _merge_kernel(oa_ref, ob_ref, ga_ref, gb_ref, x_ref, wa_ref, wb_ref, wo_ref, gain_ref,
                  x1_ref, h2_ref):
    ya = jnp.dot(oa_ref[...], wa_ref[...], preferred_element_type=F32)
    yb = jnp.dot(ob_ref[...], wb_ref[...], preferred_element_type=F32)

    def gate(ref):
        return 1.0 / (1.0 + jnp.exp(-ref[...].astype(F32)))

    merged = gate(ga_ref) * ya + gate(gb_ref) * yb
    x1 = x_ref[...] + jnp.dot(merged.astype(BF16), wo_ref[...], preferred_element_type=F32)
    x1_ref[...] = x1
    h2_ref[...] = _rmsnorm_rows(x1, gain_ref[...]).astype(BF16)


def _merge(o_a, o_b, proj, x2, wa, wb, wo, gain):
    t = x2.shape[0]
    tm = TM_MERGE
    const = dict(pipeline_mode=pl.Buffered(1))
    return pl.pallas_call(
        _merge_kernel,
        grid=(t // tm,),
        in_specs=[
            pl.BlockSpec((tm, SWA_Q_W), lambda i: (i, 0)),
            pl.BlockSpec((tm, FOX_W), lambda i: (i, 0)),
            pl.BlockSpec((tm, D_MODEL), lambda i: (i, COL_GA // D_MODEL)),
            pl.BlockSpec((tm, D_MODEL), lambda i: (i, COL_GB // D_MODEL)),
            pl.BlockSpec((tm, D_MODEL), lambda i: (i, 0)),
            pl.BlockSpec((SWA_Q_W, D_MODEL), lambda i: (0, 0), **const),
            pl.BlockSpec((FOX_W, D_MODEL), lambda i: (0, 0), **const),
            pl.BlockSpec((D_MODEL, D_MODEL), lambda i: (0, 0), **const),
            pl.BlockSpec((1, D_MODEL), lambda i: (0, 0), **const),
        ],
        out_specs=[
            pl.BlockSpec((tm, D_MODEL), lambda i: (i, 0)),
            pl.BlockSpec((tm, D_MODEL), lambda i: (i, 0)),
        ],
        out_shape=[
            jax.ShapeDtypeStruct((t, D_MODEL), F32),
            jax.ShapeDtypeStruct((t, D_MODEL), BF16),
        ],
        compiler_params=pltpu.CompilerParams(
            dimension_semantics=("arbitrary",),
            vmem_limit_bytes=56 * MIB),
        name="merge",
    )(o_a, o_b, proj, proj, x2, wa, wb, wo, gain)


def _mlp_kernel(h2_ref, x1_ref, wup_ref, wdn_ref, gain_ref, o_ref, *, final_norm):
    f = pl.program_id(1)
    last = pl.num_programs(1) - 1

    def delta():
        u = jnp.maximum(jnp.dot(h2_ref[...], wup_ref[...], preferred_element_type=F32), 0.0)
        return jnp.dot((u * u).astype(BF16), wdn_ref[...], preferred_element_type=F32)

    @pl.when(f == 0)
    def _():
        o_ref[...] = x1_ref[...] + delta()

    @pl.when((f > 0) & (f < last))
    def _():
        o_ref[...] += delta()

    @pl.when(f == last)
    def _():
        y = o_ref[...] + delta()
        o_ref[...] = _rmsnorm_rows(y, gain_ref[...]) if final_norm else y


def _mlp(h2, x1, wup, wdn, gain, final_norm):
    t = x1.shape[0]
    tm, tf = TM_MLP, TF_MLP
    return pl.pallas_call(
        functools.partial(_mlp_kernel, final_norm=final_norm),
        grid=(t // tm, D_FF // tf),
        in_specs=[
            pl.BlockSpec((tm, D_MODEL), lambda i, f: (i, 0)),
            pl.BlockSpec((tm, D_MODEL), lambda i, f: (i, 0)),
            pl.BlockSpec((D_MODEL, tf), lambda i, f: (0, f)),
            pl.BlockSpec((tf, D_MODEL), lambda i, f: (f, 0)),
            pl.BlockSpec((1, D_MODEL), lambda i, f: (0, 0)),
        ],
        out_specs=pl.BlockSpec((tm, D_MODEL), lambda i, f: (i, 0)),
        out_shape=jax.ShapeDtypeStruct((t, D_MODEL), F32),
        compiler_params=pltpu.CompilerParams(
            dimension_semantics=("arbitrary", "arbitrary"),
            vmem_limit_bytes=48 * MIB),
        name="mlp",
    )(h2, x1, wup, wdn, gain)


def _pack_w_in(w, scale):
    o = 0
    a_q = w[:, o:o + SWA_Q_W] * (scale * LOG2E); o += SWA_Q_W
    a_k = w[:, o:o + SWA_KV_W]; o += SWA_KV_W
    a_v = w[:, o:o + SWA_KV_W]; o += SWA_KV_W
    f_q = w[:, o:o + FOX_W] * (scale * LOG2E); o += FOX_W
    f_k = w[:, o:o + FOX_W]; o += FOX_W
    f_v = w[:, o:o + FOX_W]; o += FOX_W
    f_l = w[:, o:o + FOX_HEADS]; o += FOX_HEADS
    g_a = w[:, o:o + D_MODEL]; o += D_MODEL
    g_b = w[:, o:o + D_MODEL]; o += D_MODEL
    w_tok = jnp.concatenate([g_a, g_b, f_k], axis=1).astype(BF16)
    w_qv = jnp.concatenate([a_q, f_v, f_q], axis=1).T.astype(BF16)
    w_kv = jnp.concatenate([a_k, a_v, f_l], axis=1).T.astype(BF16)
    return w_tok, w_qv, w_kv


def kernel(x, positions, attn_norm, w_in, fox_f_bias, swa_sinks, w_branch_swa, w_branch_fox,
           w_out, mlp_norm, w_up, w_down, final_norm):
    batch, seq, d = x.shape
    depth = w_in.shape[0]
    assert d == D_MODEL and seq % TM_IN == 0 and seq % TQ_FOX == 0
    t = batch * seq
    scale = HEAD_DIM ** -0.5
    inv_freq = ROPE_THETA ** (-jnp.arange(0, HEAD_DIM, 2, dtype=F32) / HEAD_DIM)
    invf = inv_freq[:, None]
    pos3 = positions.reshape(t // TM_IN, 1, TM_IN)
    x2 = x.reshape(t, d)
    for l in range(depth):
        w_tok, w_qv, w_kv = _pack_w_in(w_in[l], scale)
        fbias = fox_f_bias[l].astype(F32)[:, None]
        proj, logf_t, qv_t, kv_t = _inproj(x2, pos3, invf, attn_norm[l][None, :].astype(F32),
                                           w_tok, w_qv, w_kv, fbias, batch, seq)
        aug, base = _decay(logf_t, batch, seq)
        base = base.transpose(0, 1, 3, 2).reshape(-1)
        o_a = _swa(qv_t, kv_t, swa_sinks[l].astype(F32), batch, seq)
        o_b = _fox(proj, aug, base, qv_t, batch, seq)
        x1, h2 = _merge(o_a, o_b, proj, x2,
                        w_branch_swa[l].astype(BF16), w_branch_fox[l].astype(BF16),
                        w_out[l].astype(BF16), mlp_norm[l][None, :].astype(F32))
        x2 = _mlp(h2, x1, w_up[l].astype(BF16), w_down[l].astype(BF16),
                  final_norm[None, :].astype(F32), final_norm=(l == depth - 1))
    return x2.reshape(batch, seq, d)
```

```python
import functools

import jax
import jax.numpy as jnp
from jax import lax
from jax.experimental import pallas as pl
from jax.experimental.pallas import tpu as pltpu

F32 = jnp.float32
BF16 = jnp.bfloat16

D_MODEL = 2048
HEAD_DIM = 64
HALF = HEAD_DIM // 2
SWA_Q_HEADS = 16
SWA_KV_HEADS = 2
SWA_GROUP = SWA_Q_HEADS // SWA_KV_HEADS
WINDOW = 128
FOX_HEADS = 16
D_FF = 4 * D_MODEL
ROPE_THETA = 10000.0
RMS_EPS = 1e-6
SWA_Q_W = SWA_Q_HEADS * HEAD_DIM
FOX_W = FOX_HEADS * HEAD_DIM

LANES = 128
NEG = -1e30
MIB = 1024 * 1024
LOG2E = 1.4426950408889634

COL_GA = 0
COL_GB = 2048
COL_FK = 4096
D_PROJ = 5120
ROW_AQ = 0
ROW_FV = SWA_Q_W
ROW_FQ = SWA_Q_W + FOX_W
D_QV = ROW_FQ + FOX_W
SWA_KV_W = SWA_KV_HEADS * HEAD_DIM

TM_IN = 1024
TN_IN = 1024
NJ_TOK = D_PROJ // TN_IN
J_AQ = NJ_TOK
J_KV = J_AQ + SWA_Q_W // TN_IN
J_FEAT = J_KV + 1
NJ_IN = J_FEAT + (D_QV - ROW_FV) // TN_IN
assert 2 * (NJ_TOK - 1) >= TM_IN // LANES

TQ_SWA = 512
TQ_FOX = 1024
TK_FOX = 256
FOX_AHEAD = 2
FOX_CHAINS = 2 * (TQ_FOX // TK_FOX)
FOX_SLOTS = 4
assert FOX_CHAINS % FOX_SLOTS == 0 and FOX_AHEAD == 2 and FOX_AHEAD < FOX_SLOTS
FOX_ACC_ROWS = HEAD_DIM + 8
FOX_MAX_JUMP = 100.0
FOX_ROWS = 32
FOX_PIECES = 2
FOX_DEN_ROWS = 64
SWA_DEN_ROWS = 16
DECAY_BLOCKS = 4
TM_MERGE = 512
TM_MLP = 512
TF_MLP = 1024


def _rmsnorm_rows(x, gain):
    ms = jnp.mean(x * x, axis=-1, keepdims=True)
    return x * lax.rsqrt(ms + RMS_EPS) * gain


def _split3(x):
    hi = x.astype(BF16)
    r1 = x - hi.astype(F32)
    mid = r1.astype(BF16)
    lo = (r1 - mid.astype(F32)).astype(BF16)
    return hi, mid, lo


def _inproj_kernel(pos_ref, invf_ref, x_ref, gain_ref, w_ref, wqv_ref, wkv_ref, fbias_ref,
                   proj_ref, logf_ref, qv_ref, kv_ref, h_sc, hn_sc, cos_sc, sin_sc):
    i = pl.program_id(0)
    j = pl.program_id(1)
    nchunk = TM_IN // LANES

    def norm_chunk(dst, c):
        rows = pl.ds(pl.multiple_of(c * LANES, LANES), LANES)
        dst[rows, :] = _rmsnorm_rows(x_ref[rows, :], gain_ref[...]).astype(BF16)

    @pl.when(j == 0)
    def _():
        @pl.when(i == 0)
        def _():
            def body(r, _):
                norm_chunk(h_sc, r)
                return 0
            lax.fori_loop(0, nchunk, body, 0)

        @pl.when(i > 0)
        def _():
            h_sc[...] = hn_sc[...]

        ang = invf_ref[...] * pos_ref[...].astype(F32)
        cos_sc[...] = jnp.cos(ang)
        sin_sc[...] = jnp.sin(ang)

    def features(wt_ref):
        return lax.dot_general(wt_ref[...], h_sc[...], (((1,), (1,)), ((), ())),
                               preferred_element_type=F32)

    def rope_rows(acc, heads, out_ref):
        for hd in range(heads):
            r0 = hd * HEAD_DIM
            t1 = acc[r0:r0 + HALF, :]
            t2 = acc[r0 + HALF:r0 + HEAD_DIM, :]
            out_ref[r0:r0 + HALF, :] = (t1 * cos_sc[...] - t2 * sin_sc[...]).astype(BF16)
            out_ref[r0 + HALF:r0 + HEAD_DIM, :] = (t2 * cos_sc[...] + t1 * sin_sc[...]).astype(BF16)

    @pl.when(j < NJ_TOK)
    def _():
        proj_ref[...] = jnp.dot(h_sc[...], w_ref[...], preferred_element_type=F32).astype(BF16)
        c0 = jnp.clip(2 * (j - (i == 0).astype(jnp.int32)), 0, nchunk - 2)
        norm_chunk(hn_sc, c0)
        norm_chunk(hn_sc, c0 + 1)

    @pl.when((j >= J_AQ) & (j < J_KV))
    def _():
        rope_rows(features(wqv_ref), TN_IN // HEAD_DIM, qv_ref)

    @pl.when(j == J_KV)
    def _():
        acc = features(wkv_ref)
        rope_rows(acc, SWA_KV_HEADS, kv_ref)
        kv_ref[SWA_KV_W:2 * SWA_KV_W, :] = acc[SWA_KV_W:2 * SWA_KV_W, :].astype(BF16)
        z = acc[2 * SWA_KV_W:, :] + fbias_ref[...]
        logf_ref[...] = jnp.minimum(z, 0.0) - jnp.log1p(jnp.exp(-jnp.abs(z)))

    @pl.when(j >= J_FEAT)
    def _():
        qv_ref[...] = features(wqv_ref).astype(BF16)


def _inproj(x2, pos3, invf, gain, w_tok, w_qv, w_kv, fbias, batch, seq):
    t = x2.shape[0]
    nsb = seq // TM_IN
    kv_rows = 2 * SWA_KV_W

    def qv_blk(j):
        return jnp.where(j >= J_FEAT, j - J_FEAT + ROW_FV // TN_IN,
                         jnp.clip(j - J_AQ, 0, SWA_Q_W // TN_IN - 1))

    def tok_blk(j):
        return jnp.minimum(j, NJ_TOK - 1)

    def x_blk(i, j):
        return jnp.where((i == 0) & (j == 0), 0, jnp.minimum(i + 1, t // TM_IN - 1))

    return pl.pallas_call(
        _inproj_kernel,
        grid=(t // TM_IN, NJ_IN),
        in_specs=[
            pl.BlockSpec((None, 1, TM_IN), lambda i, j: (i, 0, 0)),
            pl.BlockSpec((HALF, 1), lambda i, j: (0, 0)),
            pl.BlockSpec((TM_IN, D_MODEL), lambda i, j: (x_blk(i, j), 0)),
            pl.BlockSpec((1, D_MODEL), lambda i, j: (0, 0)),
            pl.BlockSpec((D_MODEL, TN_IN), lambda i, j: (0, tok_blk(j))),
            pl.BlockSpec((TN_IN, D_MODEL), lambda i, j: (qv_blk(j), 0)),
            pl.BlockSpec((kv_rows + FOX_HEADS, D_MODEL), lambda i, j: (0, 0)),
            pl.BlockSpec((FOX_HEADS, 1), lambda i, j: (0, 0)),
        ],
        out_specs=[
            pl.BlockSpec((TM_IN, TN_IN), lambda i, j: (i, tok_blk(j))),
            pl.BlockSpec((None, FOX_HEADS, TM_IN), lambda i, j: (i // nsb, 0, i % nsb)),
            pl.BlockSpec((None, TN_IN, TM_IN), lambda i, j: (i // nsb, qv_blk(j), i % nsb)),
            pl.BlockSpec((None, kv_rows, TM_IN), lambda i, j: (i // nsb, 0, i % nsb)),
        ],
        out_shape=[
            jax.ShapeDtypeStruct((t, D_PROJ), BF16),
            jax.ShapeDtypeStruct((batch, FOX_HEADS, seq), F32),
            jax.ShapeDtypeStruct((batch, D_QV, seq), BF16),
            jax.ShapeDtypeStruct((batch, kv_rows, seq), BF16),
        ],
        scratch_shapes=[
            pltpu.VMEM((TM_IN, D_MODEL), BF16),
            pltpu.VMEM((TM_IN, D_MODEL), BF16),
            pltpu.VMEM((HALF, TM_IN), F32),
            pltpu.VMEM((HALF, TM_IN), F32),
        ],
        compiler_params=pltpu.CompilerParams(
            dimension_semantics=("arbitrary", "arbitrary"),
            vmem_limit_bytes=57 * MIB),
        name="inproj",
    )(pos3, invf, x2, gain, w_tok, w_qv, w_kv, fbias)


def _decay_kernel(logf_ref, aug_ref, base_ref, dec_ref, carry_sc):
    sb = pl.program_id(1)

    @pl.when(sb == 0)
    def _():
        carry_sc[...] = jnp.zeros(carry_sc.shape, F32)

    row = lax.broadcasted_iota(jnp.int32, (TK_FOX, TK_FOX), 0)
    col = lax.broadcasted_iota(jnp.int32, (TK_FOX, TK_FOX), 1)
    upper = jnp.where(row <= col, 1.0, 0.0).astype(BF16)
    pad = jnp.zeros((LANES - 3 * FOX_HEADS, TK_FOX), F32)
    for c in range(DECAY_BLOCKS):
        toks = slice(c * TK_FOX, (c + 1) * TK_FOX)
        hi, mid, lo = _split3(logf_ref[:, toks] * LOG2E)
        cl = (jnp.dot(hi, upper, preferred_element_type=F32)
              + jnp.dot(mid, upper, preferred_element_type=F32)
              + jnp.dot(lo, upper, preferred_element_type=F32))
        yh, ym, yl = _split3(-cl)
        pieces_t = jnp.concatenate([yh.astype(F32), ym.astype(F32), yl.astype(F32), pad], axis=0)
        aug_ref[toks, :] = pieces_t.T.astype(BF16)
        dec_ref[:, toks] = -cl
        total = cl[:, TK_FOX - 1:TK_FOX]
        base_ref[:, c:c + 1] = -carry_sc[...]
        base_ref[:, DECAY_BLOCKS + c:DECAY_BLOCKS + c + 1] = -total
        carry_sc[...] = carry_sc[...] + total


def _decay(logf_t, batch, seq):
    toks = DECAY_BLOCKS * TK_FOX
    ns = seq // toks
    return pl.pallas_call(
        _decay_kernel,
        grid=(batch, ns),
        in_specs=[pl.BlockSpec((None, FOX_HEADS, toks), lambda b, s: (b, 0, s))],
        out_specs=[
            pl.BlockSpec((toks, LANES), lambda b, s: (b * ns + s, 0)),
            pl.BlockSpec((None, None, FOX_HEADS, 2 * DECAY_BLOCKS), lambda b, s: (b, s, 0, 0)),
            pl.BlockSpec((None, FOX_HEADS, toks), lambda b, s: (b, 0, s)),
        ],
        out_shape=[
            jax.ShapeDtypeStruct((batch * seq, LANES), BF16),
            jax.ShapeDtypeStruct((batch, ns, FOX_HEADS, 2 * DECAY_BLOCKS), F32),
            jax.ShapeDtypeStruct((batch, FOX_HEADS, seq), F32),
        ],
        scratch_shapes=[pltpu.VMEM((FOX_HEADS, 1), F32)],
        compiler_params=pltpu.CompilerParams(
            dimension_semantics=("arbitrary", "arbitrary")),
        name="decay",
    )(logf_t)


def _swa_kernel(sinks_ref, qt_ref, cur_ref, prev_ref, o_ref, ktok_sc):
    i = pl.program_id(1)
    nblk = TQ_SWA // WINDOW
    gw = SWA_GROUP * WINDOW
    kv_win = jnp.concatenate([prev_ref[...], cur_ref[...]], axis=1)
    ktok_sc[...] = kv_win[0:SWA_KV_W, :].astype(F32).T.astype(BF16)

    row = lax.broadcasted_iota(jnp.int32, (2 * WINDOW, gw), 0)
    qix = lax.broadcasted_iota(jnp.int32, (2 * WINDOW, gw), 1) % WINDOW
    allowed = (row > qix) & (row <= qix + WINDOW)
    bias_any = jnp.where(allowed, 0.0, NEG)
    bias_first = jnp.where(allowed & (row >= WINDOW), 0.0, NEG)
    bias_r0 = jnp.where(i == 0, bias_first, bias_any)
    zeros_q = jnp.zeros((HEAD_DIM, WINDOW), BF16)
    ones_rows = jnp.ones((SWA_DEN_ROWS, 2 * WINDOW), BF16)

    units = [(r, g) for r in range(nblk) for g in range(SWA_KV_HEADS)]

    def scores(unit):
        r, g = unit
        cols = slice(r * WINDOW, (r + 1) * WINDOW)
        tiles = []
        for hq in range(SWA_GROUP):
            h = g * SWA_GROUP + hq
            q_h = qt_ref[h * HEAD_DIM:(h + 1) * HEAD_DIM, cols]
            tiles.append(jnp.concatenate([q_h, zeros_q] if g == 0 else [zeros_q, q_h], axis=0))
        q_g = jnp.concatenate(tiles, axis=1)
        k_win = ktok_sc[r * WINDOW:(r + 2) * WINDOW, :]
        s = jnp.dot(k_win, q_g, preferred_element_type=F32)
        return s + (bias_r0 if r == 0 else bias_any)

    def finish(s, unit):
        r, g = unit
        sink = jnp.concatenate(
            [jnp.full((1, WINDOW), sinks_ref[g * SWA_GROUP + hq] * LOG2E, F32)
             for hq in range(SWA_GROUP)], axis=1)
        m = jnp.maximum(jnp.max(s, axis=0, keepdims=True), sink)
        pr = jnp.exp2(s - m).astype(BF16)
        v_rows = slice(SWA_KV_W + g * HEAD_DIM, SWA_KV_W + (g + 1) * HEAD_DIM)
        vt = jnp.concatenate([kv_win[v_rows, r * WINDOW:(r + 2) * WINDOW], ones_rows], axis=0)
        acc = jnp.dot(vt, pr, preferred_element_type=F32)
        den = acc[HEAD_DIM:HEAD_DIM + 1, :] + jnp.exp2(sink - m)
        o_t = acc[0:HEAD_DIM, :] * (1.0 / den)
        for pp in range(SWA_GROUP // 2):
            pair = jnp.concatenate([o_t[:, (2 * pp) * WINDOW:(2 * pp + 1) * WINDOW],
                                    o_t[:, (2 * pp + 1) * WINDOW:(2 * pp + 2) * WINDOW]], axis=0)
            c0 = (g * SWA_GROUP + 2 * pp) * HEAD_DIM
            o_ref[r * WINDOW:(r + 1) * WINDOW, c0:c0 + LANES] = pair.T.astype(BF16)

    pending = scores(units[0])
    for n, unit in enumerate(units):
        nxt = scores(units[n + 1]) if n + 1 < len(units) else None
        finish(pending, unit)
        pending = nxt


def _swa(qv_t, kv_t, sinks, batch, seq):
    nsq = seq // TQ_SWA
    per = TQ_SWA // WINDOW
    return pl.pallas_call(
        _swa_kernel,
        grid=(batch, nsq),
        in_specs=[
            pl.BlockSpec(memory_space=pltpu.SMEM),
            pl.BlockSpec((None, SWA_Q_W, TQ_SWA), lambda b, i: (b, ROW_AQ // SWA_Q_W, i)),
            pl.BlockSpec((None, 2 * SWA_KV_W, TQ_SWA), lambda b, i: (b, 0, i)),
            pl.BlockSpec((None, 2 * SWA_KV_W, WINDOW),
                         lambda b, i: (b, 0, jnp.maximum(i * per - 1, 0))),
        ],
        out_specs=pl.BlockSpec((TQ_SWA, SWA_Q_W), lambda b, i: (b * nsq + i, 0)),
        scratch_shapes=[pltpu.VMEM((WINDOW + TQ_SWA, SWA_KV_W), BF16)],
        out_shape=jax.ShapeDtypeStruct((batch * seq, SWA_Q_W), BF16),
        compiler_params=pltpu.CompilerParams(
            dimension_semantics=("arbitrary", "arbitrary")),
        name="swa",
    )(sinks, qv_t, kv_t, kv_t)


def _fox_kernel(base_ref, q_ref, k_ref, aug_ref, vt_ref, dec_ref, o_ref,
                m_sc, al_sc, be_sc, acc_sc, qt_sc, bm_sc, jump_sc, *bufs):
    b = pl.program_id(0)
    p = pl.program_id(1)
    qi = pl.program_id(2)
    s_bufs, p_bufs = bufs[:2], bufs[2:]
    dz = pl.multiple_of(jnp.minimum(qi, 0), FOX_ROWS)
    nb = k_ref.shape[0] // TK_FOX
    per = TQ_FOX // TK_FOX
    q_t = q_ref[...]
    feat = lax.broadcasted_iota(jnp.int32, (LANES, TQ_FOX), 0)
    zero = jnp.zeros_like(q_t)
    for hh in range(2):
        h = 2 * p + hh
        sel = (feat == h) | (feat == h + FOX_HEADS) | (feat == h + 2 * FOX_HEADS)
        qt_sc[hh, 0:LANES, :] = jnp.where((feat // HEAD_DIM) == hh, q_t, zero)
        qt_sc[hh, LANES:2 * LANES, :] = jnp.where(sel, 1.0, 0.0).astype(BF16)

    ones_rows = jnp.ones((FOX_DEN_ROWS, TK_FOX), BF16)

    def reset():
        m_sc[...] = jnp.full(m_sc.shape, NEG, F32)
        acc_sc[...] = jnp.zeros(acc_sc.shape, F32)

    def pieces(q_lo):
        step = TQ_FOX // FOX_PIECES
        return [(max(q_lo, u * step), (u + 1) * step) for u in range(FOX_PIECES)
                if (u + 1) * step > q_lo]

    def key_rows(key0):
        return pl.ds(pl.multiple_of(key0, TK_FOX), TK_FOX)

    def base_of(kblk, hh, span=0):
        return base_ref[((b * nb + kblk) * FOX_HEADS + 2 * p + hh) * 2 + span]

    def qk(key0, hh, lo, hi, limit):
        ks = key_rows(key0)
        k_full = jnp.concatenate([k_ref[ks, :], aug_ref[ks, :]], axis=1)
        s = jnp.dot(k_full, qt_sc[hh, :, lo:hi], preferred_element_type=F32)
        if limit is not None:
            row = lax.broadcasted_iota(jnp.int32, s.shape, 0)
            col = lax.broadcasted_iota(jnp.int32, s.shape, 1)
            s = jnp.where(row - col <= limit, s, NEG)
        return s

    def v_aug(key0, hh):
        return jnp.concatenate(
            [vt_ref[hh * HEAD_DIM:(hh + 1) * HEAD_DIM, key_rows(key0)], ones_rows], axis=0)

    def two_pass_probs(key0, kblk, hh, q_lo, limit_of, sb, pb, one_alpha):
        for lo, hi in pieces(q_lo):
            s = qk(key0, hh, lo, hi, limit_of(lo))
            s_bufs[sb][pl.ds(dz, TK_FOX), lo:hi] = s
            bm_sc[sb, :, lo:hi] = jnp.max(s, axis=0, keepdims=True)
        base = base_of(kblk, hh)
        m_prev = m_sc[hh, :, q_lo:] - base
        m_new = jnp.maximum(m_prev, bm_sc[sb, :, q_lo:])
        al_sc[pb, :, q_lo:] = jnp.ones_like(m_new) if one_alpha else jnp.exp2(m_prev - m_new)
        be_sc[pb, :, q_lo:] = jnp.ones_like(m_new)
        m_sc[hh, :, q_lo:] = m_new + base
        for r in range(0, TK_FOX, FOX_ROWS):
            rows = pl.ds(dz + r, FOX_ROWS)
            pr = jnp.exp2(s_bufs[sb][rows, q_lo:] - m_new)
            p_bufs[pb][rows, q_lo:] = pr.astype(BF16)

    def chain_keys(blk, n):
        return blk * TQ_FOX + (n // 2) * TK_FOX, blk * per + n // 2

    def limit_fn(q_lo, diag):
        if diag is None:
            return lambda lo: None
        if diag is True:
            return lambda lo: lo - q_lo
        return lambda lo: jnp.where(diag, lo - q_lo, TK_FOX)

    def one_pass_piece(blk, n, q_lo, diag, ref, lo, hi):
        key0, _ = chain_keys(blk, n)
        s = qk(key0, n % 2, lo, hi, limit_fn(q_lo, diag)(lo))
        p_bufs[n % FOX_SLOTS][pl.ds(dz, TK_FOX), lo:hi] = jnp.exp2(
            s - ref[:, lo - q_lo:hi - q_lo]).astype(BF16)
        return jnp.max(s, axis=0, keepdims=True)

    def one_pass_finish(n, q_lo, m_prev, ref, base, maxes):
        hh, slot = n % 2, n % FOX_SLOTS
        blk_max = jnp.concatenate(maxes, axis=1) if len(maxes) > 1 else maxes[0]
        m_new = jnp.maximum(m_prev, blk_max)
        al_sc[slot, :, q_lo:] = jnp.exp2(m_prev - m_new)
        be_sc[slot, :, q_lo:] = jnp.exp2(ref - m_new)
        m_sc[hh, :, q_lo:] = m_new + base
        jump_sc[hh, :, q_lo:] = jnp.maximum(jump_sc[hh, :, q_lo:], jnp.abs(blk_max - ref))

    def pv_piece(blk, n, lo, hi):
        hh, slot = n % 2, n % FOX_SLOTS
        key0, _ = chain_keys(blk, n)
        pv = jnp.dot(v_aug(key0, hh), p_bufs[slot][pl.ds(dz, TK_FOX), lo:hi],
                     preferred_element_type=F32)[0:FOX_ACC_ROWS]
        acc_sc[hh, :, lo:hi] = (acc_sc[hh, :, lo:hi] * al_sc[slot, :, lo:hi]
                                + pv * be_sc[slot, :, lo:hi])

    def chain_step(blk, n, q_lo, ahead):
        mine = pieces(q_lo)
        theirs = []
        if ahead is not None:
            a_blk, a_n, a_lo, a_diag = ahead
            theirs = pieces(a_lo)
            a_kblk = chain_keys(a_blk, a_n)[1]
            a_base = base_of(a_kblk, a_n % 2)
            a_prev = m_sc[a_n % 2, :, a_lo:] - a_base
            span = base_of(a_kblk, a_n % 2, span=1)
            if a_diag is None:
                a_ref = a_prev + span
            else:
                reach = dec_ref[a_n % 2:a_n % 2 + 1, key_rows(chain_keys(a_blk, a_n)[0])]
                if TQ_FOX - a_lo > TK_FOX:
                    reach = jnp.concatenate(
                        [reach, jnp.full((1, TQ_FOX - a_lo - TK_FOX), span, F32)], axis=1)
                a_ref = a_prev + (reach if a_diag is True else jnp.where(a_diag, reach, span))
        maxes = []
        for u in range(max(len(mine), len(theirs))):
            if u < len(theirs):
                maxes.append(one_pass_piece(a_blk, a_n, a_lo, a_diag, a_ref, *theirs[u]))
            if u < len(mine):
                pv_piece(blk, n, *mine[u])
        if ahead is not None:
            one_pass_finish(a_n, a_lo, a_prev, a_ref, a_base, maxes)

    def fast_path():
        reset()
        jump_sc[...] = jnp.full(jump_sc.shape, NEG, F32)
        for n in range(FOX_AHEAD):
            key0, kblk = chain_keys(0, n)
            two_pass_probs(key0, kblk, n, 0, limit_fn(0, qi == 0), n, n, True)

        def body(j, carry):
            for n in range(FOX_CHAINS):
                m = n + FOX_AHEAD
                if m < FOX_CHAINS:
                    ahead = (j, m, 0, None)
                else:
                    ahead = (j + 1, m - FOX_CHAINS, 0, j + 1 == qi)
                chain_step(j, n, 0, ahead)
            return carry

        lax.fori_loop(0, qi, body, 0)
        for n in range(FOX_CHAINS):
            m = n + FOX_AHEAD
            ahead = (qi, m, (m // 2) * TK_FOX, True) if m < FOX_CHAINS else None
            chain_step(qi, n, (n // 2) * TK_FOX, ahead)

    def safe_path():
        reset()

        def body(t, carry):
            blk, c = t // per, t % per
            limit = jnp.where(blk == qi, -c * TK_FOX, TK_FOX)
            for hh in range(2):
                two_pass_probs(t * TK_FOX, t, hh, 0, lambda lo: limit + lo, hh, hh, False)
                pv = jnp.dot(v_aug(t * TK_FOX, hh), p_bufs[hh][pl.ds(dz, TK_FOX), :],
                             preferred_element_type=F32)
                acc_sc[hh] = acc_sc[hh] * al_sc[hh] + pv[0:FOX_ACC_ROWS]
            return carry

        lax.fori_loop(0, (qi + 1) * per, body, 0)

    fast_path()

    @pl.when(jnp.max(jump_sc[...]) > FOX_MAX_JUMP)
    def _():
        safe_path()


    o_t = jnp.concatenate(
        [acc_sc[hh, 0:HEAD_DIM, :] * (1.0 / acc_sc[hh, HEAD_DIM:HEAD_DIM + 1, :]) for hh in range(2)],
        axis=0)
    o_ref[...] = o_t.T.astype(BF16)


def _fox(proj, aug, base, dec_t, qv_t, batch, seq):
    t = proj.shape[0]
    nq = seq // TQ_FOX
    pairs = FOX_HEADS // 2
    return pl.pallas_call(
        _fox_kernel,
        grid=(batch, pairs, nq),
        in_specs=[
            pl.BlockSpec(memory_space=pltpu.SMEM),
            pl.BlockSpec((None, LANES, TQ_FOX), lambda b, p, i: (b, ROW_FQ // LANES + p, i)),
            pl.BlockSpec((seq, LANES), lambda b, p, i: (b, COL_FK // LANES + p)),
            pl.BlockSpec((seq, LANES), lambda b, p, i: (b, 0)),
            pl.BlockSpec((None, LANES, seq), lambda b, p, i: (b, ROW_FV // LANES + p, 0)),
            pl.BlockSpec((None, None, 2, seq), lambda b, p, i: (b, p, 0, 0)),
        ],
        out_specs=pl.BlockSpec((TQ_FOX, LANES), lambda b, p, i: (b * nq + i, p)),
        out_shape=jax.ShapeDtypeStruct((t, FOX_W), BF16),
        scratch_shapes=[
            pltpu.VMEM((2, 1, TQ_FOX), F32),
            pltpu.VMEM((FOX_SLOTS, 1, TQ_FOX), F32),
            pltpu.VMEM((FOX_SLOTS, 1, TQ_FOX), F32),
            pltpu.VMEM((2, FOX_ACC_ROWS, TQ_FOX), F32),
            pltpu.VMEM((2, 2 * LANES, TQ_FOX), BF16),
            pltpu.VMEM((2, 1, TQ_FOX), F32),
            pltpu.VMEM((2, 1, TQ_FOX), F32),
        ] + [pltpu.VMEM((TK_FOX, TQ_FOX), F32)] * 2
          + [pltpu.VMEM((TK_FOX, TQ_FOX), BF16)] * FOX_SLOTS,
        compiler_params=pltpu.CompilerParams(
            dimension_semantics=("arbitrary", "arbitrary", "arbitrary")),
        name="fox",
    )(base, qv_t, proj, aug, qv_t, dec_t.reshape(batch, pairs, 2, seq))


def _merge_kernel(oa_ref, ob_ref, ga_ref, gb_ref, x_ref, wa_ref, wb_ref, wo_ref, gain_ref,
                  x1_ref, h2_ref):
    ya = jnp.dot(oa_ref[...], wa_ref[...], preferred_element_type=F32)
    yb = jnp.dot(ob_ref[...], wb_ref[...], preferred_element_type=F32)

    def gate(ref):
        return 1.0 / (1.0 + jnp.exp(-ref[...].astype(F32)))

    merged = gate(ga_ref) * ya + gate(gb_ref) * yb
    x1 = x_ref[...] + jnp.dot(merged.astype(BF16), wo_ref[...], preferred_element_type=F32)
    x1_ref[...] = x1
    h2_ref[...] = _rmsnorm_rows(x1, gain_ref[...]).astype(BF16)


def _merge(o_a, o_b, proj, x2, wa, wb, wo, gain):
    t = x2.shape[0]
    tm = TM_MERGE
    const = dict(pipeline_mode=pl.Buffered(1))
    return pl.pallas_call(
        _merge_kernel,
        grid=(t // tm,),
        in_specs=[
            pl.BlockSpec((tm, SWA_Q_W), lambda i: (i, 0)),
            pl.BlockSpec((tm, FOX_W), lambda i: (i, 0)),
            pl.BlockSpec((tm, D_MODEL), lambda i: (i, COL_GA // D_MODEL)),
            pl.BlockSpec((tm, D_MODEL), lambda i: (i, COL_GB // D_MODEL)),
            pl.BlockSpec((tm, D_MODEL), lambda i: (i, 0)),
            pl.BlockSpec((SWA_Q_W, D_MODEL), lambda i: (0, 0), **const),
            pl.BlockSpec((FOX_W, D_MODEL), lambda i: (0, 0), **const),
            pl.BlockSpec((D_MODEL, D_MODEL), lambda i: (0, 0), **const),
            pl.BlockSpec((1, D_MODEL), lambda i: (0, 0), **const),
        ],
        out_specs=[
            pl.BlockSpec((tm, D_MODEL), lambda i: (i, 0)),
            pl.BlockSpec((tm, D_MODEL), lambda i: (i, 0)),
        ],
        out_shape=[
            jax.ShapeDtypeStruct((t, D_MODEL), F32),
            jax.ShapeDtypeStruct((t, D_MODEL), BF16),
        ],
        compiler_params=pltpu.CompilerParams(
            dimension_semantics=("arbitrary",),
            vmem_limit_bytes=56 * MIB),
        name="merge",
    )(o_a, o_b, proj, proj, x2, wa, wb, wo, gain)


def _mlp_kernel(h2_ref, x1_ref, wup_ref, wdn_ref, gain_ref, o_ref, *, final_norm):
    f = pl.program_id(1)
    last = pl.num_programs(1) - 1

    def delta():
        u = jnp.maximum(jnp.dot(h2_ref[...], wup_ref[...], preferred_element_type=F32), 0.0)
        return jnp.dot((u * u).astype(BF16), wdn_ref[...], preferred_element_type=F32)

    @pl.when(f == 0)
    def _():
        o_ref[...] = x1_ref[...] + delta()

    @pl.when((f > 0) & (f < last))
    def _():
        o_ref[...] += delta()

    @pl.when(f == last)
    def _():
        y = o_ref[...] + delta()
        o_ref[...] = _rmsnorm_rows(y, gain_ref[...]) if final_norm else y


def _mlp(h2, x1, wup, wdn, gain, final_norm):
    t = x1.shape[0]
    tm, tf = TM_MLP, TF_MLP
    return pl.pallas_call(
        functools.partial(_mlp_kernel, final_norm=final_norm),
        grid=(t // tm, D_FF // tf),
        in_specs=[
            pl.BlockSpec((tm, D_MODEL), lambda i, f: (i, 0)),
            pl.BlockSpec((tm, D_MODEL), lambda i, f: (i, 0)),
            pl.BlockSpec((D_MODEL, tf), lambda i, f: (0, f)),
            pl.BlockSpec((tf, D_MODEL), lambda i, f: (f, 0)),
            pl.BlockSpec((1, D_MODEL), lambda i, f: (0, 0)),
        ],
        out_specs=pl.BlockSpec((tm, D_MODEL), lambda i, f: (i, 0)),
        out_shape=jax.ShapeDtypeStruct((t, D_MODEL), F32),
        compiler_params=pltpu.CompilerParams(
            dimension_semantics=("arbitrary", "arbitrary"),
            vmem_limit_bytes=48 * MIB),
        name="mlp",
    )(h2, x1, wup, wdn, gain)


def _pack_w_in(w, scale):
    o = 0
    a_q = w[:, o:o + SWA_Q_W] * (scale * LOG2E); o += SWA_Q_W
    a_k = w[:, o:o + SWA_KV_W]; o += SWA_KV_W
    a_v = w[:, o:o + SWA_KV_W]; o += SWA_KV_W
    f_q = w[:, o:o + FOX_W] * (scale * LOG2E); o += FOX_W
    f_k = w[:, o:o + FOX_W]; o += FOX_W
    f_v = w[:, o:o + FOX_W]; o += FOX_W
    f_l = w[:, o:o + FOX_HEADS]; o += FOX_HEADS
    g_a = w[:, o:o + D_MODEL]; o += D_MODEL
    g_b = w[:, o:o + D_MODEL]; o += D_MODEL
    w_tok = jnp.concatenate([g_a, g_b, f_k], axis=1).astype(BF16)
    w_qv = jnp.concatenate([a_q, f_v, f_q], axis=1).T.astype(BF16)
    w_kv = jnp.concatenate([a_k, a_v, f_l], axis=1).T.astype(BF16)
    return w_tok, w_qv, w_kv


def kernel(x, positions, attn_norm, w_in, fox_f_bias, swa_sinks, w_branch_swa, w_branch_fox,
           w_out, mlp_norm, w_up, w_down, final_norm):
    batch, seq, d = x.shape
    depth = w_in.shape[0]
    assert d == D_MODEL and seq % TM_IN == 0 and seq % TQ_FOX == 0
    t = batch * seq
    scale = HEAD_DIM ** -0.5
    inv_freq = ROPE_THETA ** (-jnp.arange(0, HEAD_DIM, 2, dtype=F32) / HEAD_DIM)
    invf = inv_freq[:, None]
    pos3 = positions.reshape(t // TM_IN, 1, TM_IN)
    x2 = x.reshape(t, d)
    for l in range(depth):
        w_tok, w_qv, w_kv = _pack_w_in(w_in[l], scale)
        fbias = fox_f_bias[l].astype(F32)[:, None]
        proj, logf_t, qv_t, kv_t = _inproj(x2, pos3, invf, attn_norm[l][None, :].astype(F32),
                                           w_tok, w_qv, w_kv, fbias, batch, seq)
        aug, base, dec_t = _decay(logf_t, batch, seq)
        base = base.reshape(batch, -1, FOX_HEADS, 2, DECAY_BLOCKS).transpose(0, 1, 4, 2, 3).reshape(-1)
        o_a = _swa(qv_t, kv_t, swa_sinks[l].astype(F32), batch, seq)
        o_b = _fox(proj, aug, base, dec_t, qv_t, batch, seq)
        x1, h2 = _merge(o_a, o_b, proj, x2,
                        w_branch_swa[l].astype(BF16), w_branch_fox[l].astype(BF16),
                        w_out[l].astype(BF16), mlp_norm[l][None, :].astype(F32))
        x2 = _mlp(h2, x1, w_up[l].astype(BF16), w_down[l].astype(BF16),
                  final_norm[None, :].astype(F32), final_norm=(l == depth - 1))
    return x2.reshape(batch, seq, d)
```

```python
import functools

import jax
import jax.numpy as jnp
from jax import lax
from jax.experimental import pallas as pl
from jax.experimental.pallas import tpu as pltpu

F32 = jnp.float32
BF16 = jnp.bfloat16

D_MODEL = 2048
HEAD_DIM = 64
HALF = HEAD_DIM // 2
SWA_Q_HEADS = 16
SWA_KV_HEADS = 2
SWA_GROUP = SWA_Q_HEADS // SWA_KV_HEADS
WINDOW = 128
FOX_HEADS = 16
D_FF = 4 * D_MODEL
ROPE_THETA = 10000.0
RMS_EPS = 1e-6
SWA_Q_W = SWA_Q_HEADS * HEAD_DIM
FOX_W = FOX_HEADS * HEAD_DIM

LANES = 128
NEG = -1e30
MIB = 1024 * 1024
LOG2E = 1.4426950408889634

COL_GA = 0
COL_GB = 2048
COL_FK = 4096
D_PROJ = 5120
ROW_AQ = 0
ROW_FV = SWA_Q_W
ROW_FQ = SWA_Q_W + FOX_W
D_QV = ROW_FQ + FOX_W
SWA_KV_W = SWA_KV_HEADS * HEAD_DIM

TM_IN = 1024
TN_IN = 1024
NJ_TOK = D_PROJ // TN_IN
J_AQ = NJ_TOK
J_KV = J_AQ + SWA_Q_W // TN_IN
J_FEAT = J_KV + 1
NJ_IN = J_FEAT + (D_QV - ROW_FV) // TN_IN
assert 2 * (NJ_TOK - 1) >= TM_IN // LANES

TQ_SWA = 512
TQ_FOX = 1024
TK_FOX = 512
FOX_AHEAD = 2
FOX_CHAINS = 2 * (TQ_FOX // TK_FOX)
FOX_SLOTS = 4
assert FOX_CHAINS % FOX_SLOTS == 0 and FOX_AHEAD == 2 and FOX_AHEAD < FOX_SLOTS
FOX_ACC_ROWS = HEAD_DIM + 8
FOX_MAX_JUMP = 100.0
FOX_ROWS = 32
FOX_PIECES = 4
FOX_DEN_ROWS = 64
SWA_DEN_ROWS = 16
DECAY_BLOCKS = 4
TM_MERGE = 512
TM_MLP = 512
TF_MLP = 1024


def _rmsnorm_rows(x, gain):
    ms = jnp.mean(x * x, axis=-1, keepdims=True)
    return x * lax.rsqrt(ms + RMS_EPS) * gain


def _split3(x):
    hi = x.astype(BF16)
    r1 = x - hi.astype(F32)
    mid = r1.astype(BF16)
    lo = (r1 - mid.astype(F32)).astype(BF16)
    return hi, mid, lo


def _inproj_kernel(pos_ref, invf_ref, x_ref, gain_ref, w_ref, wqv_ref, wkv_ref, fbias_ref,
                   proj_ref, logf_ref, qv_ref, kv_ref, h_sc, hn_sc, cos_sc, sin_sc):
    i = pl.program_id(0)
    j = pl.program_id(1)
    nchunk = TM_IN // LANES

    def norm_chunk(dst, c):
        rows = pl.ds(pl.multiple_of(c * LANES, LANES), LANES)
        dst[rows, :] = _rmsnorm_rows(x_ref[rows, :], gain_ref[...]).astype(BF16)

    @pl.when(j == 0)
    def _():
        @pl.when(i == 0)
        def _():
            def body(r, _):
                norm_chunk(h_sc, r)
                return 0
            lax.fori_loop(0, nchunk, body, 0)

        @pl.when(i > 0)
        def _():
            h_sc[...] = hn_sc[...]

        ang = invf_ref[...] * pos_ref[...].astype(F32)
        cos_sc[...] = jnp.cos(ang)
        sin_sc[...] = jnp.sin(ang)

    def features(wt_ref):
        return lax.dot_general(wt_ref[...], h_sc[...], (((1,), (1,)), ((), ())),
                               preferred_element_type=F32)

    def rope_rows(acc, heads, out_ref):
        for hd in range(heads):
            r0 = hd * HEAD_DIM
            t1 = acc[r0:r0 + HALF, :]
            t2 = acc[r0 + HALF:r0 + HEAD_DIM, :]
            out_ref[r0:r0 + HALF, :] = (t1 * cos_sc[...] - t2 * sin_sc[...]).astype(BF16)
            out_ref[r0 + HALF:r0 + HEAD_DIM, :] = (t2 * cos_sc[...] + t1 * sin_sc[...]).astype(BF16)

    @pl.when(j < NJ_TOK)
    def _():
        proj_ref[...] = jnp.dot(h_sc[...], w_ref[...], preferred_element_type=F32).astype(BF16)
        c0 = jnp.clip(2 * (j - (i == 0).astype(jnp.int32)), 0, nchunk - 2)
        norm_chunk(hn_sc, c0)
        norm_chunk(hn_sc, c0 + 1)

    @pl.when((j >= J_AQ) & (j < J_KV))
    def _():
        rope_rows(features(wqv_ref), TN_IN // HEAD_DIM, qv_ref)

    @pl.when(j == J_KV)
    def _():
        acc = features(wkv_ref)
        rope_rows(acc, SWA_KV_HEADS, kv_ref)
        kv_ref[SWA_KV_W:2 * SWA_KV_W, :] = acc[SWA_KV_W:2 * SWA_KV_W, :].astype(BF16)
        z = acc[2 * SWA_KV_W:, :] + fbias_ref[...]
        logf_ref[...] = jnp.minimum(z, 0.0) - jnp.log1p(jnp.exp(-jnp.abs(z)))

    @pl.when(j >= J_FEAT)
    def _():
        qv_ref[...] = features(wqv_ref).astype(BF16)


def _inproj(x2, pos3, invf, gain, w_tok, w_qv, w_kv, fbias, batch, seq):
    t = x2.shape[0]
    nsb = seq // TM_IN
    kv_rows = 2 * SWA_KV_W

    def qv_blk(j):
        return jnp.where(j >= J_FEAT, j - J_FEAT + ROW_FV // TN_IN,
                         jnp.clip(j - J_AQ, 0, SWA_Q_W // TN_IN - 1))

    def tok_blk(j):
        return jnp.minimum(j, NJ_TOK - 1)

    def x_blk(i, j):
        return jnp.where((i == 0) & (j == 0), 0, jnp.minimum(i + 1, t // TM_IN - 1))

    return pl.pallas_call(
        _inproj_kernel,
        grid=(t // TM_IN, NJ_IN),
        in_specs=[
            pl.BlockSpec((None, 1, TM_IN), lambda i, j: (i, 0, 0)),
            pl.BlockSpec((HALF, 1), lambda i, j: (0, 0)),
            pl.BlockSpec((TM_IN, D_MODEL), lambda i, j: (x_blk(i, j), 0)),
            pl.BlockSpec((1, D_MODEL), lambda i, j: (0, 0)),
            pl.BlockSpec((D_MODEL, TN_IN), lambda i, j: (0, tok_blk(j))),
            pl.BlockSpec((TN_IN, D_MODEL), lambda i, j: (qv_blk(j), 0)),
            pl.BlockSpec((kv_rows + FOX_HEADS, D_MODEL), lambda i, j: (0, 0)),
            pl.BlockSpec((FOX_HEADS, 1), lambda i, j: (0, 0)),
        ],
        out_specs=[
            pl.BlockSpec((TM_IN, TN_IN), lambda i, j: (i, tok_blk(j))),
            pl.BlockSpec((None, FOX_HEADS, TM_IN), lambda i, j: (i // nsb, 0, i % nsb)),
            pl.BlockSpec((None, TN_IN, TM_IN), lambda i, j: (i // nsb, qv_blk(j), i % nsb)),
            pl.BlockSpec((None, kv_rows, TM_IN), lambda i, j: (i // nsb, 0, i % nsb)),
        ],
        out_shape=[
            jax.ShapeDtypeStruct((t, D_PROJ), BF16),
            jax.ShapeDtypeStruct((batch, FOX_HEADS, seq), F32),
            jax.ShapeDtypeStruct((batch, D_QV, seq), BF16),
            jax.ShapeDtypeStruct((batch, kv_rows, seq), BF16),
        ],
        scratch_shapes=[
            pltpu.VMEM((TM_IN, D_MODEL), BF16),
            pltpu.VMEM((TM_IN, D_MODEL), BF16),
            pltpu.VMEM((HALF, TM_IN), F32),
            pltpu.VMEM((HALF, TM_IN), F32),
        ],
        compiler_params=pltpu.CompilerParams(
            dimension_semantics=("arbitrary", "arbitrary"),
            vmem_limit_bytes=57 * MIB),
        name="inproj",
    )(pos3, invf, x2, gain, w_tok, w_qv, w_kv, fbias)


def _decay_kernel(logf_ref, aug_ref, base_ref, dec_ref, carry_sc):
    sb = pl.program_id(1)

    @pl.when(sb == 0)
    def _():
        carry_sc[...] = jnp.zeros(carry_sc.shape, F32)

    row = lax.broadcasted_iota(jnp.int32, (TK_FOX, TK_FOX), 0)
    col = lax.broadcasted_iota(jnp.int32, (TK_FOX, TK_FOX), 1)
    upper = jnp.where(row <= col, 1.0, 0.0).astype(BF16)
    pad = jnp.zeros((LANES - 3 * FOX_HEADS, TK_FOX), F32)
    for c in range(DECAY_BLOCKS):
        toks = slice(c * TK_FOX, (c + 1) * TK_FOX)
        hi, mid, lo = _split3(logf_ref[:, toks] * LOG2E)
        cl = (jnp.dot(hi, upper, preferred_element_type=F32)
              + jnp.dot(mid, upper, preferred_element_type=F32)
              + jnp.dot(lo, upper, preferred_element_type=F32))
        yh, ym, yl = _split3(-cl)
        pieces_t = jnp.concatenate([yh.astype(F32), ym.astype(F32), yl.astype(F32), pad], axis=0)
        aug_ref[toks, :] = pieces_t.T.astype(BF16)
        dec_ref[:, toks] = -cl
        total = cl[:, TK_FOX - 1:TK_FOX]
        base_ref[:, c:c + 1] = -carry_sc[...]
        base_ref[:, DECAY_BLOCKS + c:DECAY_BLOCKS + c + 1] = -total
        carry_sc[...] = carry_sc[...] + total


def _decay(logf_t, batch, seq):
    toks = DECAY_BLOCKS * TK_FOX
    ns = seq // toks
    return pl.pallas_call(
        _decay_kernel,
        grid=(batch, ns),
        in_specs=[pl.BlockSpec((None, FOX_HEADS, toks), lambda b, s: (b, 0, s))],
        out_specs=[
            pl.BlockSpec((toks, LANES), lambda b, s: (b * ns + s, 0)),
            pl.BlockSpec((None, None, FOX_HEADS, 2 * DECAY_BLOCKS), lambda b, s: (b, s, 0, 0)),
            pl.BlockSpec((None, FOX_HEADS, toks), lambda b, s: (b, 0, s)),
        ],
        out_shape=[
            jax.ShapeDtypeStruct((batch * seq, LANES), BF16),
            jax.ShapeDtypeStruct((batch, ns, FOX_HEADS, 2 * DECAY_BLOCKS), F32),
            jax.ShapeDtypeStruct((batch, FOX_HEADS, seq), F32),
        ],
        scratch_shapes=[pltpu.VMEM((FOX_HEADS, 1), F32)],
        compiler_params=pltpu.CompilerParams(
            dimension_semantics=("arbitrary", "arbitrary")),
        name="decay",
    )(logf_t)


def _swa_kernel(sinks_ref, qt_ref, cur_ref, prev_ref, o_ref, ktok_sc):
    i = pl.program_id(1)
    nblk = TQ_SWA // WINDOW
    gw = SWA_GROUP * WINDOW
    kv_win = jnp.concatenate([prev_ref[...], cur_ref[...]], axis=1)
    ktok_sc[...] = kv_win[0:SWA_KV_W, :].astype(F32).T.astype(BF16)

    row = lax.broadcasted_iota(jnp.int32, (2 * WINDOW, gw), 0)
    qix = lax.broadcasted_iota(jnp.int32, (2 * WINDOW, gw), 1) % WINDOW
    allowed = (row > qix) & (row <= qix + WINDOW)
    bias_any = jnp.where(allowed, 0.0, NEG)
    bias_first = jnp.where(allowed & (row >= WINDOW), 0.0, NEG)
    bias_r0 = jnp.where(i == 0, bias_first, bias_any)
    zeros_q = jnp.zeros((HEAD_DIM, WINDOW), BF16)
    ones_rows = jnp.ones((SWA_DEN_ROWS, 2 * WINDOW), BF16)

    units = [(r, g) for r in range(nblk) for g in range(SWA_KV_HEADS)]

    def scores(unit):
        r, g = unit
        cols = slice(r * WINDOW, (r + 1) * WINDOW)
        tiles = []
        for hq in range(SWA_GROUP):
            h = g * SWA_GROUP + hq
            q_h = qt_ref[h * HEAD_DIM:(h + 1) * HEAD_DIM, cols]
            tiles.append(jnp.concatenate([q_h, zeros_q] if g == 0 else [zeros_q, q_h], axis=0))
        q_g = jnp.concatenate(tiles, axis=1)
        k_win = ktok_sc[r * WINDOW:(r + 2) * WINDOW, :]
        s = jnp.dot(k_win, q_g, preferred_element_type=F32)
        return s + (bias_r0 if r == 0 else bias_any)

    def finish(s, unit):
        r, g = unit
        sink = jnp.concatenate(
            [jnp.full((1, WINDOW), sinks_ref[g * SWA_GROUP + hq] * LOG2E, F32)
             for hq in range(SWA_GROUP)], axis=1)
        m = jnp.maximum(jnp.max(s, axis=0, keepdims=True), sink)
        pr = jnp.exp2(s - m).astype(BF16)
        v_rows = slice(SWA_KV_W + g * HEAD_DIM, SWA_KV_W + (g + 1) * HEAD_DIM)
        vt = jnp.concatenate([kv_win[v_rows, r * WINDOW:(r + 2) * WINDOW], ones_rows], axis=0)
        acc = jnp.dot(vt, pr, preferred_element_type=F32)
        den = acc[HEAD_DIM:HEAD_DIM + 1, :] + jnp.exp2(sink - m)
        o_t = acc[0:HEAD_DIM, :] * (1.0 / den)
        for pp in range(SWA_GROUP // 2):
            pair = jnp.concatenate([o_t[:, (2 * pp) * WINDOW:(2 * pp + 1) * WINDOW],
                                    o_t[:, (2 * pp + 1) * WINDOW:(2 * pp + 2) * WINDOW]], axis=0)
            c0 = (g * SWA_GROUP + 2 * pp) * HEAD_DIM
            o_ref[r * WINDOW:(r + 1) * WINDOW, c0:c0 + LANES] = pair.T.astype(BF16)

    pending = scores(units[0])
    for n, unit in enumerate(units):
        nxt = scores(units[n + 1]) if n + 1 < len(units) else None
        finish(pending, unit)
        pending = nxt


def _swa(qv_t, kv_t, sinks, batch, seq):
    nsq = seq // TQ_SWA
    per = TQ_SWA // WINDOW
    return pl.pallas_call(
        _swa_kernel,
        grid=(batch, nsq),
        in_specs=[
            pl.BlockSpec(memory_space=pltpu.SMEM),
            pl.BlockSpec((None, SWA_Q_W, TQ_SWA), lambda b, i: (b, ROW_AQ // SWA_Q_W, i)),
            pl.BlockSpec((None, 2 * SWA_KV_W, TQ_SWA), lambda b, i: (b, 0, i)),
            pl.BlockSpec((None, 2 * SWA_KV_W, WINDOW),
                         lambda b, i: (b, 0, jnp.maximum(i * per - 1, 0))),
        ],
        out_specs=pl.BlockSpec((TQ_SWA, SWA_Q_W), lambda b, i: (b * nsq + i, 0)),
        scratch_shapes=[pltpu.VMEM((WINDOW + TQ_SWA, SWA_KV_W), BF16)],
        out_shape=jax.ShapeDtypeStruct((batch * seq, SWA_Q_W), BF16),
        compiler_params=pltpu.CompilerParams(
            dimension_semantics=("arbitrary", "arbitrary")),
        name="swa",
    )(sinks, qv_t, kv_t, kv_t)


def _fox_kernel(base_ref, q_ref, k_ref, aug_ref, vt_ref, dec_ref, o_ref,
                m_sc, al_sc, be_sc, acc_sc, qt_sc, bm_sc, jump_sc, *bufs):
    b = pl.program_id(0)
    p = pl.program_id(1)
    qi = pl.program_id(2)
    s_bufs, p_bufs = bufs[:2], bufs[2:]
    dz = pl.multiple_of(jnp.minimum(qi, 0), FOX_ROWS)
    nb = k_ref.shape[0] // TK_FOX
    per = TQ_FOX // TK_FOX
    q_t = q_ref[...]
    feat = lax.broadcasted_iota(jnp.int32, (LANES, TQ_FOX), 0)
    zero = jnp.zeros_like(q_t)
    for hh in range(2):
        h = 2 * p + hh
        sel = (feat == h) | (feat == h + FOX_HEADS) | (feat == h + 2 * FOX_HEADS)
        qt_sc[hh, 0:LANES, :] = jnp.where((feat // HEAD_DIM) == hh, q_t, zero)
        qt_sc[hh, LANES:2 * LANES, :] = jnp.where(sel, 1.0, 0.0).astype(BF16)

    ones_rows = jnp.ones((FOX_DEN_ROWS, TK_FOX), BF16)

    def reset():
        m_sc[...] = jnp.full(m_sc.shape, NEG, F32)
        acc_sc[...] = jnp.zeros(acc_sc.shape, F32)

    def pieces(q_lo):
        step = TQ_FOX // FOX_PIECES
        return [(max(q_lo, u * step), (u + 1) * step) for u in range(FOX_PIECES)
                if (u + 1) * step > q_lo]

    def key_rows(key0):
        return pl.ds(pl.multiple_of(key0, TK_FOX), TK_FOX)

    def base_of(kblk, hh, span=0):
        return base_ref[((b * nb + kblk) * FOX_HEADS + 2 * p + hh) * 2 + span]

    def qk(key0, hh, lo, hi, limit):
        ks = key_rows(key0)
        k_full = jnp.concatenate([k_ref[ks, :], aug_ref[ks, :]], axis=1)
        s = jnp.dot(k_full, qt_sc[hh, :, lo:hi], preferred_element_type=F32)
        if limit is not None:
            row = lax.broadcasted_iota(jnp.int32, s.shape, 0)
            col = lax.broadcasted_iota(jnp.int32, s.shape, 1)
            s = jnp.where(row - col <= limit, s, NEG)
        return s

    def v_aug(key0, hh):
        return jnp.concatenate(
            [vt_ref[hh * HEAD_DIM:(hh + 1) * HEAD_DIM, key_rows(key0)], ones_rows], axis=0)

    def two_pass_probs(key0, kblk, hh, q_lo, limit_of, sb, pb, one_alpha):
        for lo, hi in pieces(q_lo):
            s = qk(key0, hh, lo, hi, limit_of(lo))
            s_bufs[sb][pl.ds(dz, TK_FOX), lo:hi] = s
            bm_sc[sb, :, lo:hi] = jnp.max(s, axis=0, keepdims=True)
        base = base_of(kblk, hh)
        m_prev = m_sc[hh, :, q_lo:] - base
        m_new = jnp.maximum(m_prev, bm_sc[sb, :, q_lo:])
        al_sc[pb, :, q_lo:] = jnp.ones_like(m_new) if one_alpha else jnp.exp2(m_prev - m_new)
        be_sc[pb, :, q_lo:] = jnp.ones_like(m_new)
        m_sc[hh, :, q_lo:] = m_new + base
        for r in range(0, TK_FOX, FOX_ROWS):
            rows = pl.ds(dz + r, FOX_ROWS)
            pr = jnp.exp2(s_bufs[sb][rows, q_lo:] - m_new)
            p_bufs[pb][rows, q_lo:] = pr.astype(BF16)

    def chain_keys(blk, n):
        return blk * TQ_FOX + (n // 2) * TK_FOX, blk * per + n // 2

    def limit_fn(q_lo, diag):
        if diag is None:
            return lambda lo: None
        if diag is True:
            return lambda lo: lo - q_lo
        return lambda lo: jnp.where(diag, lo - q_lo, TK_FOX)

    def one_pass_piece(blk, n, q_lo, diag, ref, lo, hi):
        key0, _ = chain_keys(blk, n)
        s = qk(key0, n % 2, lo, hi, limit_fn(q_lo, diag)(lo))
        p_bufs[n % FOX_SLOTS][pl.ds(dz, TK_FOX), lo:hi] = jnp.exp2(
            s - ref[:, lo - q_lo:hi - q_lo]).astype(BF16)
        return jnp.max(s, axis=0, keepdims=True)

    def one_pass_finish(n, q_lo, m_prev, ref, base, maxes):
        hh, slot = n % 2, n % FOX_SLOTS
        blk_max = jnp.concatenate(maxes, axis=1) if len(maxes) > 1 else maxes[0]
        m_new = jnp.maximum(m_prev, blk_max)
        al_sc[slot, :, q_lo:] = jnp.exp2(m_prev - m_new)
        be_sc[slot, :, q_lo:] = jnp.exp2(ref - m_new)
        m_sc[hh, :, q_lo:] = m_new + base
        jump_sc[hh, :, q_lo:] = jnp.maximum(jump_sc[hh, :, q_lo:], jnp.abs(blk_max - ref))

    def pv_piece(blk, n, lo, hi):
        hh, slot = n % 2, n % FOX_SLOTS
        key0, _ = chain_keys(blk, n)
        pv = jnp.dot(v_aug(key0, hh), p_bufs[slot][pl.ds(dz, TK_FOX), lo:hi],
                     preferred_element_type=F32)[0:FOX_ACC_ROWS]
        acc_sc[hh, :, lo:hi] = (acc_sc[hh, :, lo:hi] * al_sc[slot, :, lo:hi]
                                + pv * be_sc[slot, :, lo:hi])

    def chain_step(blk, n, q_lo, ahead):
        mine = pieces(q_lo)
        theirs = []
        if ahead is not None:
            a_blk, a_n, a_lo, a_diag = ahead
            theirs = pieces(a_lo)
            a_kblk = chain_keys(a_blk, a_n)[1]
            a_base = base_of(a_kblk, a_n % 2)
            a_prev = m_sc[a_n % 2, :, a_lo:] - a_base
            span = base_of(a_kblk, a_n % 2, span=1)
            if a_diag is None:
                a_ref = a_prev + span
            else:
                reach = dec_ref[a_n % 2:a_n % 2 + 1, key_rows(chain_keys(a_blk, a_n)[0])]
                if TQ_FOX - a_lo > TK_FOX:
                    reach = jnp.concatenate(
                        [reach, jnp.full((1, TQ_FOX - a_lo - TK_FOX), span, F32)], axis=1)
                a_ref = a_prev + (reach if a_diag is True else jnp.where(a_diag, reach, span))
        maxes = []
        for u in range(max(len(mine), len(theirs))):
            if u < len(theirs):
                maxes.append(one_pass_piece(a_blk, a_n, a_lo, a_diag, a_ref, *theirs[u]))
            if u < len(mine):
                pv_piece(blk, n, *mine[u])
        if ahead is not None:
            one_pass_finish(a_n, a_lo, a_prev, a_ref, a_base, maxes)

    def fast_path():
        reset()
        jump_sc[...] = jnp.full(jump_sc.shape, NEG, F32)
        for n in range(FOX_AHEAD):
            key0, kblk = chain_keys(0, n)
            two_pass_probs(key0, kblk, n, 0, limit_fn(0, qi == 0), n, n, True)

        def body(j, carry):
            for n in range(FOX_CHAINS):
                m = n + FOX_AHEAD
                if m < FOX_CHAINS:
                    ahead = (j, m, 0, None)
                else:
                    ahead = (j + 1, m - FOX_CHAINS, 0, j + 1 == qi)
                chain_step(j, n, 0, ahead)
            return carry

        lax.fori_loop(0, qi, body, 0)
        for n in range(FOX_CHAINS):
            m = n + FOX_AHEAD
            ahead = (qi, m, (m // 2) * TK_FOX, True) if m < FOX_CHAINS else None
            chain_step(qi, n, (n // 2) * TK_FOX, ahead)

    def safe_path():
        reset()

        def body(t, carry):
            blk, c = t // per, t % per
            limit = jnp.where(blk == qi, -c * TK_FOX, TK_FOX)
            for hh in range(2):
                two_pass_probs(t * TK_FOX, t, hh, 0, lambda lo: limit + lo, hh, hh, False)
                pv = jnp.dot(v_aug(t * TK_FOX, hh), p_bufs[hh][pl.ds(dz, TK_FOX), :],
                             preferred_element_type=F32)
                acc_sc[hh] = acc_sc[hh] * al_sc[hh] + pv[0:FOX_ACC_ROWS]
            return carry

        lax.fori_loop(0, (qi + 1) * per, body, 0)

    fast_path()

    @pl.when(jnp.max(jump_sc[...]) > FOX_MAX_JUMP)
    def _():
        safe_path()


    o_t = jnp.concatenate(
        [acc_sc[hh, 0:HEAD_DIM, :] * (1.0 / acc_sc[hh, HEAD_DIM:HEAD_DIM + 1, :]) for hh in range(2)],
        axis=0)
    o_ref[...] = o_t.T.astype(BF16)


def _fox(proj, aug, base, dec_t, qv_t, batch, seq):
    t = proj.shape[0]
    nq = seq // TQ_FOX
    pairs = FOX_HEADS // 2
    return pl.pallas_call(
        _fox_kernel,
        grid=(batch, pairs, nq),
        in_specs=[
            pl.BlockSpec(memory_space=pltpu.SMEM),
            pl.BlockSpec((None, LANES, TQ_FOX), lambda b, p, i: (b, ROW_FQ // LANES + p, i)),
            pl.BlockSpec((seq, LANES), lambda b, p, i: (b, COL_FK // LANES + p)),
            pl.BlockSpec((seq, LANES), lambda b, p, i: (b, 0)),
            pl.BlockSpec((None, LANES, seq), lambda b, p, i: (b, ROW_FV // LANES + p, 0)),
            pl.BlockSpec((None, None, 2, seq), lambda b, p, i: (b, p, 0, 0)),
        ],
        out_specs=pl.BlockSpec((TQ_FOX, LANES), lambda b, p, i: (b * nq + i, p)),
        out_shape=jax.ShapeDtypeStruct((t, FOX_W), BF16),
        scratch_shapes=[
            pltpu.VMEM((2, 1, TQ_FOX), F32),
            pltpu.VMEM((FOX_SLOTS, 1, TQ_FOX), F32),
            pltpu.VMEM((FOX_SLOTS, 1, TQ_FOX), F32),
            pltpu.VMEM((2, FOX_ACC_ROWS, TQ_FOX), F32),
            pltpu.VMEM((2, 2 * LANES, TQ_FOX), BF16),
            pltpu.VMEM((2, 1, TQ_FOX), F32),
            pltpu.VMEM((2, 1, TQ_FOX), F32),
        ] + [pltpu.VMEM((TK_FOX, TQ_FOX), F32)] * 2
          + [pltpu.VMEM((TK_FOX, TQ_FOX), BF16)] * FOX_SLOTS,
        compiler_params=pltpu.CompilerParams(
            dimension_semantics=("arbitrary", "arbitrary", "arbitrary")),
        name="fox",
    )(base, qv_t, proj, aug, qv_t, dec_t.reshape(batch, pairs, 2, seq))


def _merge_kernel(oa_ref, ob_ref, ga_ref, gb_ref, x_ref, wa_ref, wb_ref, wo_ref, gain_ref,
                  x1_ref, h2_ref):
    ya = jnp.dot(oa_ref[...], wa_ref[...], preferred_element_type=F32)
    yb = jnp.dot(ob_ref[...], wb_ref[...], preferred_element_type=F32)

    def gate(ref):
        return 1.0 / (1.0 + jnp.exp(-ref[...].astype(F32)))

    merged = gate(ga_ref) * ya + gate(gb_ref) * yb
    x1 = x_ref[...] + jnp.dot(merged.astype(BF16), wo_ref[...], preferred_element_type=F32)
    x1_ref[...] = x1
    h2_ref[...] = _rmsnorm_rows(x1, gain_ref[...]).astype(BF16)


def _merge(o_a, o_b, proj, x2, wa, wb, wo, gain):
    t = x2.shape[0]
    tm = TM_MERGE
    const = dict(pipeline_mode=pl.Buffered(1))
    return pl.pallas_call(
        _merge_kernel,
        grid=(t // tm,),
        in_specs=[
            pl.BlockSpec((tm, SWA_Q_W), lambda i: (i, 0)),
            pl.BlockSpec((tm, FOX_W), lambda i: (i, 0)),
            pl.BlockSpec((tm, D_MODEL), lambda i: (i, COL_GA // D_MODEL)),
            pl.BlockSpec((tm, D_MODEL), lambda i: (i, COL_GB // D_MODEL)),
            pl.BlockSpec((tm, D_MODEL), lambda i: (i, 0)),
            pl.BlockSpec((SWA_Q_W, D_MODEL), lambda i: (0, 0), **const),
            pl.BlockSpec((FOX_W, D_MODEL), lambda i: (0, 0), **const),
            pl.BlockSpec((D_MODEL, D_MODEL), lambda i: (0, 0), **const),
            pl.BlockSpec((1, D_MODEL), lambda i: (0, 0), **const),
        ],
        out_specs=[
            pl.BlockSpec((tm, D_MODEL), lambda i: (i, 0)),
            pl.BlockSpec((tm, D_MODEL), lambda i: (i, 0)),
        ],
        out_shape=[
            jax.ShapeDtypeStruct((t, D_MODEL), F32),
            jax.ShapeDtypeStruct((t, D_MODEL), BF16),
        ],
        compiler_params=pltpu.CompilerParams(
            dimension_semantics=("arbitrary",),
            vmem_limit_bytes=56 * MIB),
        name="merge",
    )(o_a, o_b, proj, proj, x2, wa, wb, wo, gain)


def _mlp_kernel(h2_ref, x1_ref, wup_ref, wdn_ref, gain_ref, o_ref, *, final_norm):
    f = pl.program_id(1)
    last = pl.num_programs(1) - 1

    def delta():
        u = jnp.maximum(jnp.dot(h2_ref[...], wup_ref[...], preferred_element_type=F32), 0.0)
        return jnp.dot((u * u).astype(BF16), wdn_ref[...], preferred_element_type=F32)

    @pl.when(f == 0)
    def _():
        o_ref[...] = x1_ref[...] + delta()

    @pl.when((f > 0) & (f < last))
    def _():
        o_ref[...] += delta()

    @pl.when(f == last)
    def _():
        y = o_ref[...] + delta()
        o_ref[...] = _rmsnorm_rows(y, gain_ref[...]) if final_norm else y


def _mlp(h2, x1, wup, wdn, gain, final_norm):
    t = x1.shape[0]
    tm, tf = TM_MLP, TF_MLP
    return pl.pallas_call(
        functools.partial(_mlp_kernel, final_norm=final_norm),
        grid=(t // tm, D_FF // tf),
        in_specs=[
            pl.BlockSpec((tm, D_MODEL), lambda i, f: (i, 0)),
            pl.BlockSpec((tm, D_MODEL), lambda i, f: (i, 0)),
            pl.BlockSpec((D_MODEL, tf), lambda i, f: (0, f)),
            pl.BlockSpec((tf, D_MODEL), lambda i, f: (f, 0)),
            pl.BlockSpec((1, D_MODEL), lambda i, f: (0, 0)),
        ],
        out_specs=pl.BlockSpec((tm, D_MODEL), lambda i, f: (i, 0)),
        out_shape=jax.ShapeDtypeStruct((t, D_MODEL), F32),
        compiler_params=pltpu.CompilerParams(
            dimension_semantics=("arbitrary", "arbitrary"),
            vmem_limit_bytes=48 * MIB),
        name="mlp",
    )(h2, x1, wup, wdn, gain)


def _pack_w_in(w, scale):
    o = 0
    a_q = w[:, o:o + SWA_Q_W] * (scale * LOG2E); o += SWA_Q_W
    a_k = w[:, o:o + SWA_KV_W]; o += SWA_KV_W
    a_v = w[:, o:o + SWA_KV_W]; o += SWA_KV_W
    f_q = w[:, o:o + FOX_W] * (scale * LOG2E); o += FOX_W
    f_k = w[:, o:o + FOX_W]; o += FOX_W
    f_v = w[:, o:o + FOX_W]; o += FOX_W
    f_l = w[:, o:o + FOX_HEADS]; o += FOX_HEADS
    g_a = w[:, o:o + D_MODEL]; o += D_MODEL
    g_b = w[:, o:o + D_MODEL]; o += D_MODEL
    w_tok = jnp.concatenate([g_a, g_b, f_k], axis=1).astype(BF16)
    w_qv = jnp.concatenate([a_q, f_v, f_q], axis=1).T.astype(BF16)
    w_kv = jnp.concatenate([a_k, a_v, f_l], axis=1).T.astype(BF16)
    return w_tok, w_qv, w_kv


def kernel(x, positions, attn_norm, w_in, fox_f_bias, swa_sinks, w_branch_swa, w_branch_fox,
           w_out, mlp_norm, w_up, w_down, final_norm):
    batch, seq, d = x.shape
    depth = w_in.shape[0]
    assert d == D_MODEL and seq % TM_IN == 0 and seq % TQ_FOX == 0
    t = batch * seq
    scale = HEAD_DIM ** -0.5
    inv_freq = ROPE_THETA ** (-jnp.arange(0, HEAD_DIM, 2, dtype=F32) / HEAD_DIM)
    invf = inv_freq[:, None]
    pos3 = positions.reshape(t // TM_IN, 1, TM_IN)
    x2 = x.reshape(t, d)
    for l in range(depth):
        w_tok, w_qv, w_kv = _pack_w_in(w_in[l], scale)
        fbias = fox_f_bias[l].astype(F32)[:, None]
        proj, logf_t, qv_t, kv_t = _inproj(x2, pos3, invf, attn_norm[l][None, :].astype(F32),
                                           w_tok, w_qv, w_kv, fbias, batch, seq)
        aug, base, dec_t = _decay(logf_t, batch, seq)
        base = base.reshape(batch, -1, FOX_HEADS, 2, DECAY_BLOCKS).transpose(0, 1, 4, 2, 3).reshape(-1)
        o_a = _swa(qv_t, kv_t, swa_sinks[l].astype(F32), batch, seq)
        o_b = _fox(proj, aug, base, dec_t, qv_t, batch, seq)
        x1, h2 = _merge(o_a, o_b, proj, x2,
                        w_branch_swa[l].astype(BF16), w_branch_fox[l].astype(BF16),
                        w_out[l].astype(BF16), mlp_norm[l][None, :].astype(F32))
        x2 = _mlp(h2, x1, w_up[l].astype(BF16), w_down[l].astype(BF16),
                  final_norm[None, :].astype(F32), final_norm=(l == depth - 1))
    return x2.reshape(batch, seq, d)
```

```python
import functools

import jax
import jax.numpy as jnp
from jax import lax
from jax.experimental import pallas as pl
from jax.experimental.pallas import tpu as pltpu

F32 = jnp.float32
BF16 = jnp.bfloat16

D_MODEL = 2048
HEAD_DIM = 64
HALF = HEAD_DIM // 2
SWA_Q_HEADS = 16
SWA_KV_HEADS = 2
SWA_GROUP = SWA_Q_HEADS // SWA_KV_HEADS
WINDOW = 128
FOX_HEADS = 16
D_FF = 4 * D_MODEL
ROPE_THETA = 10000.0
RMS_EPS = 1e-6
SWA_Q_W = SWA_Q_HEADS * HEAD_DIM
FOX_W = FOX_HEADS * HEAD_DIM

LANES = 128
NEG = -1e30
MIB = 1024 * 1024
LOG2E = 1.4426950408889634

COL_GA = 0
COL_GB = 2048
COL_FK = 4096
D_PROJ = 5120
ROW_AQ = 0
ROW_FV = SWA_Q_W
ROW_FQ = SWA_Q_W + FOX_W
D_QV = ROW_FQ + FOX_W
SWA_KV_W = SWA_KV_HEADS * HEAD_DIM

TM_IN = 1024
TN_IN = 1024
NJ_TOK = D_PROJ // TN_IN
J_AQ = NJ_TOK
J_KV = J_AQ + SWA_Q_W // TN_IN
J_FEAT = J_KV + 1
NJ_IN = J_FEAT + (D_QV - ROW_FV) // TN_IN
assert 2 * (NJ_TOK - 1) >= TM_IN // LANES

TQ_SWA = 512
TQ_FOX = 1024
TK_FOX = 512
FOX_AHEAD = 2
FOX_CHAINS = 2 * (TQ_FOX // TK_FOX)
FOX_SLOTS = 4
assert FOX_CHAINS % FOX_SLOTS == 0 and FOX_AHEAD == 2 and FOX_AHEAD < FOX_SLOTS
FOX_ACC_ROWS = HEAD_DIM + 8
FOX_MAX_JUMP = 100.0
FOX_ROWS = 32
FOX_PIECES = 4
FOX_DEN_ROWS = 64
SWA_DEN_ROWS = 16
DECAY_BLOCKS = 4
TM_MERGE = 512
TM_MLP = 512
TF_MLP = 1024


def _rmsnorm_rows(x, gain):
    ms = jnp.mean(x * x, axis=-1, keepdims=True)
    return x * lax.rsqrt(ms + RMS_EPS) * gain


def _split3(x):
    hi = x.astype(BF16)
    r1 = x - hi.astype(F32)
    mid = r1.astype(BF16)
    lo = (r1 - mid.astype(F32)).astype(BF16)
    return hi, mid, lo


def _inproj_kernel(pos_ref, invf_ref, x_ref, gain_ref, w_ref, wqv_ref, wkv_ref, fbias_ref,
                   proj_ref, logf_ref, qv_ref, kv_ref, h_sc, hn_sc, cos_sc, sin_sc):
    i = pl.program_id(0)
    j = pl.program_id(1)
    nchunk = TM_IN // LANES

    def norm_chunk(dst, c):
        rows = pl.ds(pl.multiple_of(c * LANES, LANES), LANES)
        dst[rows, :] = _rmsnorm_rows(x_ref[rows, :], gain_ref[...]).astype(BF16)

    @pl.when(j == 0)
    def _():
        @pl.when(i == 0)
        def _():
            def body(r, _):
                norm_chunk(h_sc, r)
                return 0
            lax.fori_loop(0, nchunk, body, 0)

        @pl.when(i > 0)
        def _():
            h_sc[...] = hn_sc[...]

        ang = invf_ref[...] * pos_ref[...].astype(F32)
        cos_sc[...] = jnp.cos(ang)
        sin_sc[...] = jnp.sin(ang)

    def features(wt_ref):
        return lax.dot_general(wt_ref[...], h_sc[...], (((1,), (1,)), ((), ())),
                               preferred_element_type=F32)

    def rope_rows(acc, heads, out_ref):
        for hd in range(heads):
            r0 = hd * HEAD_DIM
            t1 = acc[r0:r0 + HALF, :]
            t2 = acc[r0 + HALF:r0 + HEAD_DIM, :]
            out_ref[r0:r0 + HALF, :] = (t1 * cos_sc[...] - t2 * sin_sc[...]).astype(BF16)
            out_ref[r0 + HALF:r0 + HEAD_DIM, :] = (t2 * cos_sc[...] + t1 * sin_sc[...]).astype(BF16)

    @pl.when(j < NJ_TOK)
    def _():
        proj_ref[...] = jnp.dot(h_sc[...], w_ref[...], preferred_element_type=F32).astype(BF16)
        c0 = jnp.clip(2 * (j - (i == 0).astype(jnp.int32)), 0, nchunk - 2)
        norm_chunk(hn_sc, c0)
        norm_chunk(hn_sc, c0 + 1)

    @pl.when((j >= J_AQ) & (j < J_KV))
    def _():
        rope_rows(features(wqv_ref), TN_IN // HEAD_DIM, qv_ref)

    @pl.when(j == J_KV)
    def _():
        acc = features(wkv_ref)
        rope_rows(acc, SWA_KV_HEADS, kv_ref)
        kv_ref[SWA_KV_W:2 * SWA_KV_W, :] = acc[SWA_KV_W:2 * SWA_KV_W, :].astype(BF16)
        z = acc[2 * SWA_KV_W:, :] + fbias_ref[...]
        logf_ref[...] = jnp.minimum(z, 0.0) - jnp.log1p(jnp.exp(-jnp.abs(z)))

    @pl.when(j >= J_FEAT)
    def _():
        qv_ref[...] = features(wqv_ref).astype(BF16)


def _inproj(x2, pos3, invf, gain, w_tok, w_qv, w_kv, fbias, batch, seq):
    t = x2.shape[0]
    nsb = seq // TM_IN
    kv_rows = 2 * SWA_KV_W

    def qv_blk(j):
        return jnp.where(j >= J_FEAT, j - J_FEAT + ROW_FV // TN_IN,
                         jnp.clip(j - J_AQ, 0, SWA_Q_W // TN_IN - 1))

    def tok_blk(j):
        return jnp.minimum(j, NJ_TOK - 1)

    def x_blk(i, j):
        return jnp.where((i == 0) & (j == 0), 0, jnp.minimum(i + 1, t // TM_IN - 1))

    return pl.pallas_call(
        _inproj_kernel,
        grid=(t // TM_IN, NJ_IN),
        in_specs=[
            pl.BlockSpec((None, 1, TM_IN), lambda i, j: (i, 0, 0)),
            pl.BlockSpec((HALF, 1), lambda i, j: (0, 0)),
            pl.BlockSpec((TM_IN, D_MODEL), lambda i, j: (x_blk(i, j), 0)),
            pl.BlockSpec((1, D_MODEL), lambda i, j: (0, 0)),
            pl.BlockSpec((D_MODEL, TN_IN), lambda i, j: (0, tok_blk(j))),
            pl.BlockSpec((TN_IN, D_MODEL), lambda i, j: (qv_blk(j), 0)),
            pl.BlockSpec((kv_rows + FOX_HEADS, D_MODEL), lambda i, j: (0, 0)),
            pl.BlockSpec((FOX_HEADS, 1), lambda i, j: (0, 0)),
        ],
        out_specs=[
            pl.BlockSpec((TM_IN, TN_IN), lambda i, j: (i, tok_blk(j))),
            pl.BlockSpec((None, FOX_HEADS, TM_IN), lambda i, j: (i // nsb, 0, i % nsb)),
            pl.BlockSpec((None, TN_IN, TM_IN), lambda i, j: (i // nsb, qv_blk(j), i % nsb)),
            pl.BlockSpec((None, kv_rows, TM_IN), lambda i, j: (i // nsb, 0, i % nsb)),
        ],
        out_shape=[
            jax.ShapeDtypeStruct((t, D_PROJ), BF16),
            jax.ShapeDtypeStruct((batch, FOX_HEADS, seq), F32),
            jax.ShapeDtypeStruct((batch, D_QV, seq), BF16),
            jax.ShapeDtypeStruct((batch, kv_rows, seq), BF16),
        ],
        scratch_shapes=[
            pltpu.VMEM((TM_IN, D_MODEL), BF16),
            pltpu.VMEM((TM_IN, D_MODEL), BF16),
            pltpu.VMEM((HALF, TM_IN), F32),
            pltpu.VMEM((HALF, TM_IN), F32),
        ],
        compiler_params=pltpu.CompilerParams(
            dimension_semantics=("arbitrary", "arbitrary"),
            vmem_limit_bytes=57 * MIB),
        name="inproj",
    )(pos3, invf, x2, gain, w_tok, w_qv, w_kv, fbias)


def _decay_kernel(logf_ref, aug_ref, base_ref, dec_ref, carry_sc):
    sb = pl.program_id(1)

    @pl.when(sb == 0)
    def _():
        carry_sc[...] = jnp.zeros(carry_sc.shape, F32)

    row = lax.broadcasted_iota(jnp.int32, (TK_FOX, TK_FOX), 0)
    col = lax.broadcasted_iota(jnp.int32, (TK_FOX, TK_FOX), 1)
    upper = jnp.where(row <= col, 1.0, 0.0).astype(BF16)
    pad = jnp.zeros((LANES - 3 * FOX_HEADS, TK_FOX), F32)
    for c in range(DECAY_BLOCKS):
        toks = slice(c * TK_FOX, (c + 1) * TK_FOX)
        hi, mid, lo = _split3(logf_ref[:, toks] * LOG2E)
        cl = (jnp.dot(hi, upper, preferred_element_type=F32)
              + jnp.dot(mid, upper, preferred_element_type=F32)
              + jnp.dot(lo, upper, preferred_element_type=F32))
        yh, ym, yl = _split3(-cl)
        pieces_t = jnp.concatenate([yh.astype(F32), ym.astype(F32), yl.astype(F32), pad], axis=0)
        aug_ref[toks, :] = pieces_t.T.astype(BF16)
        dec_ref[:, toks] = -cl
        total = cl[:, TK_FOX - 1:TK_FOX]
        base_ref[:, c:c + 1] = -carry_sc[...]
        base_ref[:, DECAY_BLOCKS + c:DECAY_BLOCKS + c + 1] = -total
        carry_sc[...] = carry_sc[...] + total


def _decay(logf_t, batch, seq):
    toks = DECAY_BLOCKS * TK_FOX
    ns = seq // toks
    return pl.pallas_call(
        _decay_kernel,
        grid=(batch, ns),
        in_specs=[pl.BlockSpec((None, FOX_HEADS, toks), lambda b, s: (b, 0, s))],
        out_specs=[
            pl.BlockSpec((toks, LANES), lambda b, s: (b * ns + s, 0)),
            pl.BlockSpec((None, None, FOX_HEADS, 2 * DECAY_BLOCKS), lambda b, s: (b, s, 0, 0)),
            pl.BlockSpec((None, FOX_HEADS, toks), lambda b, s: (b, 0, s)),
        ],
        out_shape=[
            jax.ShapeDtypeStruct((batch * seq, LANES), BF16),
            jax.ShapeDtypeStruct((batch, ns, FOX_HEADS, 2 * DECAY_BLOCKS), F32),
            jax.ShapeDtypeStruct((batch, FOX_HEADS, seq), F32),
        ],
        scratch_shapes=[pltpu.VMEM((FOX_HEADS, 1), F32)],
        compiler_params=pltpu.CompilerParams(
            dimension_semantics=("arbitrary", "arbitrary")),
        name="decay",
    )(logf_t)


def _swa_kernel(sinks_ref, qt_ref, cur_ref, prev_ref, o_ref, ktok_sc):
    i = pl.program_id(1)
    nblk = TQ_SWA // WINDOW
    gw = SWA_GROUP * WINDOW
    kv_win = jnp.concatenate([prev_ref[...], cur_ref[...]], axis=1)
    ktok_sc[...] = kv_win[0:SWA_KV_W, :].astype(F32).T.astype(BF16)

    row = lax.broadcasted_iota(jnp.int32, (2 * WINDOW, gw), 0)
    qix = lax.broadcasted_iota(jnp.int32, (2 * WINDOW, gw), 1) % WINDOW
    allowed = (row > qix) & (row <= qix + WINDOW)
    bias_any = jnp.where(allowed, 0.0, NEG)
    bias_first = jnp.where(allowed & (row >= WINDOW), 0.0, NEG)
    bias_r0 = jnp.where(i == 0, bias_first, bias_any)
    zeros_q = jnp.zeros((HEAD_DIM, WINDOW), BF16)
    ones_rows = jnp.ones((SWA_DEN_ROWS, 2 * WINDOW), BF16)

    units = [(r, g) for r in range(nblk) for g in range(SWA_KV_HEADS)]

    def scores(unit):
        r, g = unit
        cols = slice(r * WINDOW, (r + 1) * WINDOW)
        tiles = []
        for hq in range(SWA_GROUP):
            h = g * SWA_GROUP + hq
            q_h = qt_ref[h * HEAD_DIM:(h + 1) * HEAD_DIM, cols]
            tiles.append(jnp.concatenate([q_h, zeros_q] if g == 0 else [zeros_q, q_h], axis=0))
        q_g = jnp.concatenate(tiles, axis=1)
        k_win = ktok_sc[r * WINDOW:(r + 2) * WINDOW, :]
        s = jnp.dot(k_win, q_g, preferred_element_type=F32)
        return s + (bias_r0 if r == 0 else bias_any)

    def finish(s, unit):
        r, g = unit
        sink = jnp.concatenate(
            [jnp.full((1, WINDOW), sinks_ref[g * SWA_GROUP + hq] * LOG2E, F32)
             for hq in range(SWA_GROUP)], axis=1)
        m = jnp.maximum(jnp.max(s, axis=0, keepdims=True), sink)
        pr = jnp.exp2(s - m).astype(BF16)
        v_rows = slice(SWA_KV_W + g * HEAD_DIM, SWA_KV_W + (g + 1) * HEAD_DIM)
        vt = jnp.concatenate([kv_win[v_rows, r * WINDOW:(r + 2) * WINDOW], ones_rows], axis=0)
        acc = jnp.dot(vt, pr, preferred_element_type=F32)
        den = acc[HEAD_DIM:HEAD_DIM + 1, :] + jnp.exp2(sink - m)
        o_t = acc[0:HEAD_DIM, :] * (1.0 / den)
        for pp in range(SWA_GROUP // 2):
            pair = jnp.concatenate([o_t[:, (2 * pp) * WINDOW:(2 * pp + 1) * WINDOW],
                                    o_t[:, (2 * pp + 1) * WINDOW:(2 * pp + 2) * WINDOW]], axis=0)
            c0 = (g * SWA_GROUP + 2 * pp) * HEAD_DIM
            o_ref[r * WINDOW:(r + 1) * WINDOW, c0:c0 + LANES] = pair.T.astype(BF16)

    pending = scores(units[0])
    for n, unit in enumerate(units):
        nxt = scores(units[n + 1]) if n + 1 < len(units) else None
        finish(pending, unit)
        pending = nxt


def _swa(qv_t, kv_t, sinks, batch, seq):
    nsq = seq // TQ_SWA
    per = TQ_SWA // WINDOW
    return pl.pallas_call(
        _swa_kernel,
        grid=(batch, nsq),
        in_specs=[
            pl.BlockSpec(memory_space=pltpu.SMEM),
            pl.BlockSpec((None, SWA_Q_W, TQ_SWA), lambda b, i: (b, ROW_AQ // SWA_Q_W, i)),
            pl.BlockSpec((None, 2 * SWA_KV_W, TQ_SWA), lambda b, i: (b, 0, i)),
            pl.BlockSpec((None, 2 * SWA_KV_W, WINDOW),
                         lambda b, i: (b, 0, jnp.maximum(i * per - 1, 0))),
        ],
        out_specs=pl.BlockSpec((TQ_SWA, SWA_Q_W), lambda b, i: (b * nsq + i, 0)),
        scratch_shapes=[pltpu.VMEM((WINDOW + TQ_SWA, SWA_KV_W), BF16)],
        out_shape=jax.ShapeDtypeStruct((batch * seq, SWA_Q_W), BF16),
        compiler_params=pltpu.CompilerParams(
            dimension_semantics=("arbitrary", "arbitrary")),
        name="swa",
    )(sinks, qv_t, kv_t, kv_t)


def _fox_kernel(base_ref, q_ref, k_ref, aug_ref, vt_ref, dec_ref, o_ref,
                m_sc, al_sc, be_sc, acc_sc, qt_sc, bm_sc, jump_sc, *bufs):
    b = pl.program_id(0)
    p = pl.program_id(1)
    qi = pl.program_id(2)
    s_bufs, p_bufs = bufs[:2], bufs[2:]
    dz = pl.multiple_of(jnp.minimum(qi, 0), FOX_ROWS)
    nb = k_ref.shape[0] // TK_FOX
    per = TQ_FOX // TK_FOX
    q_t = q_ref[...]
    feat = lax.broadcasted_iota(jnp.int32, (LANES, TQ_FOX), 0)
    zero = jnp.zeros_like(q_t)
    for hh in range(2):
        h = 2 * p + hh
        sel = (feat == h) | (feat == h + FOX_HEADS) | (feat == h + 2 * FOX_HEADS)
        qt_sc[hh, 0:LANES, :] = jnp.where((feat // HEAD_DIM) == hh, q_t, zero)
        qt_sc[hh, LANES:2 * LANES, :] = jnp.where(sel, 1.0, 0.0).astype(BF16)

    ones_rows = jnp.ones((FOX_DEN_ROWS, TK_FOX), BF16)

    def reset():
        m_sc[...] = jnp.full(m_sc.shape, NEG, F32)
        acc_sc[...] = jnp.zeros(acc_sc.shape, F32)

    def pieces(q_lo):
        step = TQ_FOX // FOX_PIECES
        return [(max(q_lo, u * step), (u + 1) * step) for u in range(FOX_PIECES)
                if (u + 1) * step > q_lo]

    def key_rows(key0):
        return pl.ds(pl.multiple_of(key0, TK_FOX), TK_FOX)

    def base_of(kblk, hh, span=0):
        return base_ref[((b * nb + kblk) * FOX_HEADS + 2 * p + hh) * 2 + span]

    def qk(key0, hh, lo, hi, limit):
        ks = key_rows(key0)
        k_full = jnp.concatenate([k_ref[ks, :], aug_ref[ks, :]], axis=1)
        s = jnp.dot(k_full, qt_sc[hh, :, lo:hi], preferred_element_type=F32)
        if limit is not None:
            row = lax.broadcasted_iota(jnp.int32, s.shape, 0)
            col = lax.broadcasted_iota(jnp.int32, s.shape, 1)
            s = jnp.where(row - col <= limit, s, NEG)
        return s

    def v_aug(key0, hh):
        return jnp.concatenate(
            [vt_ref[hh * HEAD_DIM:(hh + 1) * HEAD_DIM, key_rows(key0)], ones_rows], axis=0)

    def two_pass_probs(key0, kblk, hh, limit_of, sb, pb):
        for lo, hi in pieces(0):
            s = qk(key0, hh, lo, hi, limit_of(lo))
            s_bufs[sb][pl.ds(dz, TK_FOX), lo:hi] = s
            bm_sc[sb, :, lo:hi] = jnp.max(s, axis=0, keepdims=True)
        base = base_of(kblk, hh)
        m_prev = m_sc[hh] - base
        m_new = jnp.maximum(m_prev, bm_sc[sb])
        al_sc[pb] = jnp.exp2(m_prev - m_new)
        m_sc[hh] = m_new + base
        for r in range(0, TK_FOX, FOX_ROWS):
            rows = pl.ds(dz + r, FOX_ROWS)
            pr = jnp.exp2(s_bufs[sb][rows, :] - m_new)
            p_bufs[pb][rows, :] = pr.astype(BF16)

    def chain_keys(blk, n):
        return blk * TQ_FOX + (n // 2) * TK_FOX, blk * per + n // 2

    def limit_fn(q_lo, diag):
        if diag is None:
            return lambda lo: None
        if diag is True:
            return lambda lo: lo - q_lo
        return lambda lo: jnp.where(diag, lo - q_lo, TK_FOX)

    def one_pass_piece(blk, n, q_lo, diag, ref, lo, hi):
        key0, _ = chain_keys(blk, n)
        s = qk(key0, n % 2, lo, hi, limit_fn(q_lo, diag)(lo))
        p_bufs[n % FOX_SLOTS][pl.ds(dz, TK_FOX), lo:hi] = jnp.exp2(
            s - ref[:, lo - q_lo:hi - q_lo]).astype(BF16)
        return jnp.max(s, axis=0, keepdims=True)

    def one_pass_finish(n, q_lo, m_prev, ref, base, maxes):
        hh, slot = n % 2, n % FOX_SLOTS
        blk_max = jnp.concatenate(maxes, axis=1) if len(maxes) > 1 else maxes[0]
        m_new = jnp.maximum(m_prev, blk_max)
        al_sc[slot, :, q_lo:] = jnp.exp2(m_prev - m_new)
        be_sc[slot, :, q_lo:] = jnp.exp2(ref - m_new)
        m_sc[hh, :, q_lo:] = m_new + base
        jump_sc[hh, :, q_lo:] = jnp.maximum(jump_sc[hh, :, q_lo:], jnp.abs(blk_max - ref))

    def pv_piece(blk, n, lo, hi):
        hh, slot = n % 2, n % FOX_SLOTS
        key0, _ = chain_keys(blk, n)
        pv = jnp.dot(v_aug(key0, hh), p_bufs[slot][pl.ds(dz, TK_FOX), lo:hi],
                     preferred_element_type=F32)[0:FOX_ACC_ROWS]
        acc_sc[hh, :, lo:hi] = (acc_sc[hh, :, lo:hi] * al_sc[slot, :, lo:hi]
                                + pv * be_sc[slot, :, lo:hi])

    def chain_step(blk, n, q_lo, ahead):
        mine = pieces(q_lo)
        theirs = []
        if ahead is not None:
            a_blk, a_n, a_lo, a_diag, a_first = ahead
            theirs = pieces(a_lo)
            a_kblk = chain_keys(a_blk, a_n)[1]
            a_base = base_of(a_kblk, a_n % 2)
            if a_first:
                a_prev = jnp.full((1, TQ_FOX - a_lo), NEG, F32)
                ref0 = 0.0
            else:
                a_prev = m_sc[a_n % 2, :, a_lo:] - a_base
                ref0 = a_prev
            span = base_of(a_kblk, a_n % 2, span=1)
            if a_diag is None:
                a_ref = ref0 + span
            else:
                reach = dec_ref[a_n % 2:a_n % 2 + 1, key_rows(chain_keys(a_blk, a_n)[0])]
                if TQ_FOX - a_lo > TK_FOX:
                    reach = jnp.concatenate(
                        [reach, jnp.full((1, TQ_FOX - a_lo - TK_FOX), span, F32)], axis=1)
                a_ref = ref0 + (reach if a_diag is True else jnp.where(a_diag, reach, span))
        maxes = []
        for u in range(max(len(mine), len(theirs))):
            if u < len(theirs):
                maxes.append(one_pass_piece(a_blk, a_n, a_lo, a_diag, a_ref, *theirs[u]))
            if u < len(mine):
                pv_piece(blk, n, *mine[u])
        if ahead is not None:
            one_pass_finish(a_n, a_lo, a_prev, a_ref, a_base, maxes)

    def fast_path():
        reset()
        jump_sc[...] = jnp.full(jump_sc.shape, NEG, F32)
        for n in range(FOX_AHEAD):
            chain_step(0, n, TQ_FOX, (0, n, 0, qi == 0, True))

        def body(j, carry):
            for n in range(FOX_CHAINS):
                m = n + FOX_AHEAD
                if m < FOX_CHAINS:
                    ahead = (j, m, 0, None, False)
                else:
                    ahead = (j + 1, m - FOX_CHAINS, 0, j + 1 == qi, False)
                chain_step(j, n, 0, ahead)
            return carry

        lax.fori_loop(0, qi, body, 0)
        for n in range(FOX_CHAINS):
            m = n + FOX_AHEAD
            ahead = (qi, m, (m // 2) * TK_FOX, True, False) if m < FOX_CHAINS else None
            chain_step(qi, n, (n // 2) * TK_FOX, ahead)

    def safe_path():
        reset()

        def body(t, carry):
            blk, c = t // per, t % per
            limit = jnp.where(blk == qi, -c * TK_FOX, TK_FOX)
            for hh in range(2):
                two_pass_probs(t * TK_FOX, t, hh, lambda lo: limit + lo, hh, hh)
                pv = jnp.dot(v_aug(t * TK_FOX, hh), p_bufs[hh][pl.ds(dz, TK_FOX), :],
                             preferred_element_type=F32)
                acc_sc[hh] = acc_sc[hh] * al_sc[hh] + pv[0:FOX_ACC_ROWS]
            return carry

        lax.fori_loop(0, (qi + 1) * per, body, 0)

    fast_path()

    @pl.when(jnp.max(jump_sc[...]) > FOX_MAX_JUMP)
    def _():
        safe_path()


    o_t = jnp.concatenate(
        [acc_sc[hh, 0:HEAD_DIM, :] * (1.0 / acc_sc[hh, HEAD_DIM:HEAD_DIM + 1, :]) for hh in range(2)],
        axis=0)
    o_ref[...] = o_t.T.astype(BF16)


def _fox(proj, aug, base, dec_t, qv_t, batch, seq):
    t = proj.shape[0]
    nq = seq // TQ_FOX
    pairs = FOX_HEADS // 2
    return pl.pallas_call(
        _fox_kernel,
        grid=(batch, pairs, nq),
        in_specs=[
            pl.BlockSpec(memory_space=pltpu.SMEM),
            pl.BlockSpec((None, LANES, TQ_FOX), lambda b, p, i: (b, ROW_FQ // LANES + p, i)),
            pl.BlockSpec((seq, LANES), lambda b, p, i: (b, COL_FK // LANES + p)),
            pl.BlockSpec((seq, LANES), lambda b, p, i: (b, 0)),
            pl.BlockSpec((None, LANES, seq), lambda b, p, i: (b, ROW_FV // LANES + p, 0)),
            pl.BlockSpec((None, None, 2, seq), lambda b, p, i: (b, p, 0, 0)),
        ],
        out_specs=pl.BlockSpec((TQ_FOX, LANES), lambda b, p, i: (b * nq + i, p)),
        out_shape=jax.ShapeDtypeStruct((t, FOX_W), BF16),
        scratch_shapes=[
            pltpu.VMEM((2, 1, TQ_FOX), F32),
            pltpu.VMEM((FOX_SLOTS, 1, TQ_FOX), F32),
            pltpu.VMEM((FOX_SLOTS, 1, TQ_FOX), F32),
            pltpu.VMEM((2, FOX_ACC_ROWS, TQ_FOX), F32),
            pltpu.VMEM((2, 2 * LANES, TQ_FOX), BF16),
            pltpu.VMEM((2, 1, TQ_FOX), F32),
            pltpu.VMEM((2, 1, TQ_FOX), F32),
        ] + [pltpu.VMEM((TK_FOX, TQ_FOX), F32)] * 2
          + [pltpu.VMEM((TK_FOX, TQ_FOX), BF16)] * FOX_SLOTS,
        compiler_params=pltpu.CompilerParams(
            dimension_semantics=("arbitrary", "arbitrary", "arbitrary")),
        name="fox",
    )(base, qv_t, proj, aug, qv_t, dec_t.reshape(batch, pairs, 2, seq))


def _merge_kernel(oa_ref, ob_ref, ga_ref, gb_ref, x_ref, wa_ref, wb_ref, wo_ref, gain_ref,
                  x1_ref, h2_ref):
    ya = jnp.dot(oa_ref[...], wa_ref[...], preferred_element_type=F32)
    yb = jnp.dot(ob_ref[...], wb_ref[...], preferred_element_type=F32)

    def gate(ref):
        return 1.0 / (1.0 + jnp.exp(-ref[...].astype(F32)))

    merged = gate(ga_ref) * ya + gate(gb_ref) * yb
    x1 = x_ref[...] + jnp.dot(merged.astype(BF16), wo_ref[...], preferred_element_type=F32)
    x1_ref[...] = x1
    h2_ref[...] = _rmsnorm_rows(x1, gain_ref[...]).astype(BF16)


def _merge(o_a, o_b, proj, x2, wa, wb, wo, gain):
    t = x2.shape[0]
    tm = TM_MERGE
    const = dict(pipeline_mode=pl.Buffered(1))
    return pl.pallas_call(
        _merge_kernel,
        grid=(t // tm,),
        in_specs=[
            pl.BlockSpec((tm, SWA_Q_W), lambda i: (i, 0)),
            pl.BlockSpec((tm, FOX_W), lambda i: (i, 0)),
            pl.BlockSpec((tm, D_MODEL), lambda i: (i, COL_GA // D_MODEL)),
            pl.BlockSpec((tm, D_MODEL), lambda i: (i, COL_GB // D_MODEL)),
            pl.BlockSpec((tm, D_MODEL), lambda i: (i, 0)),
            pl.BlockSpec((SWA_Q_W, D_MODEL), lambda i: (0, 0), **const),
            pl.BlockSpec((FOX_W, D_MODEL), lambda i: (0, 0), **const),
            pl.BlockSpec((D_MODEL, D_MODEL), lambda i: (0, 0), **const),
            pl.BlockSpec((1, D_MODEL), lambda i: (0, 0), **const),
        ],
        out_specs=[
            pl.BlockSpec((tm, D_MODEL), lambda i: (i, 0)),
            pl.BlockSpec((tm, D_MODEL), lambda i: (i, 0)),
        ],
        out_shape=[
            jax.ShapeDtypeStruct((t, D_MODEL), F32),
            jax.ShapeDtypeStruct((t, D_MODEL), BF16),
        ],
        compiler_params=pltpu.CompilerParams(
            dimension_semantics=("arbitrary",),
            vmem_limit_bytes=56 * MIB),
        name="merge",
    )(o_a, o_b, proj, proj, x2, wa, wb, wo, gain)


def _mlp_kernel(h2_ref, x1_ref, wup_ref, wdn_ref, gain_ref, o_ref, *, final_norm):
    f = pl.program_id(1)
    last = pl.num_programs(1) - 1

    def delta():
        u = jnp.maximum(jnp.dot(h2_ref[...], wup_ref[...], preferred_element_type=F32), 0.0)
        return jnp.dot((u * u).astype(BF16), wdn_ref[...], preferred_element_type=F32)

    @pl.when(f == 0)
    def _():
        o_ref[...] = x1_ref[...] + delta()

    @pl.when((f > 0) & (f < last))
    def _():
        o_ref[...] += delta()

    @pl.when(f == last)
    def _():
        y = o_ref[...] + delta()
        o_ref[...] = _rmsnorm_rows(y, gain_ref[...]) if final_norm else y


def _mlp(h2, x1, wup, wdn, gain, final_norm):
    t = x1.shape[0]
    tm, tf = TM_MLP, TF_MLP
    return pl.pallas_call(
        functools.partial(_mlp_kernel, final_norm=final_norm),
        grid=(t // tm, D_FF // tf),
        in_specs=[
            pl.BlockSpec((tm, D_MODEL), lambda i, f: (i, 0)),
            pl.BlockSpec((tm, D_MODEL), lambda i, f: (i, 0)),
            pl.BlockSpec((D_MODEL, tf), lambda i, f: (0, f)),
            pl.BlockSpec((tf, D_MODEL), lambda i, f: (f, 0)),
            pl.BlockSpec((1, D_MODEL), lambda i, f: (0, 0)),
        ],
        out_specs=pl.BlockSpec((tm, D_MODEL), lambda i, f: (i, 0)),
        out_shape=jax.ShapeDtypeStruct((t, D_MODEL), F32),
        compiler_params=pltpu.CompilerParams(
            dimension_semantics=("arbitrary", "arbitrary"),
            vmem_limit_bytes=48 * MIB),
        name="mlp",
    )(h2, x1, wup, wdn, gain)


def _pack_w_in(w, scale):
    o = 0
    a_q = w[:, o:o + SWA_Q_W] * (scale * LOG2E); o += SWA_Q_W
    a_k = w[:, o:o + SWA_KV_W]; o += SWA_KV_W
    a_v = w[:, o:o + SWA_KV_W]; o += SWA_KV_W
    f_q = w[:, o:o + FOX_W] * (scale * LOG2E); o += FOX_W
    f_k = w[:, o:o + FOX_W]; o += FOX_W
    f_v = w[:, o:o + FOX_W]; o += FOX_W
    f_l = w[:, o:o + FOX_HEADS]; o += FOX_HEADS
    g_a = w[:, o:o + D_MODEL]; o += D_MODEL
    g_b = w[:, o:o + D_MODEL]; o += D_MODEL
    w_tok = jnp.concatenate([g_a, g_b, f_k], axis=1).astype(BF16)
    w_qv = jnp.concatenate([a_q, f_v, f_q], axis=1).T.astype(BF16)
    w_kv = jnp.concatenate([a_k, a_v, f_l], axis=1).T.astype(BF16)
    return w_tok, w_qv, w_kv


def kernel(x, positions, attn_norm, w_in, fox_f_bias, swa_sinks, w_branch_swa, w_branch_fox,
           w_out, mlp_norm, w_up, w_down, final_norm):
    batch, seq, d = x.shape
    depth = w_in.shape[0]
    assert d == D_MODEL and seq % TM_IN == 0 and seq % TQ_FOX == 0
    t = batch * seq
    scale = HEAD_DIM ** -0.5
    inv_freq = ROPE_THETA ** (-jnp.arange(0, HEAD_DIM, 2, dtype=F32) / HEAD_DIM)
    invf = inv_freq[:, None]
    pos3 = positions.reshape(t // TM_IN, 1, TM_IN)
    x2 = x.reshape(t, d)
    for l in range(depth):
        w_tok, w_qv, w_kv = _pack_w_in(w_in[l], scale)
        fbias = fox_f_bias[l].astype(F32)[:, None]
        proj, logf_t, qv_t, kv_t = _inproj(x2, pos3, invf, attn_norm[l][None, :].astype(F32),
                                           w_tok, w_qv, w_kv, fbias, batch, seq)
        aug, base, dec_t = _decay(logf_t, batch, seq)
        base = base.reshape(batch, -1, FOX_HEADS, 2, DECAY_BLOCKS).transpose(0, 1, 4, 2, 3).reshape(-1)
        o_a = _swa(qv_t, kv_t, swa_sinks[l].astype(F32), batch, seq)
        o_b = _fox(proj, aug, base, dec_t, qv_t, batch, seq)
        x1, h2 = _merge(o_a, o_b, proj, x2,
                        w_branch_swa[l].astype(BF16), w_branch_fox[l].astype(BF16),
                        w_out[l].astype(BF16), mlp_norm[l][None, :].astype(F32))
        x2 = _mlp(h2, x1, w_up[l].astype(BF16), w_down[l].astype(BF16),
                  final_norm[None, :].astype(F32), final_norm=(l == depth - 1))
    return x2.reshape(batch, seq, d)
```

```python
import functools

import jax
import jax.numpy as jnp
from jax import lax
from jax.experimental import pallas as pl
from jax.experimental.pallas import tpu as pltpu

F32 = jnp.float32
BF16 = jnp.bfloat16

D_MODEL = 2048
HEAD_DIM = 64
HALF = HEAD_DIM // 2
SWA_Q_HEADS = 16
SWA_KV_HEADS = 2
SWA_GROUP = SWA_Q_HEADS // SWA_KV_HEADS
WINDOW = 128
FOX_HEADS = 16
D_FF = 4 * D_MODEL
ROPE_THETA = 10000.0
RMS_EPS = 1e-6
SWA_Q_W = SWA_Q_HEADS * HEAD_DIM
FOX_W = FOX_HEADS * HEAD_DIM

LANES = 128
NEG = -1e30
MIB = 1024 * 1024
LOG2E = 1.4426950408889634

COL_GA = 0
COL_GB = 2048
COL_FK = 4096
D_PROJ = 5120
ROW_AQ = 0
ROW_FV = SWA_Q_W
ROW_FQ = SWA_Q_W + FOX_W
D_QV = ROW_FQ + FOX_W
SWA_KV_W = SWA_KV_HEADS * HEAD_DIM

TM_IN = 1024
TN_IN = 1024
NJ_TOK = D_PROJ // TN_IN
J_AQ = NJ_TOK
J_KV = J_AQ + SWA_Q_W // TN_IN
J_FEAT = J_KV + 1
NJ_IN = J_FEAT + (D_QV - ROW_FV) // TN_IN
assert 2 * (NJ_TOK - 1) >= TM_IN // LANES

TQ_SWA = 512
TQ_FOX = 2048
TK_FOX = 512
FOX_AHEAD = 2
FOX_CHAINS = 2 * (TQ_FOX // TK_FOX)
FOX_SLOTS = 4
assert FOX_CHAINS % FOX_SLOTS == 0 and FOX_AHEAD == 2 and FOX_AHEAD < FOX_SLOTS
FOX_ACC_ROWS = HEAD_DIM + 8
FOX_MAX_JUMP = 100.0
FOX_ROWS = 32
FOX_PIECES = TQ_FOX // 256
FOX_DEN_ROWS = 64
SWA_DEN_ROWS = 16
DECAY_BLOCKS = 4
TM_MERGE = 512
TM_MLP = 512
TF_MLP = 1024


def _rmsnorm_rows(x, gain):
    ms = jnp.mean(x * x, axis=-1, keepdims=True)
    return x * lax.rsqrt(ms + RMS_EPS) * gain


def _split3(x):
    hi = x.astype(BF16)
    r1 = x - hi.astype(F32)
    mid = r1.astype(BF16)
    lo = (r1 - mid.astype(F32)).astype(BF16)
    return hi, mid, lo


def _inproj_kernel(pos_ref, invf_ref, x_ref, gain_ref, w_ref, wqv_ref, wkv_ref, fbias_ref,
                   proj_ref, logf_ref, qv_ref, kv_ref, h_sc, hn_sc, cos_sc, sin_sc):
    i = pl.program_id(0)
    j = pl.program_id(1)
    nchunk = TM_IN // LANES

    def norm_chunk(dst, c):
        rows = pl.ds(pl.multiple_of(c * LANES, LANES), LANES)
        dst[rows, :] = _rmsnorm_rows(x_ref[rows, :], gain_ref[...]).astype(BF16)

    @pl.when(j == 0)
    def _():
        @pl.when(i == 0)
        def _():
            def body(r, _):
                norm_chunk(h_sc, r)
                return 0
            lax.fori_loop(0, nchunk, body, 0)

        @pl.when(i > 0)
        def _():
            h_sc[...] = hn_sc[...]

        ang = invf_ref[...] * pos_ref[...].astype(F32)
        cos_sc[...] = jnp.cos(ang)
        sin_sc[...] = jnp.sin(ang)

    def features(wt_ref):
        return lax.dot_general(wt_ref[...], h_sc[...], (((1,), (1,)), ((), ())),
                               preferred_element_type=F32)

    def rope_rows(acc, heads, out_ref):
        for hd in range(heads):
            r0 = hd * HEAD_DIM
            t1 = acc[r0:r0 + HALF, :]
            t2 = acc[r0 + HALF:r0 + HEAD_DIM, :]
            out_ref[r0:r0 + HALF, :] = (t1 * cos_sc[...] - t2 * sin_sc[...]).astype(BF16)
            out_ref[r0 + HALF:r0 + HEAD_DIM, :] = (t2 * cos_sc[...] + t1 * sin_sc[...]).astype(BF16)

    @pl.when(j < NJ_TOK)
    def _():
        proj_ref[...] = jnp.dot(h_sc[...], w_ref[...], preferred_element_type=F32).astype(BF16)
        c0 = jnp.clip(2 * (j - (i == 0).astype(jnp.int32)), 0, nchunk - 2)
        norm_chunk(hn_sc, c0)
        norm_chunk(hn_sc, c0 + 1)

    @pl.when((j >= J_AQ) & (j < J_KV))
    def _():
        rope_rows(features(wqv_ref), TN_IN // HEAD_DIM, qv_ref)

    @pl.when(j == J_KV)
    def _():
        acc = features(wkv_ref)
        rope_rows(acc, SWA_KV_HEADS, kv_ref)
        kv_ref[SWA_KV_W:2 * SWA_KV_W, :] = acc[SWA_KV_W:2 * SWA_KV_W, :].astype(BF16)
        z = acc[2 * SWA_KV_W:, :] + fbias_ref[...]
        logf_ref[...] = jnp.minimum(z, 0.0) - jnp.log1p(jnp.exp(-jnp.abs(z)))

    @pl.when(j >= J_FEAT)
    def _():
        qv_ref[...] = features(wqv_ref).astype(BF16)


def _inproj(x2, pos3, invf, gain, w_tok, w_qv, w_kv, fbias, batch, seq):
    t = x2.shape[0]
    nsb = seq // TM_IN
    kv_rows = 2 * SWA_KV_W

    def qv_blk(j):
        return jnp.where(j >= J_FEAT, j - J_FEAT + ROW_FV // TN_IN,
                         jnp.clip(j - J_AQ, 0, SWA_Q_W // TN_IN - 1))

    def tok_blk(j):
        return jnp.minimum(j, NJ_TOK - 1)

    def x_blk(i, j):
        return jnp.where((i == 0) & (j == 0), 0, jnp.minimum(i + 1, t // TM_IN - 1))

    return pl.pallas_call(
        _inproj_kernel,
        grid=(t // TM_IN, NJ_IN),
        in_specs=[
            pl.BlockSpec((None, 1, TM_IN), lambda i, j: (i, 0, 0)),
            pl.BlockSpec((HALF, 1), lambda i, j: (0, 0)),
            pl.BlockSpec((TM_IN, D_MODEL), lambda i, j: (x_blk(i, j), 0)),
            pl.BlockSpec((1, D_MODEL), lambda i, j: (0, 0)),
            pl.BlockSpec((D_MODEL, TN_IN), lambda i, j: (0, tok_blk(j))),
            pl.BlockSpec((TN_IN, D_MODEL), lambda i, j: (qv_blk(j), 0)),
            pl.BlockSpec((kv_rows + FOX_HEADS, D_MODEL), lambda i, j: (0, 0)),
            pl.BlockSpec((FOX_HEADS, 1), lambda i, j: (0, 0)),
        ],
        out_specs=[
            pl.BlockSpec((TM_IN, TN_IN), lambda i, j: (i, tok_blk(j))),
            pl.BlockSpec((None, FOX_HEADS, TM_IN), lambda i, j: (i // nsb, 0, i % nsb)),
            pl.BlockSpec((None, TN_IN, TM_IN), lambda i, j: (i // nsb, qv_blk(j), i % nsb)),
            pl.BlockSpec((None, kv_rows, TM_IN), lambda i, j: (i // nsb, 0, i % nsb)),
        ],
        out_shape=[
            jax.ShapeDtypeStruct((t, D_PROJ), BF16),
            jax.ShapeDtypeStruct((batch, FOX_HEADS, seq), F32),
            jax.ShapeDtypeStruct((batch, D_QV, seq), BF16),
            jax.ShapeDtypeStruct((batch, kv_rows, seq), BF16),
        ],
        scratch_shapes=[
            pltpu.VMEM((TM_IN, D_MODEL), BF16),
            pltpu.VMEM((TM_IN, D_MODEL), BF16),
            pltpu.VMEM((HALF, TM_IN), F32),
            pltpu.VMEM((HALF, TM_IN), F32),
        ],
        compiler_params=pltpu.CompilerParams(
            dimension_semantics=("arbitrary", "arbitrary"),
            vmem_limit_bytes=57 * MIB),
        name="inproj",
    )(pos3, invf, x2, gain, w_tok, w_qv, w_kv, fbias)


def _decay_kernel(logf_ref, aug_ref, base_ref, dec_ref, carry_sc):
    sb = pl.program_id(1)

    @pl.when(sb == 0)
    def _():
        carry_sc[...] = jnp.zeros(carry_sc.shape, F32)

    row = lax.broadcasted_iota(jnp.int32, (TK_FOX, TK_FOX), 0)
    col = lax.broadcasted_iota(jnp.int32, (TK_FOX, TK_FOX), 1)
    upper = jnp.where(row <= col, 1.0, 0.0).astype(BF16)
    pad = jnp.zeros((LANES - 3 * FOX_HEADS, TK_FOX), F32)
    for c in range(DECAY_BLOCKS):
        toks = slice(c * TK_FOX, (c + 1) * TK_FOX)
        hi, mid, lo = _split3(logf_ref[:, toks] * LOG2E)
        cl = (jnp.dot(hi, upper, preferred_element_type=F32)
              + jnp.dot(mid, upper, preferred_element_type=F32)
              + jnp.dot(lo, upper, preferred_element_type=F32))
        yh, ym, yl = _split3(-cl)
        pieces_t = jnp.concatenate([yh.astype(F32), ym.astype(F32), yl.astype(F32), pad], axis=0)
        aug_ref[toks, :] = pieces_t.T.astype(BF16)
        dec_ref[:, toks] = -cl
        total = cl[:, TK_FOX - 1:TK_FOX]
        base_ref[:, c:c + 1] = -carry_sc[...]
        base_ref[:, DECAY_BLOCKS + c:DECAY_BLOCKS + c + 1] = -total
        carry_sc[...] = carry_sc[...] + total


def _decay(logf_t, batch, seq):
    toks = DECAY_BLOCKS * TK_FOX
    ns = seq // toks
    return pl.pallas_call(
        _decay_kernel,
        grid=(batch, ns),
        in_specs=[pl.BlockSpec((None, FOX_HEADS, toks), lambda b, s: (b, 0, s))],
        out_specs=[
            pl.BlockSpec((toks, LANES), lambda b, s: (b * ns + s, 0)),
            pl.BlockSpec((None, None, FOX_HEADS, 2 * DECAY_BLOCKS), lambda b, s: (b, s, 0, 0)),
            pl.BlockSpec((None, FOX_HEADS, toks), lambda b, s: (b, 0, s)),
        ],
        out_shape=[
            jax.ShapeDtypeStruct((batch * seq, LANES), BF16),
            jax.ShapeDtypeStruct((batch, ns, FOX_HEADS, 2 * DECAY_BLOCKS), F32),
            jax.ShapeDtypeStruct((batch, FOX_HEADS, seq), F32),
        ],
        scratch_shapes=[pltpu.VMEM((FOX_HEADS, 1), F32)],
        compiler_params=pltpu.CompilerParams(
            dimension_semantics=("arbitrary", "arbitrary")),
        name="decay",
    )(logf_t)


def _swa_kernel(sinks_ref, qt_ref, cur_ref, prev_ref, o_ref, ktok_sc):
    i = pl.program_id(1)
    nblk = TQ_SWA // WINDOW
    gw = SWA_GROUP * WINDOW
    kv_win = jnp.concatenate([prev_ref[...], cur_ref[...]], axis=1)
    ktok_sc[...] = kv_win[0:SWA_KV_W, :].astype(F32).T.astype(BF16)

    row = lax.broadcasted_iota(jnp.int32, (2 * WINDOW, gw), 0)
    qix = lax.broadcasted_iota(jnp.int32, (2 * WINDOW, gw), 1) % WINDOW
    allowed = (row > qix) & (row <= qix + WINDOW)
    bias_any = jnp.where(allowed, 0.0, NEG)
    bias_first = jnp.where(allowed & (row >= WINDOW), 0.0, NEG)
    bias_r0 = jnp.where(i == 0, bias_first, bias_any)
    zeros_q = jnp.zeros((HEAD_DIM, WINDOW), BF16)
    ones_rows = jnp.ones((SWA_DEN_ROWS, 2 * WINDOW), BF16)

    units = [(r, g) for r in range(nblk) for g in range(SWA_KV_HEADS)]

    def scores(unit):
        r, g = unit
        cols = slice(r * WINDOW, (r + 1) * WINDOW)
        tiles = []
        for hq in range(SWA_GROUP):
            h = g * SWA_GROUP + hq
            q_h = qt_ref[h * HEAD_DIM:(h + 1) * HEAD_DIM, cols]
            tiles.append(jnp.concatenate([q_h, zeros_q] if g == 0 else [zeros_q, q_h], axis=0))
        q_g = jnp.concatenate(tiles, axis=1)
        k_win = ktok_sc[r * WINDOW:(r + 2) * WINDOW, :]
        s = jnp.dot(k_win, q_g, preferred_element_type=F32)
        return s + (bias_r0 if r == 0 else bias_any)

    def finish(s, unit):
        r, g = unit
        sink = jnp.concatenate(
            [jnp.full((1, WINDOW), sinks_ref[g * SWA_GROUP + hq] * LOG2E, F32)
             for hq in range(SWA_GROUP)], axis=1)
        m = jnp.maximum(jnp.max(s, axis=0, keepdims=True), sink)
        pr = jnp.exp2(s - m).astype(BF16)
        v_rows = slice(SWA_KV_W + g * HEAD_DIM, SWA_KV_W + (g + 1) * HEAD_DIM)
        vt = jnp.concatenate([kv_win[v_rows, r * WINDOW:(r + 2) * WINDOW], ones_rows], axis=0)
        acc = jnp.dot(vt, pr, preferred_element_type=F32)
        den = acc[HEAD_DIM:HEAD_DIM + 1, :] + jnp.exp2(sink - m)
        o_t = acc[0:HEAD_DIM, :] * (1.0 / den)
        for pp in range(SWA_GROUP // 2):
            pair = jnp.concatenate([o_t[:, (2 * pp) * WINDOW:(2 * pp + 1) * WINDOW],
                                    o_t[:, (2 * pp + 1) * WINDOW:(2 * pp + 2) * WINDOW]], axis=0)
            c0 = (g * SWA_GROUP + 2 * pp) * HEAD_DIM
            o_ref[r * WINDOW:(r + 1) * WINDOW, c0:c0 + LANES] = pair.T.astype(BF16)

    pending = scores(units[0])
    for n, unit in enumerate(units):
        nxt = scores(units[n + 1]) if n + 1 < len(units) else None
        finish(pending, unit)
        pending = nxt


def _swa(qv_t, kv_t, sinks, batch, seq):
    nsq = seq // TQ_SWA
    per = TQ_SWA // WINDOW
    return pl.pallas_call(
        _swa_kernel,
        grid=(batch, nsq),
        in_specs=[
            pl.BlockSpec(memory_space=pltpu.SMEM),
            pl.BlockSpec((None, SWA_Q_W, TQ_SWA), lambda b, i: (b, ROW_AQ // SWA_Q_W, i)),
            pl.BlockSpec((None, 2 * SWA_KV_W, TQ_SWA), lambda b, i: (b, 0, i)),
            pl.BlockSpec((None, 2 * SWA_KV_W, WINDOW),
                         lambda b, i: (b, 0, jnp.maximum(i * per - 1, 0))),
        ],
        out_specs=pl.BlockSpec((TQ_SWA, SWA_Q_W), lambda b, i: (b * nsq + i, 0)),
        scratch_shapes=[pltpu.VMEM((WINDOW + TQ_SWA, SWA_KV_W), BF16)],
        out_shape=jax.ShapeDtypeStruct((batch * seq, SWA_Q_W), BF16),
        compiler_params=pltpu.CompilerParams(
            dimension_semantics=("arbitrary", "arbitrary")),
        name="swa",
    )(sinks, qv_t, kv_t, kv_t)


def _fox_kernel(base_ref, q_ref, k_ref, aug_ref, vt_ref, dec_ref, o_ref,
                m_sc, al_sc, be_sc, acc_sc, qt_sc, bm_sc, jump_sc, *bufs):
    b = pl.program_id(0)
    p = pl.program_id(1)
    qi = pl.program_id(2)
    s_bufs, p_bufs = bufs[:2], bufs[2:]
    dz = pl.multiple_of(jnp.minimum(qi, 0), FOX_ROWS)
    nb = k_ref.shape[0] // TK_FOX
    per = TQ_FOX // TK_FOX
    q_t = q_ref[...]
    feat = lax.broadcasted_iota(jnp.int32, (LANES, TQ_FOX), 0)
    zero = jnp.zeros_like(q_t)
    for hh in range(2):
        h = 2 * p + hh
        sel = (feat == h) | (feat == h + FOX_HEADS) | (feat == h + 2 * FOX_HEADS)
        qt_sc[hh, 0:LANES, :] = jnp.where((feat // HEAD_DIM) == hh, q_t, zero)
        qt_sc[hh, LANES:2 * LANES, :] = jnp.where(sel, 1.0, 0.0).astype(BF16)

    ones_rows = jnp.ones((FOX_DEN_ROWS, TK_FOX), BF16)

    def reset():
        m_sc[...] = jnp.full(m_sc.shape, NEG, F32)
        acc_sc[...] = jnp.zeros(acc_sc.shape, F32)

    def pieces(q_lo):
        step = TQ_FOX // FOX_PIECES
        return [(max(q_lo, u * step), (u + 1) * step) for u in range(FOX_PIECES)
                if (u + 1) * step > q_lo]

    def key_rows(key0):
        return pl.ds(pl.multiple_of(key0, TK_FOX), TK_FOX)

    def base_of(kblk, hh, span=0):
        return base_ref[((b * nb + kblk) * FOX_HEADS + 2 * p + hh) * 2 + span]

    def qk(key0, hh, lo, hi, limit):
        ks = key_rows(key0)
        k_full = jnp.concatenate([k_ref[ks, :], aug_ref[ks, :]], axis=1)
        s = jnp.dot(k_full, qt_sc[hh, :, lo:hi], preferred_element_type=F32)
        if limit is not None:
            row = lax.broadcasted_iota(jnp.int32, s.shape, 0)
            col = lax.broadcasted_iota(jnp.int32, s.shape, 1)
            s = jnp.where(row - col <= limit, s, NEG)
        return s

    def v_aug(key0, hh):
        return jnp.concatenate(
            [vt_ref[hh * HEAD_DIM:(hh + 1) * HEAD_DIM, key_rows(key0)], ones_rows], axis=0)

    def two_pass_probs(key0, kblk, hh, limit_of, sb, pb):
        for lo, hi in pieces(0):
            s = qk(key0, hh, lo, hi, limit_of(lo))
            s_bufs[sb][pl.ds(dz, TK_FOX), lo:hi] = s
            bm_sc[sb, :, lo:hi] = jnp.max(s, axis=0, keepdims=True)
        base = base_of(kblk, hh)
        m_prev = m_sc[hh] - base
        m_new = jnp.maximum(m_prev, bm_sc[sb])
        al_sc[pb] = jnp.exp2(m_prev - m_new)
        m_sc[hh] = m_new + base
        for r in range(0, TK_FOX, FOX_ROWS):
            rows = pl.ds(dz + r, FOX_ROWS)
            pr = jnp.exp2(s_bufs[sb][rows, :] - m_new)
            p_bufs[pb][rows, :] = pr.astype(BF16)

    def chain_keys(blk, n):
        return blk * TQ_FOX + (n // 2) * TK_FOX, blk * per + n // 2

    def limit_fn(q_lo, diag):
        if diag is None:
            return lambda lo: None
        if diag is True:
            return lambda lo: lo - q_lo
        return lambda lo: jnp.where(diag, lo - q_lo, TK_FOX)

    def one_pass_piece(blk, n, q_lo, diag, ref, lo, hi):
        key0, _ = chain_keys(blk, n)
        s = qk(key0, n % 2, lo, hi, limit_fn(q_lo, diag)(lo))
        p_bufs[n % FOX_SLOTS][pl.ds(dz, TK_FOX), lo:hi] = jnp.exp2(
            s - ref[:, lo - q_lo:hi - q_lo]).astype(BF16)
        return jnp.max(s, axis=0, keepdims=True)

    def one_pass_finish(n, q_lo, m_prev, ref, base, maxes):
        hh, slot = n % 2, n % FOX_SLOTS
        blk_max = jnp.concatenate(maxes, axis=1) if len(maxes) > 1 else maxes[0]
        m_new = jnp.maximum(m_prev, blk_max)
        al_sc[slot, :, q_lo:] = jnp.exp2(m_prev - m_new)
        be_sc[slot, :, q_lo:] = jnp.exp2(ref - m_new)
        m_sc[hh, :, q_lo:] = m_new + base
        jump_sc[hh, :, q_lo:] = jnp.maximum(jump_sc[hh, :, q_lo:], jnp.abs(blk_max - ref))

    def pv_piece(blk, n, lo, hi):
        hh, slot = n % 2, n % FOX_SLOTS
        key0, _ = chain_keys(blk, n)
        pv = jnp.dot(v_aug(key0, hh), p_bufs[slot][pl.ds(dz, TK_FOX), lo:hi],
                     preferred_element_type=F32)[0:FOX_ACC_ROWS]
        acc_sc[hh, :, lo:hi] = (acc_sc[hh, :, lo:hi] * al_sc[slot, :, lo:hi]
                                + pv * be_sc[slot, :, lo:hi])

    def chain_step(blk, n, q_lo, ahead):
        mine = pieces(q_lo)
        theirs = []
        if ahead is not None:
            a_blk, a_n, a_lo, a_diag, a_first = ahead
            theirs = pieces(a_lo)
            a_kblk = chain_keys(a_blk, a_n)[1]
            a_base = base_of(a_kblk, a_n % 2)
            if a_first:
                a_prev = jnp.full((1, TQ_FOX - a_lo), NEG, F32)
                ref0 = 0.0
            else:
                a_prev = m_sc[a_n % 2, :, a_lo:] - a_base
                ref0 = a_prev
            span = base_of(a_kblk, a_n % 2, span=1)
            if a_diag is None:
                a_ref = ref0 + span
            else:
                reach = dec_ref[a_n % 2:a_n % 2 + 1, key_rows(chain_keys(a_blk, a_n)[0])]
                if TQ_FOX - a_lo > TK_FOX:
                    reach = jnp.concatenate(
                        [reach, jnp.full((1, TQ_FOX - a_lo - TK_FOX), span, F32)], axis=1)
                a_ref = ref0 + (reach if a_diag is True else jnp.where(a_diag, reach, span))
        maxes = []
        for u in range(max(len(mine), len(theirs))):
            if u < len(theirs):
                maxes.append(one_pass_piece(a_blk, a_n, a_lo, a_diag, a_ref, *theirs[u]))
            if u < len(mine):
                pv_piece(blk, n, *mine[u])
        if ahead is not None:
            one_pass_finish(a_n, a_lo, a_prev, a_ref, a_base, maxes)

    def fast_path():
        reset()
        jump_sc[...] = jnp.full(jump_sc.shape, NEG, F32)
        for n in range(FOX_AHEAD):
            chain_step(0, n, TQ_FOX, (0, n, 0, qi == 0, True))

        def body(j, carry):
            for n in range(FOX_CHAINS):
                m = n + FOX_AHEAD
                if m < FOX_CHAINS:
                    ahead = (j, m, 0, None, False)
                else:
                    ahead = (j + 1, m - FOX_CHAINS, 0, j + 1 == qi, False)
                chain_step(j, n, 0, ahead)
            return carry

        lax.fori_loop(0, qi, body, 0)
        for n in range(FOX_CHAINS):
            m = n + FOX_AHEAD
            ahead = (qi, m, (m // 2) * TK_FOX, True, False) if m < FOX_CHAINS else None
            chain_step(qi, n, (n // 2) * TK_FOX, ahead)

    def safe_path():
        reset()

        def body(t, carry):
            blk, c = t // per, t % per
            limit = jnp.where(blk == qi, -c * TK_FOX, TK_FOX)
            for hh in range(2):
                two_pass_probs(t * TK_FOX, t, hh, lambda lo: limit + lo, hh, hh)
                pv = jnp.dot(v_aug(t * TK_FOX, hh), p_bufs[hh][pl.ds(dz, TK_FOX), :],
                             preferred_element_type=F32)
                acc_sc[hh] = acc_sc[hh] * al_sc[hh] + pv[0:FOX_ACC_ROWS]
            return carry

        lax.fori_loop(0, (qi + 1) * per, body, 0)

    fast_path()

    @pl.when(jnp.max(jump_sc[...]) > FOX_MAX_JUMP)
    def _():
        safe_path()


    o_t = jnp.concatenate(
        [acc_sc[hh, 0:HEAD_DIM, :] * (1.0 / acc_sc[hh, HEAD_DIM:HEAD_DIM + 1, :]) for hh in range(2)],
        axis=0)
    o_ref[...] = o_t.T.astype(BF16)


def _fox(proj, aug, base, dec_t, qv_t, batch, seq):
    t = proj.shape[0]
    nq = seq // TQ_FOX
    pairs = FOX_HEADS // 2
    return pl.pallas_call(
        _fox_kernel,
        grid=(batch, pairs, nq),
        in_specs=[
            pl.BlockSpec(memory_space=pltpu.SMEM),
            pl.BlockSpec((None, LANES, TQ_FOX), lambda b, p, i: (b, ROW_FQ // LANES + p, i)),
            pl.BlockSpec((seq, LANES), lambda b, p, i: (b, COL_FK // LANES + p)),
            pl.BlockSpec((seq, LANES), lambda b, p, i: (b, 0)),
            pl.BlockSpec((None, LANES, seq), lambda b, p, i: (b, ROW_FV // LANES + p, 0)),
            pl.BlockSpec((None, None, 2, seq), lambda b, p, i: (b, p, 0, 0)),
        ],
        out_specs=pl.BlockSpec((TQ_FOX, LANES), lambda b, p, i: (b * nq + i, p)),
        out_shape=jax.ShapeDtypeStruct((t, FOX_W), BF16),
        scratch_shapes=[
            pltpu.VMEM((2, 1, TQ_FOX), F32),
            pltpu.VMEM((FOX_SLOTS, 1, TQ_FOX), F32),
            pltpu.VMEM((FOX_SLOTS, 1, TQ_FOX), F32),
            pltpu.VMEM((2, FOX_ACC_ROWS, TQ_FOX), F32),
            pltpu.VMEM((2, 2 * LANES, TQ_FOX), BF16),
            pltpu.VMEM((2, 1, TQ_FOX), F32),
            pltpu.VMEM((2, 1, TQ_FOX), F32),
        ] + [pltpu.VMEM((TK_FOX, TQ_FOX), F32)] * 2
          + [pltpu.VMEM((TK_FOX, TQ_FOX), BF16)] * FOX_SLOTS,
        compiler_params=pltpu.CompilerParams(
            dimension_semantics=("arbitrary", "arbitrary", "arbitrary")),
        name="fox",
    )(base, qv_t, proj, aug, qv_t, dec_t.reshape(batch, pairs, 2, seq))


def _merge_kernel(oa_ref, ob_ref, ga_ref, gb_ref, x_ref, wa_ref, wb_ref, wo_ref, gain_ref,
                  x1_ref, h2_ref):
    ya = jnp.dot(oa_ref[...], wa_ref[...], preferred_element_type=F32)
    yb = jnp.dot(ob_ref[...], wb_ref[...], preferred_element_type=F32)

    def gate(ref):
        return 1.0 / (1.0 + jnp.exp(-ref[...].astype(F32)))

    merged = gate(ga_ref) * ya + gate(gb_ref) * yb
    x1 = x_ref[...] + jnp.dot(merged.astype(BF16), wo_ref[...], preferred_element_type=F32)
    x1_ref[...] = x1
    h2_ref[...] = _rmsnorm_rows(x1, gain_ref[...]).astype(BF16)


def _merge(o_a, o_b, proj, x2, wa, wb, wo, gain):
    t = x2.shape[0]
    tm = TM_MERGE
    const = dict(pipeline_mode=pl.Buffered(1))
    return pl.pallas_call(
        _merge_kernel,
        grid=(t // tm,),
        in_specs=[
            pl.BlockSpec((tm, SWA_Q_W), lambda i: (i, 0)),
            pl.BlockSpec((tm, FOX_W), lambda i: (i, 0)),
            pl.BlockSpec((tm, D_MODEL), lambda i: (i, COL_GA // D_MODEL)),
            pl.BlockSpec((tm, D_MODEL), lambda i: (i, COL_GB // D_MODEL)),
            pl.BlockSpec((tm, D_MODEL), lambda i: (i, 0)),
            pl.BlockSpec((SWA_Q_W, D_MODEL), lambda i: (0, 0), **const),
            pl.BlockSpec((FOX_W, D_MODEL), lambda i: (0, 0), **const),
            pl.BlockSpec((D_MODEL, D_MODEL), lambda i: (0, 0), **const),
            pl.BlockSpec((1, D_MODEL), lambda i: (0, 0), **const),
        ],
        out_specs=[
            pl.BlockSpec((tm, D_MODEL), lambda i: (i, 0)),
            pl.BlockSpec((tm, D_MODEL), lambda i: (i, 0)),
        ],
        out_shape=[
            jax.ShapeDtypeStruct((t, D_MODEL), F32),
            jax.ShapeDtypeStruct((t, D_MODEL), BF16),
        ],
        compiler_params=pltpu.CompilerParams(
            dimension_semantics=("arbitrary",),
            vmem_limit_bytes=56 * MIB),
        name="merge",
    )(o_a, o_b, proj, proj, x2, wa, wb, wo, gain)


def _mlp_kernel(h2_ref, x1_ref, wup_ref, wdn_ref, gain_ref, o_ref, *, final_norm):
    f = pl.program_id(1)
    last = pl.num_programs(1) - 1

    def delta():
        u = jnp.maximum(jnp.dot(h2_ref[...], wup_ref[...], preferred_element_type=F32), 0.0)
        return jnp.dot((u * u).astype(BF16), wdn_ref[...], preferred_element_type=F32)

    @pl.when(f == 0)
    def _():
        o_ref[...] = x1_ref[...] + delta()

    @pl.when((f > 0) & (f < last))
    def _():
        o_ref[...] += delta()

    @pl.when(f == last)
    def _():
        y = o_ref[...] + delta()
        o_ref[...] = _rmsnorm_rows(y, gain_ref[...]) if final_norm else y


def _mlp(h2, x1, wup, wdn, gain, final_norm):
    t = x1.shape[0]
    tm, tf = TM_MLP, TF_MLP
    return pl.pallas_call(
        functools.partial(_mlp_kernel, final_norm=final_norm),
        grid=(t // tm, D_FF // tf),
        in_specs=[
            pl.BlockSpec((tm, D_MODEL), lambda i, f: (i, 0)),
            pl.BlockSpec((tm, D_MODEL), lambda i, f: (i, 0)),
            pl.BlockSpec((D_MODEL, tf), lambda i, f: (0, f)),
            pl.BlockSpec((tf, D_MODEL), lambda i, f: (f, 0)),
            pl.BlockSpec((1, D_MODEL), lambda i, f: (0, 0)),
        ],
        out_specs=pl.BlockSpec((tm, D_MODEL), lambda i, f: (i, 0)),
        out_shape=jax.ShapeDtypeStruct((t, D_MODEL), F32),
        compiler_params=pltpu.CompilerParams(
            dimension_semantics=("arbitrary", "arbitrary"),
            vmem_limit_bytes=48 * MIB),
        name="mlp",
    )(h2, x1, wup, wdn, gain)


def _pack_w_in(w, scale):
    o = 0
    a_q = w[:, o:o + SWA_Q_W] * (scale * LOG2E); o += SWA_Q_W
    a_k = w[:, o:o + SWA_KV_W]; o += SWA_KV_W
    a_v = w[:, o:o + SWA_KV_W]; o += SWA_KV_W
    f_q = w[:, o:o + FOX_W] * (scale * LOG2E); o += FOX_W
    f_k = w[:, o:o + FOX_W]; o += FOX_W
    f_v = w[:, o:o + FOX_W]; o += FOX_W
    f_l = w[:, o:o + FOX_HEADS]; o += FOX_HEADS
    g_a = w[:, o:o + D_MODEL]; o += D_MODEL
    g_b = w[:, o:o + D_MODEL]; o += D_MODEL
    w_tok = jnp.concatenate([g_a, g_b, f_k], axis=1).astype(BF16)
    w_qv = jnp.concatenate([a_q, f_v, f_q], axis=1).T.astype(BF16)
    w_kv = jnp.concatenate([a_k, a_v, f_l], axis=1).T.astype(BF16)
    return w_tok, w_qv, w_kv


def kernel(x, positions, attn_norm, w_in, fox_f_bias, swa_sinks, w_branch_swa, w_branch_fox,
           w_out, mlp_norm, w_up, w_down, final_norm):
    batch, seq, d = x.shape
    depth = w_in.shape[0]
    assert d == D_MODEL and seq % TM_IN == 0 and seq % TQ_FOX == 0
    t = batch * seq
    scale = HEAD_DIM ** -0.5
    inv_freq = ROPE_THETA ** (-jnp.arange(0, HEAD_DIM, 2, dtype=F32) / HEAD_DIM)
    invf = inv_freq[:, None]
    pos3 = positions.reshape(t // TM_IN, 1, TM_IN)
    x2 = x.reshape(t, d)
    for l in range(depth):
        w_tok, w_qv, w_kv = _pack_w_in(w_in[l], scale)
        fbias = fox_f_bias[l].astype(F32)[:, None]
        proj, logf_t, qv_t, kv_t = _inproj(x2, pos3, invf, attn_norm[l][None, :].astype(F32),
                                           w_tok, w_qv, w_kv, fbias, batch, seq)
        aug, base, dec_t = _decay(logf_t, batch, seq)
        base = base.reshape(batch, -1, FOX_HEADS, 2, DECAY_BLOCKS).transpose(0, 1, 4, 2, 3).reshape(-1)
        o_a = _swa(qv_t, kv_t, swa_sinks[l].astype(F32), batch, seq)
        o_b = _fox(proj, aug, base, dec_t, qv_t, batch, seq)
        x1, h2 = _merge(o_a, o_b, proj, x2,
                        w_branch_swa[l].astype(BF16), w_branch_fox[l].astype(BF16),
                        w_out[l].astype(BF16), mlp_norm[l][None, :].astype(F32))
        x2 = _mlp(h2, x1, w_up[l].astype(BF16), w_down[l].astype(BF16),
                  final_norm[None, :].astype(F32), final_norm=(l == depth - 1))
    return x2.reshape(batch, seq, d)
```

```python
import functools

import jax
import jax.numpy as jnp
from jax import lax
from jax.experimental import pallas as pl
from jax.experimental.pallas import tpu as pltpu

F32 = jnp.float32
BF16 = jnp.bfloat16

D_MODEL = 2048
HEAD_DIM = 64
HALF = HEAD_DIM // 2
SWA_Q_HEADS = 16
SWA_KV_HEADS = 2
SWA_GROUP = SWA_Q_HEADS // SWA_KV_HEADS
WINDOW = 128
FOX_HEADS = 16
D_FF = 4 * D_MODEL
ROPE_THETA = 10000.0
RMS_EPS = 1e-6
SWA_Q_W = SWA_Q_HEADS * HEAD_DIM
FOX_W = FOX_HEADS * HEAD_DIM

LANES = 128
NEG = -1e30
MIB = 1024 * 1024
LOG2E = 1.4426950408889634

COL_GA = 0
COL_GB = 2048
COL_FK = 4096
D_PROJ = 5120
ROW_AQ = 0
ROW_FV = SWA_Q_W
ROW_FQ = SWA_Q_W + FOX_W
D_QV = ROW_FQ + FOX_W
SWA_KV_W = SWA_KV_HEADS * HEAD_DIM

TM_IN = 1024
TN_IN = 1024
NJ_TOK = D_PROJ // TN_IN
J_AQ = NJ_TOK
J_KV = J_AQ + SWA_Q_W // TN_IN
J_FEAT = J_KV + 1
NJ_IN = J_FEAT + (D_QV - ROW_FV) // TN_IN
assert 2 * (NJ_TOK - 1) >= TM_IN // LANES

TQ_SWA = 1024
TQ_FOX = 2048
TK_FOX = 512
FOX_AHEAD = 2
FOX_CHAINS = 2 * (TQ_FOX // TK_FOX)
FOX_SLOTS = 4
assert FOX_CHAINS % FOX_SLOTS == 0 and FOX_AHEAD == 2 and FOX_AHEAD < FOX_SLOTS
FOX_ACC_ROWS = HEAD_DIM + 8
FOX_MAX_JUMP = 100.0
FOX_ROWS = 32
FOX_PIECES = TQ_FOX // 256
FOX_DEN_ROWS = 64
SWA_DEN_ROWS = 16
DECAY_BLOCKS = 4
TM_MERGE = 512
TM_MLP = 512
TF_MLP = 1024


def _rmsnorm_rows(x, gain):
    ms = jnp.mean(x * x, axis=-1, keepdims=True)
    return x * lax.rsqrt(ms + RMS_EPS) * gain


def _split3(x):
    hi = x.astype(BF16)
    r1 = x - hi.astype(F32)
    mid = r1.astype(BF16)
    lo = (r1 - mid.astype(F32)).astype(BF16)
    return hi, mid, lo


def _inproj_kernel(pos_ref, invf_ref, x_ref, gain_ref, w_ref, wqv_ref, wkv_ref, fbias_ref,
                   proj_ref, logf_ref, qv_ref, kv_ref, h_sc, hn_sc, cos_sc, sin_sc):
    i = pl.program_id(0)
    j = pl.program_id(1)
    nchunk = TM_IN // LANES

    def norm_chunk(dst, c):
        rows = pl.ds(pl.multiple_of(c * LANES, LANES), LANES)
        dst[rows, :] = _rmsnorm_rows(x_ref[rows, :], gain_ref[...]).astype(BF16)

    @pl.when(j == 0)
    def _():
        @pl.when(i == 0)
        def _():
            def body(r, _):
                norm_chunk(h_sc, r)
                return 0
            lax.fori_loop(0, nchunk, body, 0)

        @pl.when(i > 0)
        def _():
            h_sc[...] = hn_sc[...]

        ang = invf_ref[...] * pos_ref[...].astype(F32)
        cos_sc[...] = jnp.cos(ang)
        sin_sc[...] = jnp.sin(ang)

    def features(wt_ref):
        return lax.dot_general(wt_ref[...], h_sc[...], (((1,), (1,)), ((), ())),
                               preferred_element_type=F32)

    def rope_rows(acc, heads, out_ref):
        for hd in range(heads):
            r0 = hd * HEAD_DIM
            t1 = acc[r0:r0 + HALF, :]
            t2 = acc[r0 + HALF:r0 + HEAD_DIM, :]
            out_ref[r0:r0 + HALF, :] = (t1 * cos_sc[...] - t2 * sin_sc[...]).astype(BF16)
            out_ref[r0 + HALF:r0 + HEAD_DIM, :] = (t2 * cos_sc[...] + t1 * sin_sc[...]).astype(BF16)

    @pl.when(j < NJ_TOK)
    def _():
        proj_ref[...] = jnp.dot(h_sc[...], w_ref[...], preferred_element_type=F32).astype(BF16)
        c0 = jnp.clip(2 * (j - (i == 0).astype(jnp.int32)), 0, nchunk - 2)
        norm_chunk(hn_sc, c0)
        norm_chunk(hn_sc, c0 + 1)

    @pl.when((j >= J_AQ) & (j < J_KV))
    def _():
        rope_rows(features(wqv_ref), TN_IN // HEAD_DIM, qv_ref)

    @pl.when(j == J_KV)
    def _():
        acc = features(wkv_ref)
        rope_rows(acc, SWA_KV_HEADS, kv_ref)
        kv_ref[SWA_KV_W:2 * SWA_KV_W, :] = acc[SWA_KV_W:2 * SWA_KV_W, :].astype(BF16)
        z = acc[2 * SWA_KV_W:, :] + fbias_ref[...]
        logf_ref[...] = jnp.minimum(z, 0.0) - jnp.log1p(jnp.exp(-jnp.abs(z)))

    @pl.when(j >= J_FEAT)
    def _():
        qv_ref[...] = features(wqv_ref).astype(BF16)


def _inproj(x2, pos3, invf, gain, w_tok, w_qv, w_kv, fbias, batch, seq):
    t = x2.shape[0]
    nsb = seq // TM_IN
    kv_rows = 2 * SWA_KV_W

    def qv_blk(j):
        return jnp.where(j >= J_FEAT, j - J_FEAT + ROW_FV // TN_IN,
                         jnp.clip(j - J_AQ, 0, SWA_Q_W // TN_IN - 1))

    def tok_blk(j):
        return jnp.minimum(j, NJ_TOK - 1)

    def x_blk(i, j):
        return jnp.where((i == 0) & (j == 0), 0, jnp.minimum(i + 1, t // TM_IN - 1))

    return pl.pallas_call(
        _inproj_kernel,
        grid=(t // TM_IN, NJ_IN),
        in_specs=[
            pl.BlockSpec((None, 1, TM_IN), lambda i, j: (i, 0, 0)),
            pl.BlockSpec((HALF, 1), lambda i, j: (0, 0)),
            pl.BlockSpec((TM_IN, D_MODEL), lambda i, j: (x_blk(i, j), 0)),
            pl.BlockSpec((1, D_MODEL), lambda i, j: (0, 0)),
            pl.BlockSpec((D_MODEL, TN_IN), lambda i, j: (0, tok_blk(j))),
            pl.BlockSpec((TN_IN, D_MODEL), lambda i, j: (qv_blk(j), 0)),
            pl.BlockSpec((kv_rows + FOX_HEADS, D_MODEL), lambda i, j: (0, 0)),
            pl.BlockSpec((FOX_HEADS, 1), lambda i, j: (0, 0)),
        ],
        out_specs=[
            pl.BlockSpec((TM_IN, TN_IN), lambda i, j: (i, tok_blk(j))),
            pl.BlockSpec((None, FOX_HEADS, TM_IN), lambda i, j: (i // nsb, 0, i % nsb)),
            pl.BlockSpec((None, TN_IN, TM_IN), lambda i, j: (i // nsb, qv_blk(j), i % nsb)),
            pl.BlockSpec((None, kv_rows, TM_IN), lambda i, j: (i // nsb, 0, i % nsb)),
        ],
        out_shape=[
            jax.ShapeDtypeStruct((t, D_PROJ), BF16),
            jax.ShapeDtypeStruct((batch, FOX_HEADS, seq), F32),
            jax.ShapeDtypeStruct((batch, D_QV, seq), BF16),
            jax.ShapeDtypeStruct((batch, kv_rows, seq), BF16),
        ],
        scratch_shapes=[
            pltpu.VMEM((TM_IN, D_MODEL), BF16),
            pltpu.VMEM((TM_IN, D_MODEL), BF16),
            pltpu.VMEM((HALF, TM_IN), F32),
            pltpu.VMEM((HALF, TM_IN), F32),
        ],
        compiler_params=pltpu.CompilerParams(
            dimension_semantics=("arbitrary", "arbitrary"),
            vmem_limit_bytes=57 * MIB),
        name="inproj",
    )(pos3, invf, x2, gain, w_tok, w_qv, w_kv, fbias)


def _decay_kernel(logf_ref, aug_ref, base_ref, dec_ref, carry_sc):
    sb = pl.program_id(1)

    @pl.when(sb == 0)
    def _():
        carry_sc[...] = jnp.zeros(carry_sc.shape, F32)

    row = lax.broadcasted_iota(jnp.int32, (TK_FOX, TK_FOX), 0)
    col = lax.broadcasted_iota(jnp.int32, (TK_FOX, TK_FOX), 1)
    upper = jnp.where(row <= col, 1.0, 0.0).astype(BF16)
    pad = jnp.zeros((LANES - 3 * FOX_HEADS, TK_FOX), F32)
    for c in range(DECAY_BLOCKS):
        toks = slice(c * TK_FOX, (c + 1) * TK_FOX)
        hi, mid, lo = _split3(logf_ref[:, toks] * LOG2E)
        cl = (jnp.dot(hi, upper, preferred_element_type=F32)
              + jnp.dot(mid, upper, preferred_element_type=F32)
              + jnp.dot(lo, upper, preferred_element_type=F32))
        yh, ym, yl = _split3(-cl)
        pieces_t = jnp.concatenate([yh.astype(F32), ym.astype(F32), yl.astype(F32), pad], axis=0)
        aug_ref[toks, :] = pieces_t.T.astype(BF16)
        dec_ref[:, toks] = -cl
        total = cl[:, TK_FOX - 1:TK_FOX]
        base_ref[:, c:c + 1] = -carry_sc[...]
        base_ref[:, DECAY_BLOCKS + c:DECAY_BLOCKS + c + 1] = -total
        carry_sc[...] = carry_sc[...] + total


def _decay(logf_t, batch, seq):
    toks = DECAY_BLOCKS * TK_FOX
    ns = seq // toks
    return pl.pallas_call(
        _decay_kernel,
        grid=(batch, ns),
        in_specs=[pl.BlockSpec((None, FOX_HEADS, toks), lambda b, s: (b, 0, s))],
        out_specs=[
            pl.BlockSpec((toks, LANES), lambda b, s: (b * ns + s, 0)),
            pl.BlockSpec((None, None, FOX_HEADS, 2 * DECAY_BLOCKS), lambda b, s: (b, s, 0, 0)),
            pl.BlockSpec((None, FOX_HEADS, toks), lambda b, s: (b, 0, s)),
        ],
        out_shape=[
            jax.ShapeDtypeStruct((batch * seq, LANES), BF16),
            jax.ShapeDtypeStruct((batch, ns, FOX_HEADS, 2 * DECAY_BLOCKS), F32),
            jax.ShapeDtypeStruct((batch, FOX_HEADS, seq), F32),
        ],
        scratch_shapes=[pltpu.VMEM((FOX_HEADS, 1), F32)],
        compiler_params=pltpu.CompilerParams(
            dimension_semantics=("arbitrary", "arbitrary")),
        name="decay",
    )(logf_t)


def _swa_kernel(sinks_ref, qt_ref, cur_ref, prev_ref, o_ref, ktok_sc):
    i = pl.program_id(1)
    nblk = TQ_SWA // WINDOW
    gw = SWA_GROUP * WINDOW
    kv_win = jnp.concatenate([prev_ref[...], cur_ref[...]], axis=1)
    ktok_sc[...] = kv_win[0:SWA_KV_W, :].astype(F32).T.astype(BF16)

    row = lax.broadcasted_iota(jnp.int32, (2 * WINDOW, gw), 0)
    qix = lax.broadcasted_iota(jnp.int32, (2 * WINDOW, gw), 1) % WINDOW
    allowed = (row > qix) & (row <= qix + WINDOW)
    bias_any = jnp.where(allowed, 0.0, NEG)
    bias_first = jnp.where(allowed & (row >= WINDOW), 0.0, NEG)
    bias_r0 = jnp.where(i == 0, bias_first, bias_any)
    zeros_q = jnp.zeros((HEAD_DIM, WINDOW), BF16)
    ones_rows = jnp.ones((SWA_DEN_ROWS, 2 * WINDOW), BF16)

    units = [(r, g) for r in range(nblk) for g in range(SWA_KV_HEADS)]

    def scores(unit):
        r, g = unit
        cols = slice(r * WINDOW, (r + 1) * WINDOW)
        tiles = []
        for hq in range(SWA_GROUP):
            h = g * SWA_GROUP + hq
            q_h = qt_ref[h * HEAD_DIM:(h + 1) * HEAD_DIM, cols]
            tiles.append(jnp.concatenate([q_h, zeros_q] if g == 0 else [zeros_q, q_h], axis=0))
        q_g = jnp.concatenate(tiles, axis=1)
        k_win = ktok_sc[r * WINDOW:(r + 2) * WINDOW, :]
        s = jnp.dot(k_win, q_g, preferred_element_type=F32)
        return s + (bias_r0 if r == 0 else bias_any)

    def finish(s, unit):
        r, g = unit
        sink = jnp.concatenate(
            [jnp.full((1, WINDOW), sinks_ref[g * SWA_GROUP + hq] * LOG2E, F32)
             for hq in range(SWA_GROUP)], axis=1)
        m = jnp.maximum(jnp.max(s, axis=0, keepdims=True), sink)
        pr = jnp.exp2(s - m).astype(BF16)
        v_rows = slice(SWA_KV_W + g * HEAD_DIM, SWA_KV_W + (g + 1) * HEAD_DIM)
        vt = jnp.concatenate([kv_win[v_rows, r * WINDOW:(r + 2) * WINDOW], ones_rows], axis=0)
        acc = jnp.dot(vt, pr, preferred_element_type=F32)
        den = acc[HEAD_DIM:HEAD_DIM + 1, :] + jnp.exp2(sink - m)
        o_t = acc[0:HEAD_DIM, :] * (1.0 / den)
        for pp in range(SWA_GROUP // 2):
            pair = jnp.concatenate([o_t[:, (2 * pp) * WINDOW:(2 * pp + 1) * WINDOW],
                                    o_t[:, (2 * pp + 1) * WINDOW:(2 * pp + 2) * WINDOW]], axis=0)
            c0 = (g * SWA_GROUP + 2 * pp) * HEAD_DIM
            o_ref[r * WINDOW:(r + 1) * WINDOW, c0:c0 + LANES] = pair.T.astype(BF16)

    pending = scores(units[0])
    for n, unit in enumerate(units):
        nxt = scores(units[n + 1]) if n + 1 < len(units) else None
        finish(pending, unit)
        pending = nxt


def _swa(qv_t, kv_t, sinks, batch, seq):
    nsq = seq // TQ_SWA
    per = TQ_SWA // WINDOW
    return pl.pallas_call(
        _swa_kernel,
        grid=(batch, nsq),
        in_specs=[
            pl.BlockSpec(memory_space=pltpu.SMEM),
            pl.BlockSpec((None, SWA_Q_W, TQ_SWA), lambda b, i: (b, ROW_AQ // SWA_Q_W, i)),
            pl.BlockSpec((None, 2 * SWA_KV_W, TQ_SWA), lambda b, i: (b, 0, i)),
            pl.BlockSpec((None, 2 * SWA_KV_W, WINDOW),
                         lambda b, i: (b, 0, jnp.maximum(i * per - 1, 0))),
        ],
        out_specs=pl.BlockSpec((TQ_SWA, SWA_Q_W), lambda b, i: (b * nsq + i, 0)),
        scratch_shapes=[pltpu.VMEM((WINDOW + TQ_SWA, SWA_KV_W), BF16)],
        out_shape=jax.ShapeDtypeStruct((batch * seq, SWA_Q_W), BF16),
        compiler_params=pltpu.CompilerParams(
            dimension_semantics=("arbitrary", "arbitrary")),
        name="swa",
    )(sinks, qv_t, kv_t, kv_t)


def _fox_kernel(base_ref, q_ref, k_ref, aug_ref, vt_ref, dec_ref, o_ref,
                m_sc, al_sc, be_sc, acc_sc, qt_sc, bm_sc, jump_sc, *bufs):
    b = pl.program_id(0)
    p = pl.program_id(1)
    qi = pl.program_id(2)
    s_bufs, p_bufs = bufs[:2], bufs[2:]
    dz = pl.multiple_of(jnp.minimum(qi, 0), FOX_ROWS)
    nb = k_ref.shape[0] // TK_FOX
    per = TQ_FOX // TK_FOX
    q_t = q_ref[...]
    feat = lax.broadcasted_iota(jnp.int32, (LANES, TQ_FOX), 0)
    zero = jnp.zeros_like(q_t)
    for hh in range(2):
        h = 2 * p + hh
        sel = (feat == h) | (feat == h + FOX_HEADS) | (feat == h + 2 * FOX_HEADS)
        qt_sc[hh, 0:LANES, :] = jnp.where((feat // HEAD_DIM) == hh, q_t, zero)
        qt_sc[hh, LANES:2 * LANES, :] = jnp.where(sel, 1.0, 0.0).astype(BF16)

    ones_rows = jnp.ones((FOX_DEN_ROWS, TK_FOX), BF16)

    def reset():
        m_sc[...] = jnp.full(m_sc.shape, NEG, F32)
        acc_sc[...] = jnp.zeros(acc_sc.shape, F32)

    def pieces(q_lo):
        step = TQ_FOX // FOX_PIECES
        return [(max(q_lo, u * step), (u + 1) * step) for u in range(FOX_PIECES)
                if (u + 1) * step > q_lo]

    def key_rows(key0):
        return pl.ds(pl.multiple_of(key0, TK_FOX), TK_FOX)

    def base_of(kblk, hh, span=0):
        return base_ref[((b * nb + kblk) * FOX_HEADS + 2 * p + hh) * 2 + span]

    def qk(key0, hh, lo, hi, limit):
        ks = key_rows(key0)
        k_full = jnp.concatenate([k_ref[ks, :], aug_ref[ks, :]], axis=1)
        s = jnp.dot(k_full, qt_sc[hh, :, lo:hi], preferred_element_type=F32)
        if limit is not None:
            row = lax.broadcasted_iota(jnp.int32, s.shape, 0)
            col = lax.broadcasted_iota(jnp.int32, s.shape, 1)
            s = jnp.where(row - col <= limit, s, NEG)
        return s

    def v_aug(key0, hh):
        return jnp.concatenate(
            [vt_ref[hh * HEAD_DIM:(hh + 1) * HEAD_DIM, key_rows(key0)], ones_rows], axis=0)

    def two_pass_probs(key0, kblk, hh, limit_of, sb, pb):
        for lo, hi in pieces(0):
            s = qk(key0, hh, lo, hi, limit_of(lo))
            s_bufs[sb][pl.ds(dz, TK_FOX), lo:hi] = s
            bm_sc[sb, :, lo:hi] = jnp.max(s, axis=0, keepdims=True)
        base = base_of(kblk, hh)
        m_prev = m_sc[hh] - base
        m_new = jnp.maximum(m_prev, bm_sc[sb])
        al_sc[pb] = jnp.exp2(m_prev - m_new)
        m_sc[hh] = m_new + base
        for r in range(0, TK_FOX, FOX_ROWS):
            rows = pl.ds(dz + r, FOX_ROWS)
            pr = jnp.exp2(s_bufs[sb][rows, :] - m_new)
            p_bufs[pb][rows, :] = pr.astype(BF16)

    def chain_keys(blk, n):
        return blk * TQ_FOX + (n // 2) * TK_FOX, blk * per + n // 2

    def limit_fn(q_lo, diag):
        if diag is None:
            return lambda lo: None
        if diag is True:
            return lambda lo: lo - q_lo
        return lambda lo: jnp.where(diag, lo - q_lo, TK_FOX)

    def one_pass_piece(blk, n, q_lo, diag, ref, lo, hi):
        key0, _ = chain_keys(blk, n)
        s = qk(key0, n % 2, lo, hi, limit_fn(q_lo, diag)(lo))
        p_bufs[n % FOX_SLOTS][pl.ds(dz, TK_FOX), lo:hi] = jnp.exp2(
            s - ref[:, lo - q_lo:hi - q_lo]).astype(BF16)
        return jnp.max(s, axis=0, keepdims=True)

    def one_pass_finish(n, q_lo, m_prev, ref, base, maxes):
        hh, slot = n % 2, n % FOX_SLOTS
        blk_max = jnp.concatenate(maxes, axis=1) if len(maxes) > 1 else maxes[0]
        m_new = jnp.maximum(m_prev, blk_max)
        al_sc[slot, :, q_lo:] = jnp.exp2(m_prev - m_new)
        be_sc[slot, :, q_lo:] = jnp.exp2(ref - m_new)
        m_sc[hh, :, q_lo:] = m_new + base
        jump_sc[hh, :, q_lo:] = jnp.maximum(jump_sc[hh, :, q_lo:], jnp.abs(blk_max - ref))

    def pv_piece(blk, n, lo, hi):
        hh, slot = n % 2, n % FOX_SLOTS
        key0, _ = chain_keys(blk, n)
        pv = jnp.dot(v_aug(key0, hh), p_bufs[slot][pl.ds(dz, TK_FOX), lo:hi],
                     preferred_element_type=F32)[0:FOX_ACC_ROWS]
        acc_sc[hh, :, lo:hi] = (acc_sc[hh, :, lo:hi] * al_sc[slot, :, lo:hi]
                                + pv * be_sc[slot, :, lo:hi])

    def chain_step(blk, n, q_lo, ahead):
        mine = pieces(q_lo)
        theirs = []
        if ahead is not None:
            a_blk, a_n, a_lo, a_diag, a_first = ahead
            theirs = pieces(a_lo)
            a_kblk = chain_keys(a_blk, a_n)[1]
            a_base = base_of(a_kblk, a_n % 2)
            if a_first:
                a_prev = jnp.full((1, TQ_FOX - a_lo), NEG, F32)
                ref0 = 0.0
            else:
                a_prev = m_sc[a_n % 2, :, a_lo:] - a_base
                ref0 = a_prev
            span = base_of(a_kblk, a_n % 2, span=1)
            if a_diag is None:
                a_ref = ref0 + span
            else:
                reach = dec_ref[a_n % 2:a_n % 2 + 1, key_rows(chain_keys(a_blk, a_n)[0])]
                if TQ_FOX - a_lo > TK_FOX:
                    reach = jnp.concatenate(
                        [reach, jnp.full((1, TQ_FOX - a_lo - TK_FOX), span, F32)], axis=1)
                a_ref = ref0 + (reach if a_diag is True else jnp.where(a_diag, reach, span))
        maxes = []
        for u in range(max(len(mine), len(theirs))):
            if u < len(theirs):
                maxes.append(one_pass_piece(a_blk, a_n, a_lo, a_diag, a_ref, *theirs[u]))
            if u < len(mine):
                pv_piece(blk, n, *mine[u])
        if ahead is not None:
            one_pass_finish(a_n, a_lo, a_prev, a_ref, a_base, maxes)

    def fast_path():
        reset()
        jump_sc[...] = jnp.full(jump_sc.shape, NEG, F32)
        for n in range(FOX_AHEAD):
            chain_step(0, n, TQ_FOX, (0, n, 0, qi == 0, True))

        def body(j, carry):
            for n in range(FOX_CHAINS):
                m = n + FOX_AHEAD
                if m < FOX_CHAINS:
                    ahead = (j, m, 0, None, False)
                else:
                    ahead = (j + 1, m - FOX_CHAINS, 0, j + 1 == qi, False)
                chain_step(j, n, 0, ahead)
            return carry

        lax.fori_loop(0, qi, body, 0)
        for n in range(FOX_CHAINS):
            m = n + FOX_AHEAD
            ahead = (qi, m, (m // 2) * TK_FOX, True, False) if m < FOX_CHAINS else None
            chain_step(qi, n, (n // 2) * TK_FOX, ahead)

    def safe_path():
        reset()

        def body(t, carry):
            blk, c = t // per, t % per
            limit = jnp.where(blk == qi, -c * TK_FOX, TK_FOX)
            for hh in range(2):
                two_pass_probs(t * TK_FOX, t, hh, lambda lo: limit + lo, hh, hh)
                pv = jnp.dot(v_aug(t * TK_FOX, hh), p_bufs[hh][pl.ds(dz, TK_FOX), :],
                             preferred_element_type=F32)
                acc_sc[hh] = acc_sc[hh] * al_sc[hh] + pv[0:FOX_ACC_ROWS]
            return carry

        lax.fori_loop(0, (qi + 1) * per, body, 0)

    fast_path()

    @pl.when(jnp.max(jump_sc[...]) > FOX_MAX_JUMP)
    def _():
        safe_path()


    o_t = jnp.concatenate(
        [acc_sc[hh, 0:HEAD_DIM, :] * (1.0 / acc_sc[hh, HEAD_DIM:HEAD_DIM + 1, :]) for hh in range(2)],
        axis=0)
    o_ref[...] = o_t.T.astype(BF16)


def _fox(proj, aug, base, dec_t, qv_t, batch, seq):
    t = proj.shape[0]
    nq = seq // TQ_FOX
    pairs = FOX_HEADS // 2
    return pl.pallas_call(
        _fox_kernel,
        grid=(batch, pairs, nq),
        in_specs=[
            pl.BlockSpec(memory_space=pltpu.SMEM),
            pl.BlockSpec((None, LANES, TQ_FOX), lambda b, p, i: (b, ROW_FQ // LANES + p, i)),
            pl.BlockSpec((seq, LANES), lambda b, p, i: (b, COL_FK // LANES + p)),
            pl.BlockSpec((seq, LANES), lambda b, p, i: (b, 0)),
            pl.BlockSpec((None, LANES, seq), lambda b, p, i: (b, ROW_FV // LANES + p, 0)),
            pl.BlockSpec((None, None, 2, seq), lambda b, p, i: (b, p, 0, 0)),
        ],
        out_specs=pl.BlockSpec((TQ_FOX, LANES), lambda b, p, i: (b * nq + i, p)),
        out_shape=jax.ShapeDtypeStruct((t, FOX_W), BF16),
        scratch_shapes=[
            pltpu.VMEM((2, 1, TQ_FOX), F32),
            pltpu.VMEM((FOX_SLOTS, 1, TQ_FOX), F32),
            pltpu.VMEM((FOX_SLOTS, 1, TQ_FOX), F32),
            pltpu.VMEM((2, FOX_ACC_ROWS, TQ_FOX), F32),
            pltpu.VMEM((2, 2 * LANES, TQ_FOX), BF16),
            pltpu.VMEM((2, 1, TQ_FOX), F32),
            pltpu.VMEM((2, 1, TQ_FOX), F32),
        ] + [pltpu.VMEM((TK_FOX, TQ_FOX), F32)] * 2
          + [pltpu.VMEM((TK_FOX, TQ_FOX), BF16)] * FOX_SLOTS,
        compiler_params=pltpu.CompilerParams(
            dimension_semantics=("arbitrary", "arbitrary", "arbitrary")),
        name="fox",
    )(base, qv_t, proj, aug, qv_t, dec_t.reshape(batch, pairs, 2, seq))


def _merge_kernel(oa_ref, ob_ref, ga_ref, gb_ref, x_ref, wa_ref, wb_ref, wo_ref, gain_ref,
                  x1_ref, h2_ref):
    ya = jnp.dot(oa_ref[...], wa_ref[...], preferred_element_type=F32)
    yb = jnp.dot(ob_ref[...], wb_ref[...], preferred_element_type=F32)

    def gate(ref):
        return 1.0 / (1.0 + jnp.exp(-ref[...].astype(F32)))

    merged = gate(ga_ref) * ya + gate(gb_ref) * yb
    x1 = x_ref[...] + jnp.dot(merged.astype(BF16), wo_ref[...], preferred_element_type=F32)
    x1_ref[...] = x1
    h2_ref[...] = _rmsnorm_rows(x1, gain_ref[...]).astype(BF16)


def _merge(o_a, o_b, proj, x2, wa, wb, wo, gain):
    t = x2.shape[0]
    tm = TM_MERGE
    const = dict(pipeline_mode=pl.Buffered(1))
    return pl.pallas_call(
        _merge_kernel,
        grid=(t // tm,),
        in_specs=[
            pl.BlockSpec((tm, SWA_Q_W), lambda i: (i, 0)),
            pl.BlockSpec((tm, FOX_W), lambda i: (i, 0)),
            pl.BlockSpec((tm, D_MODEL), lambda i: (i, COL_GA // D_MODEL)),
            pl.BlockSpec((tm, D_MODEL), lambda i: (i, COL_GB // D_MODEL)),
            pl.BlockSpec((tm, D_MODEL), lambda i: (i, 0)),
            pl.BlockSpec((SWA_Q_W, D_MODEL), lambda i: (0, 0), **const),
            pl.BlockSpec((FOX_W, D_MODEL), lambda i: (0, 0), **const),
            pl.BlockSpec((D_MODEL, D_MODEL), lambda i: (0, 0), **const),
            pl.BlockSpec((1, D_MODEL), lambda i: (0, 0), **const),
        ],
        out_specs=[
            pl.BlockSpec((tm, D_MODEL), lambda i: (i, 0)),
            pl.BlockSpec((tm, D_MODEL), lambda i: (i, 0)),
        ],
        out_shape=[
            jax.ShapeDtypeStruct((t, D_MODEL), F32),
            jax.ShapeDtypeStruct((t, D_MODEL), BF16),
        ],
        compiler_params=pltpu.CompilerParams(
            dimension_semantics=("arbitrary",),
            vmem_limit_bytes=56 * MIB),
        name="merge",
    )(o_a, o_b, proj, proj, x2, wa, wb, wo, gain)


def _mlp_kernel(h2_ref, x1_ref, wup_ref, wdn_ref, gain_ref, o_ref, *, final_norm):
    f = pl.program_id(1)
    last = pl.num_programs(1) - 1

    def delta():
        u = jnp.maximum(jnp.dot(h2_ref[...], wup_ref[...], preferred_element_type=F32), 0.0)
        return jnp.dot((u * u).astype(BF16), wdn_ref[...], preferred_element_type=F32)

    @pl.when(f == 0)
    def _():
        o_ref[...] = x1_ref[...] + delta()

    @pl.when((f > 0) & (f < last))
    def _():
        o_ref[...] += delta()

    @pl.when(f == last)
    def _():
        y = o_ref[...] + delta()
        o_ref[...] = _rmsnorm_rows(y, gain_ref[...]) if final_norm else y


def _mlp(h2, x1, wup, wdn, gain, final_norm):
    t = x1.shape[0]
    tm, tf = TM_MLP, TF_MLP
    return pl.pallas_call(
        functools.partial(_mlp_kernel, final_norm=final_norm),
        grid=(t // tm, D_FF // tf),
        in_specs=[
            pl.BlockSpec((tm, D_MODEL), lambda i, f: (i, 0)),
            pl.BlockSpec((tm, D_MODEL), lambda i, f: (i, 0)),
            pl.BlockSpec((D_MODEL, tf), lambda i, f: (0, f)),
            pl.BlockSpec((tf, D_MODEL), lambda i, f: (f, 0)),
            pl.BlockSpec((1, D_MODEL), lambda i, f: (0, 0)),
        ],
        out_specs=pl.BlockSpec((tm, D_MODEL), lambda i, f: (i, 0)),
        out_shape=jax.ShapeDtypeStruct((t, D_MODEL), F32),
        compiler_params=pltpu.CompilerParams(
            dimension_semantics=("arbitrary", "arbitrary"),
            vmem_limit_bytes=48 * MIB),
        name="mlp",
    )(h2, x1, wup, wdn, gain)


def _pack_w_in(w, scale):
    o = 0
    a_q = w[:, o:o + SWA_Q_W] * (scale * LOG2E); o += SWA_Q_W
    a_k = w[:, o:o + SWA_KV_W]; o += SWA_KV_W
    a_v = w[:, o:o + SWA_KV_W]; o += SWA_KV_W
    f_q = w[:, o:o + FOX_W] * (scale * LOG2E); o += FOX_W
    f_k = w[:, o:o + FOX_W]; o += FOX_W
    f_v = w[:, o:o + FOX_W]; o += FOX_W
    f_l = w[:, o:o + FOX_HEADS]; o += FOX_HEADS
    g_a = w[:, o:o + D_MODEL]; o += D_MODEL
    g_b = w[:, o:o + D_MODEL]; o += D_MODEL
    w_tok = jnp.concatenate([g_a, g_b, f_k], axis=1).astype(BF16)
    w_qv = jnp.concatenate([a_q, f_v, f_q], axis=1).T.astype(BF16)
    w_kv = jnp.concatenate([a_k, a_v, f_l], axis=1).T.astype(BF16)
    return w_tok, w_qv, w_kv


def kernel(x, positions, attn_norm, w_in, fox_f_bias, swa_sinks, w_branch_swa, w_branch_fox,
           w_out, mlp_norm, w_up, w_down, final_norm):
    batch, seq, d = x.shape
    depth = w_in.shape[0]
    assert d == D_MODEL and seq % TM_IN == 0 and seq % TQ_FOX == 0
    t = batch * seq
    scale = HEAD_DIM ** -0.5
    inv_freq = ROPE_THETA ** (-jnp.arange(0, HEAD_DIM, 2, dtype=F32) / HEAD_DIM)
    invf = inv_freq[:, None]
    pos3 = positions.reshape(t // TM_IN, 1, TM_IN)
    x2 = x.reshape(t, d)
    for l in range(depth):
        w_tok, w_qv, w_kv = _pack_w_in(w_in[l], scale)
        fbias = fox_f_bias[l].astype(F32)[:, None]
        proj, logf_t, qv_t, kv_t = _inproj(x2, pos3, invf, attn_norm[l][None, :].astype(F32),
                                           w_tok, w_qv, w_kv, fbias, batch, seq)
        aug, base, dec_t = _decay(logf_t, batch, seq)
        base = base.reshape(batch, -1, FOX_HEADS, 2, DECAY_BLOCKS).transpose(0, 1, 4, 2, 3).reshape(-1)
        o_a = _swa(qv_t, kv_t, swa_sinks[l].astype(F32), batch, seq)
        o_b = _fox(proj, aug, base, dec_t, qv_t, batch, seq)
        x1, h2 = _merge(o_a, o_b, proj, x2,
                        w_branch_swa[l].astype(BF16), w_branch_fox[l].astype(BF16),
                        w_out[l].astype(BF16), mlp_norm[l][None, :].astype(F32))
        x2 = _mlp(h2, x1, w_up[l].astype(BF16), w_down[l].astype(BF16),
                  final_norm[None, :].astype(F32), final_norm=(l == depth - 1))
    return x2.reshape(batch, seq, d)
```

```python
import functools

import jax
import jax.numpy as jnp
from jax import lax
from jax.experimental import pallas as pl
from jax.experimental.pallas import tpu as pltpu

F32 = jnp.float32
BF16 = jnp.bfloat16

D_MODEL = 2048
HEAD_DIM = 64
HALF = HEAD_DIM // 2
SWA_Q_HEADS = 16
SWA_KV_HEADS = 2
SWA_GROUP = SWA_Q_HEADS // SWA_KV_HEADS
WINDOW = 128
FOX_HEADS = 16
D_FF = 4 * D_MODEL
ROPE_THETA = 10000.0
RMS_EPS = 1e-6
SWA_Q_W = SWA_Q_HEADS * HEAD_DIM
FOX_W = FOX_HEADS * HEAD_DIM

LANES = 128
NEG = -1e30
MIB = 1024 * 1024
LOG2E = 1.4426950408889634

COL_GA = 0
COL_GB = 2048
COL_FK = 4096
D_PROJ = 5120
ROW_AQ = 0
ROW_FV = SWA_Q_W
ROW_FQ = SWA_Q_W + FOX_W
D_QV = ROW_FQ + FOX_W
SWA_KV_W = SWA_KV_HEADS * HEAD_DIM

TM_IN = 1024
TN_IN = 1024
NJ_TOK = D_PROJ // TN_IN
J_AQ = NJ_TOK
J_KV = J_AQ + SWA_Q_W // TN_IN
J_FEAT = J_KV + 1
NJ_IN = J_FEAT + (D_QV - ROW_FV) // TN_IN
assert 2 * (NJ_TOK - 1) >= TM_IN // LANES

TQ_SWA = 1024
TQ_FOX = 2048
TK_FOX = 512
FOX_AHEAD = 2
FOX_CHAINS = 2 * (TQ_FOX // TK_FOX)
FOX_SLOTS = 4
assert FOX_CHAINS % FOX_SLOTS == 0 and FOX_AHEAD == 2 and FOX_AHEAD < FOX_SLOTS
FOX_ACC_ROWS = HEAD_DIM + 8
FOX_MAX_JUMP = 100.0
FOX_ROWS = 32
FOX_PIECES = TQ_FOX // 256
FOX_DEN_ROWS = 64
SWA_DEN_ROWS = 16
DECAY_BLOCKS = 4
TM_MERGE = 512
TM_MLP = 512
TF_MLP = 1024


def _rmsnorm_rows(x, gain):
    ms = jnp.mean(x * x, axis=-1, keepdims=True)
    return x * lax.rsqrt(ms + RMS_EPS) * gain


def _split3(x):
    hi = x.astype(BF16)
    r1 = x - hi.astype(F32)
    mid = r1.astype(BF16)
    lo = (r1 - mid.astype(F32)).astype(BF16)
    return hi, mid, lo


def _inproj_kernel(pos_ref, invf_ref, x_ref, gain_ref, w_ref, wqv_ref, wkv_ref, fbias_ref,
                   proj_ref, logf_ref, qv_ref, kv_ref, h_sc, hn_sc, cos_sc, sin_sc):
    i = pl.program_id(0)
    j = pl.program_id(1)
    nchunk = TM_IN // LANES

    def norm_chunk(dst, c):
        rows = pl.ds(pl.multiple_of(c * LANES, LANES), LANES)
        dst[rows, :] = _rmsnorm_rows(x_ref[rows, :], gain_ref[...]).astype(BF16)

    @pl.when(j == 0)
    def _():
        @pl.when(i == 0)
        def _():
            def body(r, _):
                norm_chunk(h_sc, r)
                return 0
            lax.fori_loop(0, nchunk, body, 0)

        @pl.when(i > 0)
        def _():
            h_sc[...] = hn_sc[...]

        ang = invf_ref[...] * pos_ref[...].astype(F32)
        cos_sc[...] = jnp.cos(ang)
        sin_sc[...] = jnp.sin(ang)

    def features(wt_ref):
        return lax.dot_general(wt_ref[...], h_sc[...], (((1,), (1,)), ((), ())),
                               preferred_element_type=F32)

    def rope_rows(acc, heads, out_ref):
        for hd in range(heads):
            r0 = hd * HEAD_DIM
            t1 = acc[r0:r0 + HALF, :]
            t2 = acc[r0 + HALF:r0 + HEAD_DIM, :]
            out_ref[r0:r0 + HALF, :] = (t1 * cos_sc[...] - t2 * sin_sc[...]).astype(BF16)
            out_ref[r0 + HALF:r0 + HEAD_DIM, :] = (t2 * cos_sc[...] + t1 * sin_sc[...]).astype(BF16)

    @pl.when(j < NJ_TOK)
    def _():
        proj_ref[...] = jnp.dot(h_sc[...], w_ref[...], preferred_element_type=F32).astype(BF16)
        c0 = jnp.clip(2 * (j - (i == 0).astype(jnp.int32)), 0, nchunk - 2)
        norm_chunk(hn_sc, c0)
        norm_chunk(hn_sc, c0 + 1)

    @pl.when((j >= J_AQ) & (j < J_KV))
    def _():
        rope_rows(features(wqv_ref), TN_IN // HEAD_DIM, qv_ref)

    @pl.when(j == J_KV)
    def _():
        acc = features(wkv_ref)
        rope_rows(acc, SWA_KV_HEADS, kv_ref)
        kv_ref[SWA_KV_W:2 * SWA_KV_W, :] = acc[SWA_KV_W:2 * SWA_KV_W, :].astype(BF16)
        z = acc[2 * SWA_KV_W:, :] + fbias_ref[...]
        logf_ref[...] = jnp.minimum(z, 0.0) - jnp.log1p(jnp.exp(-jnp.abs(z)))

    @pl.when(j >= J_FEAT)
    def _():
        qv_ref[...] = features(wqv_ref).astype(BF16)


def _inproj(x2, pos3, invf, gain, w_tok, w_qv, w_kv, fbias, batch, seq):
    t = x2.shape[0]
    nsb = seq // TM_IN
    kv_rows = 2 * SWA_KV_W

    def qv_blk(j):
        return jnp.where(j >= J_FEAT, j - J_FEAT + ROW_FV // TN_IN,
                         jnp.clip(j - J_AQ, 0, SWA_Q_W // TN_IN - 1))

    def tok_blk(j):
        return jnp.minimum(j, NJ_TOK - 1)

    def x_blk(i, j):
        return jnp.where((i == 0) & (j == 0), 0, jnp.minimum(i + 1, t // TM_IN - 1))

    return pl.pallas_call(
        _inproj_kernel,
        grid=(t // TM_IN, NJ_IN),
        in_specs=[
            pl.BlockSpec((None, 1, TM_IN), lambda i, j: (i, 0, 0)),
            pl.BlockSpec((HALF, 1), lambda i, j: (0, 0)),
            pl.BlockSpec((TM_IN, D_MODEL), lambda i, j: (x_blk(i, j), 0)),
            pl.BlockSpec((1, D_MODEL), lambda i, j: (0, 0)),
            pl.BlockSpec((D_MODEL, TN_IN), lambda i, j: (0, tok_blk(j))),
            pl.BlockSpec((TN_IN, D_MODEL), lambda i, j: (qv_blk(j), 0)),
            pl.BlockSpec((kv_rows + FOX_HEADS, D_MODEL), lambda i, j: (0, 0)),
            pl.BlockSpec((FOX_HEADS, 1), lambda i, j: (0, 0)),
        ],
        out_specs=[
            pl.BlockSpec((TM_IN, TN_IN), lambda i, j: (i, tok_blk(j))),
            pl.BlockSpec((None, FOX_HEADS, TM_IN), lambda i, j: (i // nsb, 0, i % nsb)),
            pl.BlockSpec((None, TN_IN, TM_IN), lambda i, j: (i // nsb, qv_blk(j), i % nsb)),
            pl.BlockSpec((None, kv_rows, TM_IN), lambda i, j: (i // nsb, 0, i % nsb)),
        ],
        out_shape=[
            jax.ShapeDtypeStruct((t, D_PROJ), BF16),
            jax.ShapeDtypeStruct((batch, FOX_HEADS, seq), F32),
            jax.ShapeDtypeStruct((batch, D_QV, seq), BF16),
            jax.ShapeDtypeStruct((batch, kv_rows, seq), BF16),
        ],
        scratch_shapes=[
            pltpu.VMEM((TM_IN, D_MODEL), BF16),
            pltpu.VMEM((TM_IN, D_MODEL), BF16),
            pltpu.VMEM((HALF, TM_IN), F32),
            pltpu.VMEM((HALF, TM_IN), F32),
        ],
        compiler_params=pltpu.CompilerParams(
            dimension_semantics=("arbitrary", "arbitrary"),
            vmem_limit_bytes=57 * MIB),
        name="inproj",
    )(pos3, invf, x2, gain, w_tok, w_qv, w_kv, fbias)


def _decay_kernel(logf_ref, aug_ref, base_ref, dec_ref, carry_sc):
    sb = pl.program_id(1)

    @pl.when(sb == 0)
    def _():
        carry_sc[...] = jnp.zeros(carry_sc.shape, F32)

    row = lax.broadcasted_iota(jnp.int32, (TK_FOX, TK_FOX), 0)
    col = lax.broadcasted_iota(jnp.int32, (TK_FOX, TK_FOX), 1)
    upper = jnp.where(row <= col, 1.0, 0.0).astype(BF16)
    pad = jnp.zeros((LANES - 3 * FOX_HEADS, TK_FOX), F32)
    for c in range(DECAY_BLOCKS):
        toks = slice(c * TK_FOX, (c + 1) * TK_FOX)
        hi, mid, lo = _split3(logf_ref[:, toks] * LOG2E)
        cl = (jnp.dot(hi, upper, preferred_element_type=F32)
              + jnp.dot(mid, upper, preferred_element_type=F32)
              + jnp.dot(lo, upper, preferred_element_type=F32))
        yh, ym, yl = _split3(-cl)
        pieces_t = jnp.concatenate([yh.astype(F32), ym.astype(F32), yl.astype(F32), pad], axis=0)
        aug_ref[toks, :] = pieces_t.T.astype(BF16)
        dec_ref[:, toks] = -cl
        total = cl[:, TK_FOX - 1:TK_FOX]
        base_ref[:, c:c + 1] = -carry_sc[...]
        base_ref[:, DECAY_BLOCKS + c:DECAY_BLOCKS + c + 1] = -total
        carry_sc[...] = carry_sc[...] + total


def _decay(logf_t, batch, seq):
    toks = DECAY_BLOCKS * TK_FOX
    ns = seq // toks
    return pl.pallas_call(
        _decay_kernel,
        grid=(batch, ns),
        in_specs=[pl.BlockSpec((None, FOX_HEADS, toks), lambda b, s: (b, 0, s))],
        out_specs=[
            pl.BlockSpec((toks, LANES), lambda b, s: (b * ns + s, 0)),
            pl.BlockSpec((None, None, FOX_HEADS, 2 * DECAY_BLOCKS), lambda b, s: (b, s, 0, 0)),
            pl.BlockSpec((None, FOX_HEADS, toks), lambda b, s: (b, 0, s)),
        ],
        out_shape=[
            jax.ShapeDtypeStruct((batch * seq, LANES), BF16),
            jax.ShapeDtypeStruct((batch, ns, FOX_HEADS, 2 * DECAY_BLOCKS), F32),
            jax.ShapeDtypeStruct((batch, FOX_HEADS, seq), F32),
        ],
        scratch_shapes=[pltpu.VMEM((FOX_HEADS, 1), F32)],
        compiler_params=pltpu.CompilerParams(
            dimension_semantics=("arbitrary", "arbitrary")),
        name="decay",
    )(logf_t)


def _swa_kernel(sinks_ref, qt_ref, cur_ref, prev_ref, o_ref, ktok_sc):
    i = pl.program_id(1)
    nblk = TQ_SWA // WINDOW
    gw = SWA_GROUP * WINDOW
    kv_win = jnp.concatenate([prev_ref[...], cur_ref[...]], axis=1)
    ktok_sc[...] = kv_win[0:SWA_KV_W, :].astype(F32).T.astype(BF16)

    row = lax.broadcasted_iota(jnp.int32, (2 * WINDOW, gw), 0)
    qix = lax.broadcasted_iota(jnp.int32, (2 * WINDOW, gw), 1) % WINDOW
    allowed = (row > qix) & (row <= qix + WINDOW)
    bias_any = jnp.where(allowed, 0.0, NEG)
    bias_first = jnp.where(allowed & (row >= WINDOW), 0.0, NEG)
    bias_r0 = jnp.where(i == 0, bias_first, bias_any)
    zeros_q = jnp.zeros((HEAD_DIM, WINDOW), BF16)
    ones_rows = jnp.ones((SWA_DEN_ROWS, 2 * WINDOW), BF16)

    units = [(r, g) for r in range(nblk) for g in range(SWA_KV_HEADS)]

    def scores(unit):
        r, g = unit
        cols = slice(r * WINDOW, (r + 1) * WINDOW)
        tiles = []
        for hq in range(SWA_GROUP):
            h = g * SWA_GROUP + hq
            q_h = qt_ref[h * HEAD_DIM:(h + 1) * HEAD_DIM, cols]
            tiles.append(jnp.concatenate([q_h, zeros_q] if g == 0 else [zeros_q, q_h], axis=0))
        q_g = jnp.concatenate(tiles, axis=1)
        k_win = ktok_sc[r * WINDOW:(r + 2) * WINDOW, :]
        s = jnp.dot(k_win, q_g, preferred_element_type=F32)
        return s + (bias_r0 if r == 0 else bias_any)

    def finish(s, unit):
        r, g = unit
        sink = jnp.concatenate(
            [jnp.full((1, WINDOW), sinks_ref[g * SWA_GROUP + hq] * LOG2E, F32)
             for hq in range(SWA_GROUP)], axis=1)
        m = jnp.maximum(jnp.max(s, axis=0, keepdims=True), sink)
        pr = jnp.exp2(s - m).astype(BF16)
        v_rows = slice(SWA_KV_W + g * HEAD_DIM, SWA_KV_W + (g + 1) * HEAD_DIM)
        vt = jnp.concatenate([kv_win[v_rows, r * WINDOW:(r + 2) * WINDOW], ones_rows], axis=0)
        acc = jnp.dot(vt, pr, preferred_element_type=F32)
        den = acc[HEAD_DIM:HEAD_DIM + 1, :] + jnp.exp2(sink - m)
        o_t = acc[0:HEAD_DIM, :] * (1.0 / den)
        for pp in range(SWA_GROUP // 2):
            pair = jnp.concatenate([o_t[:, (2 * pp) * WINDOW:(2 * pp + 1) * WINDOW],
                                    o_t[:, (2 * pp + 1) * WINDOW:(2 * pp + 2) * WINDOW]], axis=0)
            c0 = (g * SWA_GROUP + 2 * pp) * HEAD_DIM
            o_ref[r * WINDOW:(r + 1) * WINDOW, c0:c0 + LANES] = pair.T.astype(BF16)

    pending = scores(units[0])
    for n, unit in enumerate(units):
        nxt = scores(units[n + 1]) if n + 1 < len(units) else None
        finish(pending, unit)
        pending = nxt


def _swa(qv_t, kv_t, sinks, batch, seq):
    nsq = seq // TQ_SWA
    per = TQ_SWA // WINDOW
    return pl.pallas_call(
        _swa_kernel,
        grid=(batch, nsq),
        in_specs=[
            pl.BlockSpec(memory_space=pltpu.SMEM),
            pl.BlockSpec((None, SWA_Q_W, TQ_SWA), lambda b, i: (b, ROW_AQ // SWA_Q_W, i)),
            pl.BlockSpec((None, 2 * SWA_KV_W, TQ_SWA), lambda b, i: (b, 0, i)),
            pl.BlockSpec((None, 2 * SWA_KV_W, WINDOW),
                         lambda b, i: (b, 0, jnp.maximum(i * per - 1, 0))),
        ],
        out_specs=pl.BlockSpec((TQ_SWA, SWA_Q_W), lambda b, i: (b * nsq + i, 0)),
        scratch_shapes=[pltpu.VMEM((WINDOW + TQ_SWA, SWA_KV_W), BF16)],
        out_shape=jax.ShapeDtypeStruct((batch * seq, SWA_Q_W), BF16),
        compiler_params=pltpu.CompilerParams(
            dimension_semantics=("arbitrary", "arbitrary")),
        name="swa",
    )(sinks, qv_t, kv_t, kv_t)


def _fox_kernel(base_ref, q_ref, k_ref, aug_ref, vt_ref, dec_ref, o_ref,
                m_sc, al_sc, be_sc, acc_sc, qt_sc, bm_sc, jump_sc, *bufs):
    b = pl.program_id(0)
    p = pl.program_id(1)
    qi = pl.program_id(2)
    s_bufs, p_bufs = bufs[:2], bufs[2:]
    dz = pl.multiple_of(jnp.minimum(qi, 0), FOX_ROWS)
    nb = k_ref.shape[0] // TK_FOX
    per = TQ_FOX // TK_FOX
    q_t = q_ref[...]
    feat = lax.broadcasted_iota(jnp.int32, (LANES, TQ_FOX), 0)
    zero = jnp.zeros_like(q_t)
    for hh in range(2):
        h = 2 * p + hh
        sel = (feat == h) | (feat == h + FOX_HEADS) | (feat == h + 2 * FOX_HEADS)
        qt_sc[hh, 0:LANES, :] = jnp.where((feat // HEAD_DIM) == hh, q_t, zero)
        qt_sc[hh, LANES:2 * LANES, :] = jnp.where(sel, 1.0, 0.0).astype(BF16)

    ones_rows = jnp.ones((FOX_DEN_ROWS, TK_FOX), BF16)

    def reset():
        m_sc[...] = jnp.full(m_sc.shape, NEG, F32)
        acc_sc[...] = jnp.zeros(acc_sc.shape, F32)

    def pieces(q_lo):
        step = TQ_FOX // FOX_PIECES
        return [(max(q_lo, u * step), (u + 1) * step) for u in range(FOX_PIECES)
                if (u + 1) * step > q_lo]

    def key_rows(key0):
        return pl.ds(pl.multiple_of(key0, TK_FOX), TK_FOX)

    def base_of(kblk, hh, span=0):
        return base_ref[((b * nb + kblk) * FOX_HEADS + 2 * p + hh) * 2 + span]

    def qk(key0, hh, lo, hi, limit):
        ks = key_rows(key0)
        k_full = jnp.concatenate([k_ref[ks, :], aug_ref[ks, :]], axis=1)
        s = jnp.dot(k_full, qt_sc[hh, :, lo:hi], preferred_element_type=F32)
        if limit is not None:
            row = lax.broadcasted_iota(jnp.int32, s.shape, 0)
            col = lax.broadcasted_iota(jnp.int32, s.shape, 1)
            s = jnp.where(row - col <= limit, s, NEG)
        return s

    def v_aug(key0, hh):
        return jnp.concatenate(
            [vt_ref[hh * HEAD_DIM:(hh + 1) * HEAD_DIM, key_rows(key0)], ones_rows], axis=0)

    def two_pass_probs(key0, kblk, hh, limit_of, sb, pb):
        for lo, hi in pieces(0):
            s = qk(key0, hh, lo, hi, limit_of(lo))
            s_bufs[sb][pl.ds(dz, TK_FOX), lo:hi] = s
            bm_sc[sb, :, lo:hi] = jnp.max(s, axis=0, keepdims=True)
        base = base_of(kblk, hh)
        m_prev = m_sc[hh] - base
        m_new = jnp.maximum(m_prev, bm_sc[sb])
        al_sc[pb] = jnp.exp2(m_prev - m_new)
        m_sc[hh] = m_new + base
        for r in range(0, TK_FOX, FOX_ROWS):
            rows = pl.ds(dz + r, FOX_ROWS)
            pr = jnp.exp2(s_bufs[sb][rows, :] - m_new)
            p_bufs[pb][rows, :] = pr.astype(BF16)

    def chain_keys(blk, n):
        return blk * TQ_FOX + (n // 2) * TK_FOX, blk * per + n // 2

    def limit_fn(q_lo, diag):
        if diag is None:
            return lambda lo: None
        if diag is True:
            return lambda lo: lo - q_lo
        return lambda lo: jnp.where(diag, lo - q_lo, TK_FOX)

    def one_pass_piece(blk, n, q_lo, diag, ref, lo, hi):
        key0, _ = chain_keys(blk, n)
        s = qk(key0, n % 2, lo, hi, limit_fn(q_lo, diag)(lo))
        p_bufs[n % FOX_SLOTS][pl.ds(dz, TK_FOX), lo:hi] = jnp.exp2(
            s - ref[:, lo - q_lo:hi - q_lo]).astype(BF16)
        return jnp.max(s, axis=0, keepdims=True)

    def one_pass_finish(n, q_lo, m_prev, ref, base, maxes):
        hh, slot = n % 2, n % FOX_SLOTS
        blk_max = jnp.concatenate(maxes, axis=1) if len(maxes) > 1 else maxes[0]
        m_new = jnp.maximum(m_prev, blk_max)
        al_sc[slot, :, q_lo:] = jnp.exp2(m_prev - m_new)
        be_sc[slot, :, q_lo:] = jnp.exp2(ref - m_new)
        m_sc[hh, :, q_lo:] = m_new + base
        jump_sc[hh, :, q_lo:] = jnp.maximum(jump_sc[hh, :, q_lo:], jnp.abs(blk_max - ref))

    def pv_piece(blk, n, lo, hi):
        hh, slot = n % 2, n % FOX_SLOTS
        key0, _ = chain_keys(blk, n)
        pv = jnp.dot(v_aug(key0, hh), p_bufs[slot][pl.ds(dz, TK_FOX), lo:hi],
                     preferred_element_type=F32)[0:FOX_ACC_ROWS]
        acc_sc[hh, :, lo:hi] = (acc_sc[hh, :, lo:hi] * al_sc[slot, :, lo:hi]
                                + pv * be_sc[slot, :, lo:hi])

    def chain_step(blk, n, q_lo, ahead):
        mine = pieces(q_lo)
        theirs = []
        if ahead is not None:
            a_blk, a_n, a_lo, a_diag, a_first = ahead
            theirs = pieces(a_lo)
            a_kblk = chain_keys(a_blk, a_n)[1]
            a_base = base_of(a_kblk, a_n % 2)
            if a_first:
                a_prev = jnp.full((1, TQ_FOX - a_lo), NEG, F32)
                ref0 = 0.0
            else:
                a_prev = m_sc[a_n % 2, :, a_lo:] - a_base
                ref0 = a_prev
            span = base_of(a_kblk, a_n % 2, span=1)
            if a_diag is None:
                a_ref = ref0 + span
            else:
                reach = dec_ref[a_n % 2:a_n % 2 + 1, key_rows(chain_keys(a_blk, a_n)[0])]
                if TQ_FOX - a_lo > TK_FOX:
                    reach = jnp.concatenate(
                        [reach, jnp.full((1, TQ_FOX - a_lo - TK_FOX), span, F32)], axis=1)
                a_ref = ref0 + (reach if a_diag is True else jnp.where(a_diag, reach, span))
        maxes = []
        for u in range(max(len(mine), len(theirs))):
            if u < len(theirs):
                maxes.append(one_pass_piece(a_blk, a_n, a_lo, a_diag, a_ref, *theirs[u]))
            if u < len(mine):
                pv_piece(blk, n, *mine[u])
        if ahead is not None:
            one_pass_finish(a_n, a_lo, a_prev, a_ref, a_base, maxes)

    def fast_path():
        reset()
        jump_sc[...] = jnp.full(jump_sc.shape, NEG, F32)
        for n in range(FOX_AHEAD):
            chain_step(0, n, TQ_FOX, (0, n, 0, qi == 0, True))

        def body(j, carry):
            for n in range(FOX_CHAINS):
                m = n + FOX_AHEAD
                if m < FOX_CHAINS:
                    ahead = (j, m, 0, None, False)
                else:
                    ahead = (j + 1, m - FOX_CHAINS, 0, j + 1 == qi, False)
                chain_step(j, n, 0, ahead)
            return carry

        lax.fori_loop(0, qi, body, 0)
        for n in range(FOX_CHAINS):
            m = n + FOX_AHEAD
            ahead = (qi, m, (m // 2) * TK_FOX, True, False) if m < FOX_CHAINS else None
            chain_step(qi, n, (n // 2) * TK_FOX, ahead)

    def safe_path():
        reset()

        def body(t, carry):
            blk, c = t // per, t % per
            limit = jnp.where(blk == qi, -c * TK_FOX, TK_FOX)
            for hh in range(2):
                two_pass_probs(t * TK_FOX, t, hh, lambda lo: limit + lo, hh, hh)
                pv = jnp.dot(v_aug(t * TK_FOX, hh), p_bufs[hh][pl.ds(dz, TK_FOX), :],
                             preferred_element_type=F32)
                acc_sc[hh] = acc_sc[hh] * al_sc[hh] + pv[0:FOX_ACC_ROWS]
            return carry

        lax.fori_loop(0, (qi + 1) * per, body, 0)

    fast_path()

    @pl.when(jnp.max(jump_sc[...]) > FOX_MAX_JUMP)
    def _():
        safe_path()


    o_t = jnp.concatenate(
        [acc_sc[hh, 0:HEAD_DIM, :] * (1.0 / acc_sc[hh, HEAD_DIM:HEAD_DIM + 1, :]) for hh in range(2)],
        axis=0)
    o_ref[...] = o_t.T.astype(BF16)


def _fox(proj, aug, base, dec_t, qv_t, batch, seq):
    t = proj.shape[0]
    nq = seq // TQ_FOX
    pairs = FOX_HEADS // 2
    return pl.pallas_call(
        _fox_kernel,
        grid=(batch, pairs, nq),
        in_specs=[
            pl.BlockSpec(memory_space=pltpu.SMEM),
            pl.BlockSpec((None, LANES, TQ_FOX), lambda b, p, i: (b, ROW_FQ // LANES + p, i)),
            pl.BlockSpec((seq, LANES), lambda b, p, i: (b, COL_FK // LANES + p)),
            pl.BlockSpec((seq, LANES), lambda b, p, i: (b, 0)),
            pl.BlockSpec((None, LANES, seq), lambda b, p, i: (b, ROW_FV // LANES + p, 0)),
            pl.BlockSpec((None, None, 2, seq), lambda b, p, i: (b, p, 0, 0)),
        ],
        out_specs=pl.BlockSpec((TQ_FOX, LANES), lambda b, p, i: (b * nq + i, p)),
        out_shape=jax.ShapeDtypeStruct((t, FOX_W), BF16),
        scratch_shapes=[
            pltpu.VMEM((2, 1, TQ_FOX), F32),
            pltpu.VMEM((FOX_SLOTS, 1, TQ_FOX), F32),
            pltpu.VMEM((FOX_SLOTS, 1, TQ_FOX), F32),
            pltpu.VMEM((2, FOX_ACC_ROWS, TQ_FOX), F32),
            pltpu.VMEM((2, 2 * LANES, TQ_FOX), BF16),
            pltpu.VMEM((2, 1, TQ_FOX), F32),
            pltpu.VMEM((2, 1, TQ_FOX), F32),
        ] + [pltpu.VMEM((TK_FOX, TQ_FOX), F32)] * 2
          + [pltpu.VMEM((TK_FOX, TQ_FOX), BF16)] * FOX_SLOTS,
        compiler_params=pltpu.CompilerParams(
            dimension_semantics=("arbitrary", "arbitrary", "arbitrary")),
        name="fox",
    )(base, qv_t, proj, aug, qv_t, dec_t.reshape(batch, pairs, 2, seq))


def _merge_kernel(oa_ref, ob_ref, ga_ref, gb_ref, x_ref, wa_ref, wb_ref, wo_ref, gain_ref,
                  x1_ref, h2_ref):
    ya = jnp.dot(oa_ref[...], wa_ref[...], preferred_element_type=F32)
    yb = jnp.dot(ob_ref[...], wb_ref[...], preferred_element_type=F32)

    def gate(ref):
        return 1.0 / (1.0 + jnp.exp(-ref[...].astype(F32)))

    merged = gate(ga_ref) * ya + gate(gb_ref) * yb
    x1 = x_ref[...] + jnp.dot(merged.astype(BF16), wo_ref[...], preferred_element_type=F32)
    x1_ref[...] = x1
    h2_ref[...] = _rmsnorm_rows(x1, gain_ref[...]).astype(BF16)


def _merge(o_a, o_b, proj, x2, wa, wb, wo, gain):
    t = x2.shape[0]
    tm = TM_MERGE
    const = dict(pipeline_mode=pl.Buffered(1))
    return pl.pallas_call(
        _merge_kernel,
        grid=(t // tm,),
        in_specs=[
            pl.BlockSpec((tm, SWA_Q_W), lambda i: (i, 0)),
            pl.BlockSpec((tm, FOX_W), lambda i: (i, 0)),
            pl.BlockSpec((tm, D_MODEL), lambda i: (i, COL_GA // D_MODEL)),
            pl.BlockSpec((tm, D_MODEL), lambda i: (i, COL_GB // D_MODEL)),
            pl.BlockSpec((tm, D_MODEL), lambda i: (i, 0)),
            pl.BlockSpec((SWA_Q_W, D_MODEL), lambda i: (0, 0), **const),
            pl.BlockSpec((FOX_W, D_MODEL), lambda i: (0, 0), **const),
            pl.BlockSpec((D_MODEL, D_MODEL), lambda i: (0, 0), **const),
            pl.BlockSpec((1, D_MODEL), lambda i: (0, 0), **const),
        ],
        out_specs=[
            pl.BlockSpec((tm, D_MODEL), lambda i: (i, 0)),
            pl.BlockSpec((tm, D_MODEL), lambda i: (i, 0)),
        ],
        out_shape=[
            jax.ShapeDtypeStruct((t, D_MODEL), F32),
            jax.ShapeDtypeStruct((t, D_MODEL), BF16),
        ],
        compiler_params=pltpu.CompilerParams(
            dimension_semantics=("arbitrary",),
            vmem_limit_bytes=56 * MIB),
        name="merge",
    )(o_a, o_b, proj, proj, x2, wa, wb, wo, gain)


def _mlp_kernel(h2_ref, x1_ref, wup_ref, wdn_ref, gain_ref, o_ref, ua_sc, ub_sc, *, nf, final_norm):
    f = pl.program_id(1)
    last = nf
    bufs = (ua_sc, ub_sc)

    def hidden():
        u = jnp.maximum(jnp.dot(h2_ref[...], wup_ref[...], preferred_element_type=F32), 0.0)
        return (u * u).astype(BF16)

    def down(src):
        return jnp.dot(src[...], wdn_ref[...], preferred_element_type=F32)

    @pl.when(f == 0)
    def _():
        ua_sc[...] = hidden()

    @pl.when(f == 1)
    def _():
        o_ref[...] = x1_ref[...] + down(ua_sc)
        ub_sc[...] = hidden()

    for parity in range(2):
        @pl.when((f > 1) & (f < last) & (f % 2 == parity))
        def _():
            o_ref[...] += down(bufs[1 - parity])
            bufs[parity][...] = hidden()

    @pl.when(f == last)
    def _():
        y = o_ref[...] + down(bufs[(nf - 1) % 2])
        o_ref[...] = _rmsnorm_rows(y, gain_ref[...]) if final_norm else y


def _mlp(h2, x1, wup, wdn, gain, final_norm):
    t = x1.shape[0]
    tm, tf = TM_MLP, TF_MLP
    nf = D_FF // tf
    assert nf >= 2
    return pl.pallas_call(
        functools.partial(_mlp_kernel, nf=nf, final_norm=final_norm),
        grid=(t // tm, nf + 1),
        in_specs=[
            pl.BlockSpec((tm, D_MODEL), lambda i, f: (i, 0)),
            pl.BlockSpec((tm, D_MODEL), lambda i, f: (i, 0)),
            pl.BlockSpec((D_MODEL, tf), lambda i, f: (0, jnp.minimum(f, nf - 1))),
            pl.BlockSpec((tf, D_MODEL), lambda i, f: (jnp.maximum(f - 1, 0), 0)),
            pl.BlockSpec((1, D_MODEL), lambda i, f: (0, 0)),
        ],
        out_specs=pl.BlockSpec((tm, D_MODEL), lambda i, f: (i, 0)),
        out_shape=jax.ShapeDtypeStruct((t, D_MODEL), F32),
        scratch_shapes=[pltpu.VMEM((tm, tf), BF16), pltpu.VMEM((tm, tf), BF16)],
        compiler_params=pltpu.CompilerParams(
            dimension_semantics=("arbitrary", "arbitrary"),
            vmem_limit_bytes=48 * MIB),
        name="mlp",
    )(h2, x1, wup, wdn, gain)


def _pack_w_in(w, scale):
    o = 0
    a_q = w[:, o:o + SWA_Q_W] * (scale * LOG2E); o += SWA_Q_W
    a_k = w[:, o:o + SWA_KV_W]; o += SWA_KV_W
    a_v = w[:, o:o + SWA_KV_W]; o += SWA_KV_W
    f_q = w[:, o:o + FOX_W] * (scale * LOG2E); o += FOX_W
    f_k = w[:, o:o + FOX_W]; o += FOX_W
    f_v = w[:, o:o + FOX_W]; o += FOX_W
    f_l = w[:, o:o + FOX_HEADS]; o += FOX_HEADS
    g_a = w[:, o:o + D_MODEL]; o += D_MODEL
    g_b = w[:, o:o + D_MODEL]; o += D_MODEL
    w_tok = jnp.concatenate([g_a, g_b, f_k], axis=1).astype(BF16)
    w_qv = jnp.concatenate([a_q, f_v, f_q], axis=1).T.astype(BF16)
    w_kv = jnp.concatenate([a_k, a_v, f_l], axis=1).T.astype(BF16)
    return w_tok, w_qv, w_kv


def kernel(x, positions, attn_norm, w_in, fox_f_bias, swa_sinks, w_branch_swa, w_branch_fox,
           w_out, mlp_norm, w_up, w_down, final_norm):
    batch, seq, d = x.shape
    depth = w_in.shape[0]
    assert d == D_MODEL and seq % TM_IN == 0 and seq % TQ_FOX == 0
    t = batch * seq
    scale = HEAD_DIM ** -0.5
    inv_freq = ROPE_THETA ** (-jnp.arange(0, HEAD_DIM, 2, dtype=F32) / HEAD_DIM)
    invf = inv_freq[:, None]
    pos3 = positions.reshape(t // TM_IN, 1, TM_IN)
    x2 = x.reshape(t, d)
    for l in range(depth):
        w_tok, w_qv, w_kv = _pack_w_in(w_in[l], scale)
        fbias = fox_f_bias[l].astype(F32)[:, None]
        proj, logf_t, qv_t, kv_t = _inproj(x2, pos3, invf, attn_norm[l][None, :].astype(F32),
                                           w_tok, w_qv, w_kv, fbias, batch, seq)
        aug, base, dec_t = _decay(logf_t, batch, seq)
        base = base.reshape(batch, -1, FOX_HEADS, 2, DECAY_BLOCKS).transpose(0, 1, 4, 2, 3).reshape(-1)
        o_a = _swa(qv_t, kv_t, swa_sinks[l].astype(F32), batch, seq)
        o_b = _fox(proj, aug, base, dec_t, qv_t, batch, seq)
        x1, h2 = _merge(o_a, o_b, proj, x2,
                        w_branch_swa[l].astype(BF16), w_branch_fox[l].astype(BF16),
                        w_out[l].astype(BF16), mlp_norm[l][None, :].astype(F32))
        x2 = _mlp(h2, x1, w_up[l].astype(BF16), w_down[l].astype(BF16),
                  final_norm[None, :].astype(F32), final_norm=(l == depth - 1))
    return x2.reshape(batch, seq, d)
```
